```python
import math
import jax, jax.numpy as jnp
from jax import lax
import numpy as np

D_MODEL = 1024
BATCH = 16
SEQ = 2048
DEPTH = 2

MEM_LEN = 256
N_MOBA_LAYERS = (DEPTH + 1) // 2
N_SSD_LAYERS = DEPTH // 2

MOBA_HEADS = 12
MOBA_HD = 128
MOBA_WIDTH = MOBA_HEADS * MOBA_HD
MOBA_BLOCK = 256
MOBA_TOPK = 3
MOBA_QCHUNK = 128

REL_BUCKETS = 32
REL_MAX_DIST = 128

MEM_HEADS = 4
MEM_HD = 128
MEM_WIDTH = MEM_HEADS * MEM_HD

SSD_HEADS = 24
SSD_HD = 64
SSD_INNER = SSD_HEADS * SSD_HD
SSD_GROUPS = 4
SSD_STATE = 128
SSD_CONV = 4
SSD_CHUNK = 128
SSD_CONV_DIM = SSD_INNER + 2 * SSD_GROUPS * SSD_STATE

MIX_WIDTH = MOBA_WIDTH + MEM_WIDTH
IN_MOBA = 3 * MOBA_WIDTH + MEM_WIDTH
IN_SSD = SSD_INNER + SSD_CONV_DIM + SSD_HEADS + MEM_WIDTH

D_FF = 3584
N_EXPERTS = 8
TOP_K = 2
EXP_CHUNK = 256

EPS = 1e-6

kernel_name = 'hybrid_moba_ssd_moe_block'


def rms_norm(x, g):
    xf = x.astype(jnp.float32)
    y = xf * lax.rsqrt(jnp.mean(xf * xf, axis=-1, keepdims=True) + EPS)
    return (y * g.astype(jnp.float32)).astype(x.dtype)


def t5_bucket(dist):
    n = jnp.maximum(dist, 0)
    max_exact = REL_BUCKETS // 2
    large = max_exact + (jnp.log(jnp.maximum(n, 1).astype(jnp.float32) / max_exact)
                         / math.log(REL_MAX_DIST / max_exact)
                         * (REL_BUCKETS - max_exact)).astype(jnp.int32)
    large = jnp.minimum(large, REL_BUCKETS - 1)
    return jnp.where(n < max_exact, n, large)


def bucket_layout(ids, valid, n_buckets, chunk, n_buf):
    n_items = ids.shape[0]
    oh = ((ids[:, None] == jnp.arange(n_buckets)[None, :]) & valid[:, None]).astype(jnp.int32)
    counts = oh.sum(axis=0)
    rank = jnp.take_along_axis(jnp.cumsum(oh, axis=0) - oh, ids[:, None], axis=1)[:, 0]
    padded = (counts + chunk - 1) // chunk * chunk
    ends = jnp.cumsum(padded)
    starts = ends - padded
    dest = jnp.where(valid, starts[ids] + rank, n_buf)
    src = jnp.full((n_buf,), n_items, jnp.int32).at[dest].set(jnp.arange(n_items, dtype=jnp.int32), mode='drop')
    chunk_start = jnp.arange(n_buf // chunk) * chunk
    chunk_ids = jnp.minimum(jnp.sum(chunk_start[:, None] >= ends[None, :], axis=1), n_buckets - 1)
    return dest, src, chunk_ids


def moba_head(q, k, v, bias_h):
    s_pad = q.shape[0]
    nb = s_pad // MOBA_BLOCK
    n_sel = min(MOBA_TOPK, nb - 1)
    scale = MOBA_HD ** -0.5
    qb = q.reshape(nb, MOBA_BLOCK, MOBA_HD)
    kb = k.reshape(nb, MOBA_BLOCK, MOBA_HD)
    vb = v.reshape(nb, MOBA_BLOCK, MOBA_HD)
    loc = jnp.arange(MOBA_BLOCK)
    dist_self = loc[:, None] - loc[None, :]
    s_self = jnp.einsum('nqd,nkd->nqk', qb, kb).astype(jnp.float32) * scale + bias_h[t5_bucket(dist_self)]
    s_self = jnp.where(dist_self >= 0, s_self, -jnp.inf)
    lse_self = jax.nn.logsumexp(s_self, axis=-1)
    o_self = jnp.einsum('nqk,nkd->nqd', jnp.exp(s_self - lse_self[..., None]).astype(v.dtype), vb)
    o_self = o_self.reshape(s_pad, MOBA_HD)
    if n_sel == 0:
        return o_self
    lse_self = lse_self.reshape(s_pad)
    pos = jnp.arange(s_pad)
    k_mean = kb.mean(axis=1)
    gate = (q @ k_mean.T).astype(jnp.float32)
    past = jnp.arange(nb)[None, :] < (pos // MOBA_BLOCK)[:, None]
    gate = jnp.where(past, gate, -jnp.inf)
    top_s, top_i = lax.top_k(gate, n_sel)
    valid = jnp.isfinite(top_s).reshape(-1)
    ids = top_i.reshape(-1).astype(jnp.int32)
    pair_q = jnp.repeat(pos, n_sel).astype(jnp.int32)
    n_buf = s_pad * n_sel + nb * MOBA_QCHUNK
    n_ch = n_buf // MOBA_QCHUNK
    dest, src, cid = bucket_layout(ids, valid, nb, MOBA_QCHUNK, n_buf)
    buf_q = jnp.concatenate([pair_q, jnp.zeros((1,), jnp.int32)])[src]
    qg = q[buf_q].reshape(n_ch, MOBA_QCHUNK, MOBA_HD)
    kg = kb[cid]
    vg = vb[cid]
    kpos = cid[:, None] * MOBA_BLOCK + loc[None, :]
    dist = buf_q.reshape(n_ch, MOBA_QCHUNK)[:, :, None] - kpos[:, None, :]
    s_r = jnp.einsum('cqd,ckd->cqk', qg, kg).astype(jnp.float32) * scale + bias_h[t5_bucket(dist)]
    lse_r = jax.nn.logsumexp(s_r, axis=-1)
    o_r = jnp.einsum('cqk,ckd->cqd', jnp.exp(s_r - lse_r[..., None]).astype(v.dtype), vg)
    o_r = o_r.reshape(n_buf, MOBA_HD)
    take = jnp.minimum(dest, n_buf - 1)
    o_pair = o_r[take].reshape(s_pad, n_sel, MOBA_HD)
    lse_pair = jnp.where(valid, lse_r.reshape(n_buf)[take], -jnp.inf).reshape(s_pad, n_sel)
    w = jax.nn.softmax(jnp.concatenate([lse_self[:, None], lse_pair], axis=1), axis=-1)
    o_all = jnp.concatenate([o_self[:, None], o_pair], axis=1)
    return jnp.einsum('sj,sjd->sd', w.astype(o_all.dtype), o_all)


def moba_attention(qkv, rel_bias):
    bsz, s, _ = qkv.shape
    s_pad = -(-s // MOBA_BLOCK) * MOBA_BLOCK
    q, k, v = jnp.split(qkv, 3, axis=-1)

    def to_heads(t):
        t = t.reshape(bsz, s, MOBA_HEADS, MOBA_HD).transpose(0, 2, 1, 3)
        return jnp.pad(t, ((0, 0), (0, 0), (0, s_pad - s), (0, 0)))

    q, k, v = to_heads(q), to_heads(k), to_heads(v)
    bias_heads = rel_bias.T
    per_head = jax.vmap(moba_head, in_axes=(0, 0, 0, 0))

    def per_batch(args):
        qb, kb, vb = args
        return per_head(qb, kb, vb, bias_heads)

    o = lax.map(per_batch, (q, k, v))
    return o[:, :, :s].transpose(0, 2, 1, 3).reshape(bsz, s, MOBA_WIDTH)


def memory_cross_attention(q, mem_n, w_kv):
    bsz, s, _ = q.shape
    kv = mem_n @ w_kv
    k, v = jnp.split(kv, 2, axis=-1)
    q = q.reshape(bsz, s, MEM_HEADS, MEM_HD)
    k = k.reshape(bsz, -1, MEM_HEADS, MEM_HD)
    v = v.reshape(bsz, -1, MEM_HEADS, MEM_HD)
    sc = jnp.einsum('bshd,bmhd->bhsm', q, k).astype(jnp.float32) * (MEM_HD ** -0.5)
    p = jax.nn.softmax(sc, axis=-1).astype(v.dtype)
    return jnp.einsum('bhsm,bmhd->bshd', p, v).reshape(bsz, s, MEM_WIDTH)


def causal_dwconv(x, w, b):
    ksz = w.shape[0]
    y = lax.conv_general_dilated(x, w[:, None, :].astype(x.dtype), window_strides=(1,),
                                 padding=[(ksz - 1, 0)], dimension_numbers=('NWC', 'WIO', 'NWC'),
                                 feature_group_count=x.shape[-1])
    return y + b.astype(x.dtype)


def ssd_scan(x, dt, a, b_in, c_in):
    bsz, s = x.shape[0], x.shape[1]
    s_pad = -(-s // SSD_CHUNK) * SSD_CHUNK

    def pad(t):
        return jnp.pad(t, ((0, 0), (0, s_pad - s)) + ((0, 0),) * (t.ndim - 2))

    x, dt, b_in, c_in = pad(x), pad(dt), pad(b_in), pad(c_in)
    nc = s_pad // SSD_CHUNK
    e = SSD_HEADS // SSD_GROUPS
    xdt = (x * dt[..., None]).reshape(bsz, nc, SSD_CHUNK, SSD_GROUPS, e, SSD_HD)
    la = (dt * a).reshape(bsz, nc, SSD_CHUNK, SSD_GROUPS, e)
    la_cs = jnp.cumsum(la, axis=2)
    bc = b_in.reshape(bsz, nc, SSD_CHUNK, SSD_GROUPS, SSD_STATE)
    cc = c_in.reshape(bsz, nc, SSD_CHUNK, SSD_GROUPS, SSD_STATE)
    causal = (jnp.arange(SSD_CHUNK)[:, None] >= jnp.arange(SSD_CHUNK)[None, :])[None, None, :, :, None, None]
    decay = jnp.exp(jnp.where(causal, la_cs[:, :, :, None] - la_cs[:, :, None], -jnp.inf))
    scores = jnp.einsum('bclgn,bcsgn->bclsg', cc, bc)[..., None] * decay
    y_diag = jnp.einsum('bclsge,bcsgep->bclgep', scores, xdt)
    decay_to_end = jnp.exp(la_cs[:, :, -1:] - la_cs)
    states = jnp.einsum('bclgn,bclge,bclgep->bcgepn', bc, decay_to_end, xdt)
    chunk_decay = jnp.exp(la_cs[:, :, -1])

    def step(h, inp):
        st, dec = inp
        return dec[..., None, None] * h + st, h

    h0 = jnp.zeros((bsz, SSD_GROUPS, e, SSD_HD, SSD_STATE), jnp.float32)
    _, h_prev = lax.scan(step, h0, (states.transpose(1, 0, 2, 3, 4, 5), chunk_decay.transpose(1, 0, 2, 3)))
    h_prev = h_prev.transpose(1, 0, 2, 3, 4, 5)
    y_off = jnp.einsum('bclgn,bcgepn,bclge->bclgep', cc, h_prev, jnp.exp(la_cs))
    y = (y_diag + y_off).reshape(bsz, s_pad, SSD_HEADS, SSD_HD)
    return y[:, :s]


def ssd_mixer(zxbcdt, conv_w, conv_b, dt_bias, a_log, d_skip, g_out):
    bsz, s, _ = zxbcdt.shape
    z = zxbcdt[..., :SSD_INNER]
    xbc = zxbcdt[..., SSD_INNER:SSD_INNER + SSD_CONV_DIM]
    dt_raw = zxbcdt[..., SSD_INNER + SSD_CONV_DIM:]
    xbc = jax.nn.silu(causal_dwconv(xbc, conv_w, conv_b)).astype(jnp.float32)
    xs = xbc[..., :SSD_INNER]
    b_in = xbc[..., SSD_INNER:SSD_INNER + SSD_GROUPS * SSD_STATE].reshape(bsz, s, SSD_GROUPS, SSD_STATE)
    c_in = xbc[..., SSD_INNER + SSD_GROUPS * SSD_STATE:].reshape(bsz, s, SSD_GROUPS, SSD_STATE)
    dt = jax.nn.softplus(dt_raw.astype(jnp.float32) + dt_bias.astype(jnp.float32))
    a = -jnp.exp(a_log.astype(jnp.float32))
    xh = xs.reshape(bsz, s, SSD_HEADS, SSD_HD)
    y = ssd_scan(xh, dt, a, b_in, c_in) + xh * d_skip.astype(jnp.float32)[:, None]
    u = (y.reshape(bsz, s, SSD_INNER) * jax.nn.silu(z.astype(jnp.float32))).reshape(bsz, s, SSD_GROUPS, -1)
    u = u * lax.rsqrt(jnp.mean(u * u, axis=-1, keepdims=True) + EPS)
    return (u.reshape(bsz, s, SSD_INNER) * g_out.astype(jnp.float32)).astype(zxbcdt.dtype)


def swiglu(h, w_gate, w_up, w_down):
    return (jax.nn.silu(h @ w_gate) * (h @ w_up)) @ w_down


def moe_swiglu(h, w_router, b_router, w_gate, w_up, w_down):
    bsz, s, d = h.shape
    n_tok = bsz * s
    xf = h.reshape(n_tok, d)
    logits = (xf @ w_router).astype(jnp.float32) + b_router.astype(jnp.float32)
    top_l, top_e = lax.top_k(logits, TOP_K)
    gates = jax.nn.softmax(top_l, axis=-1)
    ids = top_e.reshape(-1).astype(jnp.int32)
    tok = jnp.repeat(jnp.arange(n_tok, dtype=jnp.int32), TOP_K)
    n_buf = n_tok * TOP_K + N_EXPERTS * EXP_CHUNK
    dest, src, cid = bucket_layout(ids, jnp.ones_like(ids, dtype=bool), N_EXPERTS, EXP_CHUNK, n_buf)
    buf_tok = jnp.concatenate([tok, jnp.zeros((1,), jnp.int32)])[src]
    xb = xf[buf_tok].reshape(n_buf // EXP_CHUNK, EXP_CHUNK, d)

    def expert_chunk(args):
        xc, e = args
        return (jax.nn.silu(xc @ w_gate[e]) * (xc @ w_up[e])) @ w_down[e]

    yb = lax.map(expert_chunk, (xb, cid)).reshape(n_buf, d)
    y_pair = yb[dest].reshape(n_tok, TOP_K, d)
    y = jnp.einsum('tk,tkd->td', gates.astype(y_pair.dtype), y_pair)
    return y.reshape(bsz, s, d)


def setup_inputs(seed: int = 0) -> dict:
    key = jax.random.key(seed)
    ks = iter(jax.random.split(key, 32))
    f32 = jnp.float32

    def nrm(shape, scale):
        return jax.random.normal(next(ks), shape, f32) * scale

    def gain(shape):
        return jnp.ones(shape, f32) + nrm(shape, 0.02)

    dt0 = jnp.exp(jax.random.uniform(next(ks), (N_SSD_LAYERS, SSD_HEADS), f32)
                  * (math.log(0.1) - math.log(0.001)) + math.log(0.001))
    return {
        'x': nrm((BATCH, SEQ, D_MODEL), 1.0),
        'mem': nrm((BATCH, MEM_LEN, D_MODEL), 1.0),
        'g_mix': gain((DEPTH, D_MODEL)),
        'g_mem': gain((DEPTH, D_MODEL)),
        'w_in_moba': nrm((N_MOBA_LAYERS, D_MODEL, IN_MOBA), D_MODEL ** -0.5),
        'w_in_ssd': nrm((N_SSD_LAYERS, D_MODEL, IN_SSD), D_MODEL ** -0.5),
        'w_mem_kv': nrm((DEPTH, D_MODEL, 2 * MEM_WIDTH), D_MODEL ** -0.5),
        'w_out': nrm((DEPTH, MIX_WIDTH, D_MODEL), MIX_WIDTH ** -0.5),
        'rel_bias': nrm((REL_BUCKETS, MOBA_HEADS), 0.2),
        'conv_w': nrm((N_SSD_LAYERS, SSD_CONV, SSD_CONV_DIM), SSD_CONV ** -0.5),
        'conv_b': nrm((N_SSD_LAYERS, SSD_CONV_DIM), 0.02),
        'dt_bias': dt0 + jnp.log(-jnp.expm1(-dt0)),
        'a_log': jnp.log(jax.random.uniform(next(ks), (N_SSD_LAYERS, SSD_HEADS), f32, 1.0, 16.0)),
        'd_skip': jnp.ones((N_SSD_LAYERS, SSD_HEADS), f32) + nrm((N_SSD_LAYERS, SSD_HEADS), 0.1),
        'g_ssd_out': gain((N_SSD_LAYERS, SSD_INNER)),
        'g_ffn': gain((DEPTH, D_MODEL)),
        'w_ffn_gate': nrm((N_MOBA_LAYERS, D_MODEL, D_FF), D_MODEL ** -0.5),
        'w_ffn_up': nrm((N_MOBA_LAYERS, D_MODEL, D_FF), D_MODEL ** -0.5),
        'w_ffn_down': nrm((N_MOBA_LAYERS, D_FF, D_MODEL), D_FF ** -0.5),
        'w_router': nrm((N_SSD_LAYERS, D_MODEL, N_EXPERTS), D_MODEL ** -0.5),
        'b_router': nrm((N_SSD_LAYERS, N_EXPERTS), 0.01),
        'w_exp_gate': nrm((N_SSD_LAYERS, N_EXPERTS, D_MODEL, D_FF), D_MODEL ** -0.5),
        'w_exp_up': nrm((N_SSD_LAYERS, N_EXPERTS, D_MODEL, D_FF), D_MODEL ** -0.5),
        'w_exp_down': nrm((N_SSD_LAYERS, N_EXPERTS, D_FF, D_MODEL), D_FF ** -0.5),
        'g_final': gain((D_MODEL,)),
    }


def reference(x, mem, g_mix, g_mem, w_in_moba, w_in_ssd, w_mem_kv, w_out, rel_bias, conv_w, conv_b,
              dt_bias, a_log, d_skip, g_ssd_out, g_ffn, w_ffn_gate, w_ffn_up, w_ffn_down, w_router,
              b_router, w_exp_gate, w_exp_up, w_exp_down, g_final):
    for i in range(DEPTH):
        j = i // 2
        h = rms_norm(x, g_mix[i])
        mem_n = rms_norm(mem, g_mem[i])
        if i % 2 == 0:
            proj = h @ w_in_moba[j]
            y_tok = moba_attention(proj[..., :3 * MOBA_WIDTH], rel_bias)
            q_mem = proj[..., 3 * MOBA_WIDTH:]
        else:
            proj = h @ w_in_ssd[j]
            y_tok = ssd_mixer(proj[..., :IN_SSD - MEM_WIDTH], conv_w[j], conv_b[j], dt_bias[j],
                              a_log[j], d_skip[j], g_ssd_out[j])
            q_mem = proj[..., IN_SSD - MEM_WIDTH:]
        y_mem = memory_cross_attention(q_mem, mem_n, w_mem_kv[i])
        x = x + jnp.concatenate([y_tok, y_mem], axis=-1) @ w_out[i]
        h = rms_norm(x, g_ffn[i])
        if i % 2 == 0:
            x = x + swiglu(h, w_ffn_gate[j], w_ffn_up[j], w_ffn_down[j])
        else:
            x = x + moe_swiglu(h, w_router[j], b_router[j], w_exp_gate[j], w_exp_up[j], w_exp_down[j])
    return rms_norm(x, g_final)
```

```python
import functools
import math

import jax
import jax.numpy as jnp
from jax import lax
from jax.experimental import pallas as pl
from jax.experimental.pallas import tpu as pltpu

F32 = jnp.float32
BF16 = jnp.bfloat16

D_MODEL = 1024
HD = 128
MOBA_HEADS = 12
MOBA_WIDTH = MOBA_HEADS * HD
MOBA_BLOCK = 256
MOBA_TOPK = 3
REL_BUCKETS = 32
REL_MAX_DIST = 128
MEM_HEADS = 4
MEM_WIDTH = MEM_HEADS * HD
SSD_HEADS = 24
SSD_HD = 64
SSD_INNER = SSD_HEADS * SSD_HD
SSD_GROUPS = 4
SSD_STATE = 128
SSD_CONV = 4
SSD_CHUNK = 128
SSD_GHEADS = SSD_HEADS // SSD_GROUPS
SSD_GW = SSD_GHEADS * SSD_HD
SSD_CONV_GW = SSD_GW + 2 * SSD_STATE
D_FF = 3584
N_EXPERTS = 8
TOP_K = 2
EPS = 1e-6

LANES = 128
VMEM_LIMIT = 48 * 1024 * 1024
EXPERT_ROWS = 1024


def _dot(a, b):
    return jnp.dot(a, b, preferred_element_type=F32)


def _dot_nt(a, b):
    return lax.dot_general(a, b, (((1,), (1,)), ((), ())), preferred_element_type=F32)


def _split3(x):
    p0 = x.astype(BF16)
    r0 = x - p0.astype(F32)
    p1 = r0.astype(BF16)
    p2 = (r0 - p1.astype(F32)).astype(BF16)
    return p0, p1, p2


def _dot3l(x, w):
    return sum(_dot(p, w) for p in _split3(x))


def _dot3r(w, x):
    return sum(_dot(w, p) for p in _split3(x))


def _sigmoid(x):
    return 1.0 / (1.0 + jnp.exp(-x))


def _rms(x, g):
    ms = jnp.mean(x * x, axis=-1, keepdims=True)
    return x * lax.rsqrt(ms + EPS) * g


def _rms_rows_to(h_ref, x_ref, g_ref, rows):
    step = min(rows, 256)
    for r in range(0, rows, step):
        h_ref[r:r + step, :] = _rms(x_ref[r:r + step, :], g_ref[...]).astype(h_ref.dtype)


def _params(*sem):
    return pltpu.CompilerParams(dimension_semantics=sem, vmem_limit_bytes=VMEM_LIMIT)


def _normmm_kernel(x_ref, g_ref, w_ref, o_ref, h_ref, *, tm):
    @pl.when(pl.program_id(1) == 0)
    def _():
        _rms_rows_to(h_ref, x_ref, g_ref, tm)

    o_ref[...] = _dot(h_ref[...], w_ref[...]).astype(o_ref.dtype)


def norm_matmul(x, g, w, out_dtype, tm, tn):
    t, d = x.shape
    n = w.shape[1]
    return pl.pallas_call(
        functools.partial(_normmm_kernel, tm=tm),
        grid=(t // tm, n // tn),
        in_specs=[pl.BlockSpec((tm, d), lambda i, j: (i, 0)),
                  pl.BlockSpec((1, d), lambda i, j: (0, 0)),
                  pl.BlockSpec((d, tn), lambda i, j: (0, j))],
        out_specs=pl.BlockSpec((tm, tn), lambda i, j: (i, j)),
        out_shape=jax.ShapeDtypeStruct((t, n), out_dtype),
        scratch_shapes=[pltpu.VMEM((tm, d), BF16)],
        compiler_params=_params("parallel", "arbitrary"),
        name="norm_matmul",
    )(x, g.reshape(1, d), w)


def _moba_kernel(q_ref, k_ref, v_ref, tab_ref, o_ref, vt_ref, km_ref, sel_ref, *, nb, n_sel):
    blk = MOBA_BLOCK
    scale = HD ** -0.5
    nbp = km_ref.shape[0]
    km_ref[...] = jnp.zeros_like(km_ref)
    for j in range(nb):
        rows = slice(j * blk, (j + 1) * blk)
        vt_ref[j] = v_ref[rows, :].astype(F32).T.astype(BF16)
        km_ref[j:j + 1, :] = jnp.mean(k_ref[rows, :].astype(F32), axis=0, keepdims=True)
    km = km_ref[...]
    km_hi = km.astype(BF16)
    km_lo = (km - km_hi.astype(F32)).astype(BF16)
    q_all = q_ref[...]
    gate = _dot_nt(km_hi, q_all) + _dot_nt(km_lo, q_all)
    sub = lax.broadcasted_iota(jnp.int32, (nbp, blk), 0)

    for i in range(nb):
        rows = slice(i * blk, (i + 1) * blk)
        qi = q_ref[rows, :]
        if i > 0:
            valid = sub < i
            gm = jnp.where(valid, gate[:, i * blk:(i + 1) * blk], -jnp.inf)
            rank = jnp.zeros((nbp, blk), F32)
            for jp in range(i):
                row = gm[jp:jp + 1, :]
                beats = (row > gm) | ((row == gm) & (sub > jp))
                rank = rank + jnp.where(beats, 1.0, 0.0)
            sel_ref[...] = jnp.where(valid & (rank < n_sel), 0.0, -jnp.inf)

        s = _dot_nt(k_ref[rows, :], qi) * scale + tab_ref[0]
        m = jnp.max(s, axis=0, keepdims=True)
        p = jnp.exp(s - m)
        l = jnp.sum(p, axis=0, keepdims=True)
        acc = _dot(vt_ref[i], p.astype(BF16))

        def step(kj, vtj, bias, m, l, acc):
            s = _dot_nt(kj, qi) * scale + bias
            m_new = jnp.maximum(m, jnp.max(s, axis=0, keepdims=True))
            a = jnp.exp(m - m_new)
            p = jnp.exp(s - m_new)
            l_new = a * l + jnp.sum(p, axis=0, keepdims=True)
            return m_new, l_new, a * acc + _dot(vtj, p.astype(BF16))

        if i >= 1:
            j = i - 1
            m, l, acc = step(k_ref[j * blk:(j + 1) * blk, :], vt_ref[j],
                             tab_ref[1] + sel_ref[j:j + 1, :], m, l, acc)
        if i >= 2:
            def body(j, carry):
                r0 = pl.multiple_of(j * blk, blk)
                return step(k_ref[pl.ds(r0, blk), :], vt_ref[j],
                            tab_ref[2] + sel_ref[pl.ds(j, 1), :], *carry)

            m, l, acc = lax.fori_loop(0, i - 1, body, (m, l, acc))
        o_ref[rows, :] = (acc / l).T.astype(o_ref.dtype)


def _t5_bucket_idx(dist):
    n = jnp.maximum(dist, 0)
    max_exact = REL_BUCKETS // 2
    large = max_exact + (jnp.log(jnp.maximum(n, 1).astype(F32) / max_exact)
                         / math.log(REL_MAX_DIST / max_exact)
                         * (REL_BUCKETS - max_exact)).astype(jnp.int32)
    large = jnp.minimum(large, REL_BUCKETS - 1)
    return jnp.where(n < max_exact, n, large)


def _moba_bias_tables(rel_bias):
    loc = jnp.arange(MOBA_BLOCK)
    d0 = loc[None, :] - loc[:, None]
    bh = rel_bias.T.astype(F32)
    t_self = jnp.where(d0 >= 0, bh[:, _t5_bucket_idx(d0)], -jnp.inf)
    t_prev = bh[:, _t5_bucket_idx(d0 + MOBA_BLOCK)]
    t_far = bh[:, _t5_bucket_idx(d0 + 2 * MOBA_BLOCK)]
    return jnp.stack([t_self, t_prev, t_far], axis=1)


def moba_attention(proj, tabs, bsz, s):
    assert s % MOBA_BLOCK == 0
    nb = s // MOBA_BLOCK
    nbp = -(-nb // 8) * 8
    n_sel = min(MOBA_TOPK, nb - 1)
    blk = MOBA_BLOCK

    def col(off):
        return pl.BlockSpec((None, s, HD), lambda h, b: (b, 0, off + h))

    return pl.pallas_call(
        functools.partial(_moba_kernel, nb=nb, n_sel=n_sel),
        grid=(MOBA_HEADS, bsz),
        in_specs=[col(0), col(MOBA_HEADS), col(2 * MOBA_HEADS),
                  pl.BlockSpec((None, 3, blk, blk), lambda h, b: (h, 0, 0, 0))],
        out_specs=pl.BlockSpec((None, s, HD), lambda h, b: (b, 0, h)),
        out_shape=jax.ShapeDtypeStruct((bsz, s, MOBA_WIDTH), BF16),
        scratch_shapes=[pltpu.VMEM((nb, HD, blk), BF16),
                        pltpu.VMEM((nbp, HD), F32),
                        pltpu.VMEM((nbp, blk), F32)],
        compiler_params=_params("parallel", "parallel"),
        name="moba_attention",
    )(proj, proj, proj, tabs)


def _memattn_kernel(q_ref, k_ref, v_ref, o_ref, *, n_chunks, qc):
    scale = HD ** -0.5
    k = k_ref[...]
    vt = v_ref[...].astype(F32).T.astype(BF16)

    def body(c, carry):
        r0 = pl.multiple_of(c * qc, qc)
        s = _dot_nt(k, q_ref[pl.ds(r0, qc), :]) * scale
        m = jnp.max(s, axis=0, keepdims=True)
        p = jnp.exp(s - m)
        l = jnp.sum(p, axis=0, keepdims=True)
        o = _dot(vt, p.astype(BF16)) / l
        o_ref[pl.ds(r0, qc), :] = o.T.astype(o_ref.dtype)
        return carry

    lax.fori_loop(0, n_chunks, body, 0)


def memory_attention(proj, q_off, kv, bsz, s):
    m_len = kv.shape[1]
    qc = 256
    return pl.pallas_call(
        functools.partial(_memattn_kernel, n_chunks=s // qc, qc=qc),
        grid=(bsz, MEM_HEADS),
        in_specs=[pl.BlockSpec((None, s, HD), lambda b, h: (b, 0, q_off + h)),
                  pl.BlockSpec((None, m_len, HD), lambda b, h: (b, 0, h)),
                  pl.BlockSpec((None, m_len, HD), lambda b, h: (b, 0, MEM_HEADS + h))],
        out_specs=pl.BlockSpec((None, s, HD), lambda b, h: (b, 0, h)),
        out_shape=jax.ShapeDtypeStruct((bsz, s, MEM_WIDTH), BF16),
        compiler_params=_params("parallel", "parallel"),
        name="memory_attention",
    )(proj, kv, kv)


def _outproj_kernel(x_ref, ya_ref, yb_ref, wa_ref, wb_ref, o_ref):
    o_ref[...] = x_ref[...] + _dot(ya_ref[...], wa_ref[...]) + _dot(yb_ref[...], wb_ref[...])


def out_projection(x, ya, yb, wa, wb, tm):
    t, d = x.shape
    ka, kb = ya.shape[1], yb.shape[1]
    return pl.pallas_call(
        _outproj_kernel,
        grid=(t // tm,),
        in_specs=[pl.BlockSpec((tm, d), lambda i: (i, 0)),
                  pl.BlockSpec((tm, ka), lambda i: (i, 0)),
                  pl.BlockSpec((tm, kb), lambda i: (i, 0)),
                  pl.BlockSpec((ka, d), lambda i: (0, 0)),
                  pl.BlockSpec((kb, d), lambda i: (0, 0))],
        out_specs=pl.BlockSpec((tm, d), lambda i: (i, 0)),
        out_shape=jax.ShapeDtypeStruct((t, d), F32),
        compiler_params=_params("parallel"),
        name="out_projection",
    )(x, ya, yb, wa, wb)


def _ffn_kernel(x_ref, g_ref, wg_ref, wu_ref, wd_ref, o_ref, h_ref, acc_ref, *, tm):
    f = pl.program_id(1)

    @pl.when(f == 0)
    def _():
        _rms_rows_to(h_ref, x_ref, g_ref, tm)
        acc_ref[...] = x_ref[...]

    h = h_ref[...]
    a = _dot(h, wg_ref[...])
    u = _dot(h, wu_ref[...])
    act = (a * _sigmoid(a) * u).astype(BF16)
    acc_ref[...] += _dot(act, wd_ref[...])

    @pl.when(f == pl.num_programs(1) - 1)
    def _():
        o_ref[...] = acc_ref[...]


def dense_ffn(x, g, wg, wu, wd, tm, tf):
    t, d = x.shape
    ff = wg.shape[1]
    return pl.pallas_call(
        functools.partial(_ffn_kernel, tm=tm),
        grid=(t // tm, ff // tf),
        in_specs=[pl.BlockSpec((tm, d), lambda i, f: (i, 0)),
                  pl.BlockSpec((1, d), lambda i, f: (0, 0)),
                  pl.BlockSpec((d, tf), lambda i, f: (0, f)),
                  pl.BlockSpec((d, tf), lambda i, f: (0, f)),
                  pl.BlockSpec((tf, d), lambda i, f: (f, 0))],
        out_specs=pl.BlockSpec((tm, d), lambda i, f: (i, 0)),
        out_shape=jax.ShapeDtypeStruct((t, d), F32),
        scratch_shapes=[pltpu.VMEM((tm, d), BF16), pltpu.VMEM((tm, d), F32)],
        compiler_params=_params("parallel", "arbitrary"),
        name="dense_ffn",
    )(x, g.reshape(1, d), wg, wu, wd)


def _ssd_kernel(z_ref, xs_ref, b_ref, c_ref, dt_ref, cw_ref, cb_ref, dtb_ref, alog_ref, dsk_ref,
                gout_ref, exp_ref, o_ref, h_ref, pad_ref, *, nc):
    L = SSD_CHUNK
    gw = SSD_GW
    n = SSD_STATE
    h_ref[...] = jnp.zeros_like(h_ref)
    a_neg = -jnp.exp(alog_ref[...])
    causal = (lax.broadcasted_iota(jnp.int32, (L, L), 0) >= lax.broadcasted_iota(jnp.int32, (L, L), 1))
    tri = jnp.where(causal, 1.0, 0.0).astype(BF16)
    lane_head = lax.broadcasted_iota(jnp.int32, (1, gw), 1) // SSD_HD
    expm = exp_ref[...]
    srcs = ((xs_ref, 0, gw), (b_ref, gw, gw + n), (c_ref, gw + n, gw + 2 * n))

    def chunk(c, carry):
        r0 = pl.multiple_of(c * L, L)
        rp = pl.multiple_of(jnp.maximum(r0 - 16, 0), 16)
        for ref, lo, hi in srcs:
            pad_ref[0:16, lo:hi] = jnp.where(c > 0, ref[pl.ds(rp, 16), :].astype(F32), 0.0)
            pad_ref[16:16 + L, lo:hi] = ref[pl.ds(r0, L), :].astype(F32)
        conv = cb_ref[...]
        for k in range(SSD_CONV):
            off = 16 - (SSD_CONV - 1) + k
            conv = conv + cw_ref[k:k + 1, :] * pad_ref[off:off + L, :]
        act = conv * _sigmoid(conv)
        xs = act[:, 0:gw]
        bm = act[:, gw:gw + n]
        cm = act[:, gw + n:gw + 2 * n]

        dtr = dt_ref[pl.ds(r0, L), :] + dtb_ref[...]
        dt = jnp.maximum(dtr, 0.0) + jnp.log(1.0 + jnp.exp(-jnp.abs(dtr)))
        la = dt * a_neg
        cs = _dot3r(tri, la)
        cs_t = cs.T
        dt_t = dt.T
        bm16 = bm.astype(BF16)
        cm16 = cm.astype(BF16)
        scores = _dot_nt(cm16, bm16)
        y = _dot_nt(cm16, h_ref[...].astype(BF16)) * _dot3l(jnp.exp(cs), expm)
        w_end = dt * jnp.exp(cs[L - 1:L, :] - cs)
        xs_t = xs.T
        for e in range(SSD_GHEADS):
            diff = cs[:, e:e + 1] - cs_t[e:e + 1, :]
            dec = jnp.exp(jnp.where(causal, diff, -jnp.inf))
            mm = (scores * dec * dt_t[e:e + 1, :]).astype(BF16)
            xe = jnp.where(lane_head == e, xs, 0.0).astype(BF16)
            y = y + _dot(mm, xe)
            bw = (bm * w_end[:, e:e + 1]).astype(BF16)
            st = _dot(xs_t[e * SSD_HD:(e + 1) * SSD_HD, :].astype(BF16), bw)
            cdec = jnp.exp(cs[L - 1:L, e:e + 1])
            hs = slice(e * SSD_HD, (e + 1) * SSD_HD)
            h_ref[hs, :] = h_ref[hs, :] * cdec + st
        y = y + xs * dsk_ref[...]
        z = z_ref[pl.ds(r0, L), :].astype(F32)
        u = y * (z * _sigmoid(z))
        ms = jnp.mean(u * u, axis=-1, keepdims=True)
        o_ref[pl.ds(r0, L), :] = (u * lax.rsqrt(ms + EPS) * gout_ref[...]).astype(o_ref.dtype)
        return carry

    lax.fori_loop(0, nc, chunk, 0)


def ssd_mixer(proj, dt_all, cw_g, cb_g, dtb_g, alog_g, dsk_g, gout_g, expand, bsz, s):
    assert s % SSD_CHUNK == 0
    gw = SSD_GW
    z_blk = 0
    xs_blk = SSD_INNER // gw
    b_blk = 2 * SSD_INNER // SSD_STATE
    c_blk = b_blk + SSD_GROUPS

    def seq(width, off):
        return pl.BlockSpec((None, s, width), lambda b, g: (b, 0, off + g))

    def par(rows, width):
        return pl.BlockSpec((None, rows, width), lambda b, g: (g, 0, 0))

    return pl.pallas_call(
        functools.partial(_ssd_kernel, nc=s // SSD_CHUNK),
        grid=(bsz, SSD_GROUPS),
        in_specs=[seq(gw, z_blk), seq(gw, xs_blk), seq(SSD_STATE, b_blk), seq(SSD_STATE, c_blk),
                  seq(LANES, 0),
                  par(SSD_CONV, SSD_CONV_GW), par(1, SSD_CONV_GW), par(1, LANES), par(1, LANES),
                  par(1, gw), par(1, gw),
                  pl.BlockSpec((LANES, gw), lambda b, g: (0, 0))],
        out_specs=seq(gw, 0),
        out_shape=jax.ShapeDtypeStruct((bsz, s, SSD_INNER), BF16),
        scratch_shapes=[pltpu.VMEM((gw, SSD_STATE), F32),
                        pltpu.VMEM((16 + SSD_CHUNK, SSD_CONV_GW), F32)],
        compiler_params=_params("parallel", "parallel"),
        name="ssd_mixer",
    )(proj, proj, proj, proj, dt_all, cw_g, cb_g, dtb_g, alog_g, dsk_g, gout_g, expand)


def _ssd_group_params(conv_w, conv_b, dt_bias, a_log, d_skip, g_out):
    g, gh, gw, n = SSD_GROUPS, SSD_GHEADS, SSD_GW, SSD_STATE

    def conv_cols(a):
        xs = a[..., :SSD_INNER].reshape(a.shape[:-1] + (g, gw))
        bb = a[..., SSD_INNER:SSD_INNER + g * n].reshape(a.shape[:-1] + (g, n))
        cc = a[..., SSD_INNER + g * n:].reshape(a.shape[:-1] + (g, n))
        return jnp.moveaxis(jnp.concatenate([xs, bb, cc], axis=-1), -2, 0)

    cw_g = conv_cols(conv_w.astype(F32))
    cb_g = conv_cols(conv_b.astype(F32)[None, :])

    def per_head(a):
        return jnp.pad(a.astype(F32).reshape(g, 1, gh), ((0, 0), (0, 0), (0, LANES - gh)))

    dsk_g = jnp.repeat(d_skip.astype(F32), SSD_HD).reshape(g, 1, gw)
    gout_g = g_out.astype(F32).reshape(g, 1, gw)
    expand = (jnp.arange(LANES)[:, None] == (jnp.arange(gw)[None, :] // SSD_HD)).astype(BF16)
    return cw_g, cb_g, per_head(dt_bias), per_head(a_log), dsk_g, gout_g, expand


def _router_kernel(x_ref, g_ref, wrt_ref, br_ref, h_ref, route_ref, gates_ref, cnt_ref, carry_ref, *, tm):
    @pl.when(pl.program_id(0) == 0)
    def _():
        carry_ref[...] = jnp.zeros_like(carry_ref)

    ne = N_EXPERTS
    hn = _rms(x_ref[...], g_ref[...])
    h_ref[...] = hn.astype(h_ref.dtype)
    h_hi = hn.astype(BF16)
    h_lo = (hn - h_hi.astype(F32)).astype(BF16)
    wr = wrt_ref[...]
    w_hi = wr.astype(BF16)
    w_lo = (wr - w_hi.astype(F32)).astype(BF16)
    logits = _dot_nt(w_hi, h_hi) + _dot_nt(w_hi, h_lo) + _dot_nt(w_lo, h_hi) + br_ref[:, 0:1]
    sub = lax.broadcasted_iota(jnp.int32, (ne, tm), 0)
    l1 = jnp.max(logits, axis=0, keepdims=True)
    i1 = jnp.min(jnp.where(logits == l1, sub, ne), axis=0, keepdims=True)
    rest = jnp.where(sub == i1, -jnp.inf, logits)
    l2 = jnp.max(rest, axis=0, keepdims=True)
    i2 = jnp.min(jnp.where(rest == l2, sub, ne), axis=0, keepdims=True)
    e2 = jnp.exp(l2 - l1)
    g1 = 1.0 / (1.0 + e2)
    g2 = e2 / (1.0 + e2)
    sel = jnp.where((sub == i1) | (sub == i2), 1.0, 0.0)
    before = (lax.broadcasted_iota(jnp.int32, (tm, tm), 0) < lax.broadcasted_iota(jnp.int32, (tm, tm), 1))
    prefix = _dot(sel.astype(BF16), jnp.where(before, 1.0, 0.0).astype(BF16)) + carry_ref[:, 0:1]
    r1 = jnp.sum(jnp.where(sub == i1, prefix, 0.0), axis=0, keepdims=True)
    r2 = jnp.sum(jnp.where(sub == i2, prefix, 0.0), axis=0, keepdims=True)
    route = jnp.where(sub == 0, i1, jnp.where(sub == 1, i2, jnp.where(
        sub == 2, r1.astype(jnp.int32), jnp.where(sub == 3, r2.astype(jnp.int32), 0))))
    route_ref[...] = route
    gates_ref[...] = jnp.where(sub == 0, g1, jnp.where(sub == 1, g2, 0.0))
    carry_ref[...] = carry_ref[...] + jnp.sum(sel, axis=1, keepdims=True)
    cnt_ref[...] = carry_ref[...]


def moe_router(x, g, w_router, b_router, tm, h_dtype):
    t, d = x.shape
    ne = N_EXPERTS
    wrt = w_router.astype(F32).T
    br = jnp.broadcast_to(b_router.astype(F32)[:, None], (ne, LANES))
    return pl.pallas_call(
        functools.partial(_router_kernel, tm=tm),
        grid=(t // tm,),
        in_specs=[pl.BlockSpec((tm, d), lambda i: (i, 0)),
                  pl.BlockSpec((1, d), lambda i: (0, 0)),
                  pl.BlockSpec((ne, d), lambda i: (0, 0)),
                  pl.BlockSpec((ne, LANES), lambda i: (0, 0))],
        out_specs=[pl.BlockSpec((tm, d), lambda i: (i, 0)),
                   pl.BlockSpec((ne, tm), lambda i: (0, i)),
                   pl.BlockSpec((ne, tm), lambda i: (0, i)),
                   pl.BlockSpec((ne, LANES), lambda i: (0, 0))],
        out_shape=[jax.ShapeDtypeStruct((t, d), h_dtype),
                   jax.ShapeDtypeStruct((ne, t), jnp.int32),
                   jax.ShapeDtypeStruct((ne, t), F32),
                   jax.ShapeDtypeStruct((ne, LANES), F32)],
        scratch_shapes=[pltpu.VMEM((ne, LANES), F32)],
        compiler_params=_params("arbitrary"),
        name="moe_router",
    )(x, g.reshape(1, d), wrt, br)


def _expert_kernel(te_ref, nu_ref, xb_ref, wg_ref, wu_ref, wd_ref, o_ref, acc_ref):
    c = pl.program_id(0)
    f = pl.program_id(1)
    last = pl.num_programs(1) - 1
    used = c < nu_ref[0]

    @pl.when(used)
    def _():
        h = xb_ref[...].astype(BF16)
        a = _dot(h, wg_ref[...])
        u = _dot(h, wu_ref[...])
        act = (a * _sigmoid(a) * u).astype(BF16)
        y = _dot(act, wd_ref[...])

        @pl.when(f == 0)
        def _():
            acc_ref[...] = y

        @pl.when(f > 0)
        def _():
            acc_ref[...] += y

        @pl.when(f == last)
        def _():
            o_ref[...] = acc_ref[...]

    @pl.when(jnp.logical_and(jnp.logical_not(used), f == last))
    def _():
        o_ref[...] = jnp.zeros_like(o_ref)


def expert_ffn(tile_expert, n_used, xb, wg, wu, wd, tf):
    n_buf, d = xb.shape
    ff = wg.shape[2]
    r = EXPERT_ROWS
    grid_spec = pltpu.PrefetchScalarGridSpec(
        num_scalar_prefetch=2,
        grid=(n_buf // r, ff // tf),
        in_specs=[pl.BlockSpec((r, d), lambda c, f, te, nu: (c, 0)),
                  pl.BlockSpec((None, d, tf), lambda c, f, te, nu: (te[c], 0, f)),
                  pl.BlockSpec((None, d, tf), lambda c, f, te, nu: (te[c], 0, f)),
                  pl.BlockSpec((None, tf, d), lambda c, f, te, nu: (te[c], f, 0))],
        out_specs=pl.BlockSpec((r, d), lambda c, f, te, nu: (c, 0)),
        scratch_shapes=[pltpu.VMEM((r, d), F32)],
    )
    return pl.pallas_call(
        _expert_kernel,
        grid_spec=grid_spec,
        out_shape=jax.ShapeDtypeStruct((n_buf, d), F32),
        compiler_params=_params("parallel", "arbitrary"),
        name="expert_ffn",
    )(tile_expert, n_used, xb, wg, wu, wd)


def _combine_kernel(x_ref, y1_ref, y2_ref, g1_ref, g2_ref, g_ref, o_ref):
    y = x_ref[...] + (g1_ref[...] * y1_ref[...] + g2_ref[...] * y2_ref[...])
    o_ref[...] = _rms(y, g_ref[...])


def combine_final(x, y1, y2, g1, g2, g, tm):
    t, d = x.shape
    row = pl.BlockSpec((tm, d), lambda i: (i, 0))
    colv = pl.BlockSpec((tm, 1), lambda i: (i, 0))
    return pl.pallas_call(
        _combine_kernel,
        grid=(t // tm,),
        in_specs=[row, row, row, colv, colv, pl.BlockSpec((1, d), lambda i: (0, 0))],
        out_specs=row,
        out_shape=jax.ShapeDtypeStruct((t, d), F32),
        compiler_params=_params("parallel"),
        name="combine_final",
    )(x, y1, y2, g1, g2, g.reshape(1, d))


def _pick(n, prefs):
    for p in prefs:
        if n % p == 0:
            return p
    return n


def kernel(x, mem, g_mix, g_mem, w_in_moba, w_in_ssd, w_mem_kv, w_out, rel_bias, conv_w, conv_b, dt_bias, a_log, d_skip, g_ssd_out, g_ffn, w_ffn_gate, w_ffn_up, w_ffn_down, w_router, b_router, w_exp_gate, w_exp_up, w_exp_down, g_final):
    bsz, s, d = x.shape
    t = bsz * s
    m_len = mem.shape[1]
    tm = _pick(t, (1024, 512, 256))
    xf = x.reshape(t, d).astype(F32)
    memf = mem.reshape(bsz * m_len, d).astype(F32)
    tmm = _pick(bsz * m_len, (1024, 512, 256))

    proj0 = norm_matmul(xf, g_mix[0], w_in_moba[0].astype(BF16), BF16, tm, 1280)
    kv0 = norm_matmul(memf, g_mem[0], w_mem_kv[0].astype(BF16), BF16, tmm, 1024)
    proj0 = proj0.reshape(bsz, s, -1)
    y_tok = moba_attention(proj0, _moba_bias_tables(rel_bias), bsz, s)
    y_mem = memory_attention(proj0, 3 * MOBA_HEADS, kv0.reshape(bsz, m_len, -1), bsz, s)
    wo = w_out[0].astype(BF16)
    x1 = out_projection(xf, y_tok.reshape(t, -1), y_mem.reshape(t, -1), wo[:MOBA_WIDTH], wo[MOBA_WIDTH:], 512)
    x1 = dense_ffn(x1, g_ffn[0], w_ffn_gate[0].astype(BF16), w_ffn_up[0].astype(BF16),
                   w_ffn_down[0].astype(BF16), tm, 512)

    n_zx = SSD_INNER + SSD_INNER + 2 * SSD_GROUPS * SSD_STATE
    w1 = w_in_ssd[0]
    w_main = jnp.concatenate([w1[:, :n_zx], w1[:, n_zx + SSD_HEADS:]], axis=1).astype(BF16)
    w_dt = jnp.pad(w1[:, n_zx:n_zx + SSD_HEADS].reshape(d, SSD_GROUPS, SSD_GHEADS),
                   ((0, 0), (0, 0), (0, LANES - SSD_GHEADS))).reshape(d, SSD_GROUPS * LANES).astype(BF16)
    proj1 = norm_matmul(x1, g_mix[1], w_main, BF16, tm, 1536).reshape(bsz, s, -1)
    dt_all = norm_matmul(x1, g_mix[1], w_dt, F32, tm, SSD_GROUPS * LANES).reshape(bsz, s, -1)
    kv1 = norm_matmul(memf, g_mem[1], w_mem_kv[1].astype(BF16), BF16, tmm, 1024)
    y_tok = ssd_mixer(proj1, dt_all, *_ssd_group_params(conv_w[0], conv_b[0], dt_bias[0], a_log[0],
                                                         d_skip[0], g_ssd_out[0]), bsz, s)
    y_mem = memory_attention(proj1, n_zx // HD, kv1.reshape(bsz, m_len, -1), bsz, s)
    wo = w_out[1].astype(BF16)
    x2 = out_projection(x1, y_tok.reshape(t, -1), y_mem.reshape(t, -1), wo[:SSD_INNER], wo[SSD_INNER:], 512)

    h, route, gates, counts = moe_router(x2, g_ffn[1], w_router[0], b_router[0], 512, BF16)
    r = EXPERT_ROWS
    n_buf = t * TOP_K + N_EXPERTS * r
    n_tiles = n_buf // r
    cnt = counts[:, 0].astype(jnp.int32)
    padded = (cnt + r - 1) // r * r
    ends = jnp.cumsum(padded)
    starts = ends - padded
    ids = route[0:2]
    dest = jnp.sum(jnp.where(ids[:, :, None] == jnp.arange(N_EXPERTS)[None, None, :],
                             starts[None, None, :], 0), axis=-1) + route[2:4]
    tile_start = jnp.arange(n_tiles, dtype=jnp.int32) * r
    n_used = (ends[-1] // r).astype(jnp.int32)
    tile_expert = jnp.sum(tile_start[:, None] >= ends[None, :], axis=1).astype(jnp.int32)
    last_expert = jnp.sum(jnp.maximum(n_used - 1, 0) * r >= ends).astype(jnp.int32)
    tile_expert = jnp.where(jnp.arange(n_tiles) < n_used, tile_expert, last_expert)
    tok = jnp.arange(t, dtype=jnp.int32)
    src = jnp.zeros((n_buf,), jnp.int32).at[dest.reshape(-1)].set(jnp.concatenate([tok, tok]))
    xb = h[src]
    yb = expert_ffn(tile_expert, n_used.reshape(1), xb, w_exp_gate[0].astype(BF16),
                    w_exp_up[0].astype(BF16), w_exp_down[0].astype(BF16), 512)
    out = combine_final(x2, yb[dest[0]], yb[dest[1]], gates[0][:, None], gates[1][:, None], g_final, 512)
    return out.reshape(bsz, s, d).astype(x.dtype)
```

```python
import functools
import math

import jax
import jax.numpy as jnp
from jax import lax
from jax.experimental import pallas as pl
from jax.experimental.pallas import tpu as pltpu

F32 = jnp.float32
BF16 = jnp.bfloat16

D_MODEL = 1024
HD = 128
MOBA_HEADS = 12
MOBA_WIDTH = MOBA_HEADS * HD
MOBA_BLOCK = 256
MOBA_TOPK = 3
REL_BUCKETS = 32
REL_MAX_DIST = 128
MEM_HEADS = 4
MEM_WIDTH = MEM_HEADS * HD
SSD_HEADS = 24
SSD_HD = 64
SSD_INNER = SSD_HEADS * SSD_HD
SSD_GROUPS = 4
SSD_STATE = 128
SSD_CONV = 4
SSD_CHUNK = 128
SSD_GHEADS = SSD_HEADS // SSD_GROUPS
SSD_GW = SSD_GHEADS * SSD_HD
SSD_CONV_GW = SSD_GW + 2 * SSD_STATE
D_FF = 3584
N_EXPERTS = 8
TOP_K = 2
EPS = 1e-6

LOG2E = math.log2(math.e)
LANES = 128
VMEM_LIMIT = 48 * 1024 * 1024
EXPERT_ROWS = 1024


def _dot(a, b):
    return jnp.dot(a, b, preferred_element_type=F32)


def _dot_nt(a, b):
    return lax.dot_general(a, b, (((1,), (1,)), ((), ())), preferred_element_type=F32)


def _split3(x):
    p0 = x.astype(BF16)
    r0 = x - p0.astype(F32)
    p1 = r0.astype(BF16)
    p2 = (r0 - p1.astype(F32)).astype(BF16)
    return p0, p1, p2


def _dot3l(x, w):
    return sum(_dot(p, w) for p in _split3(x))


def _dot3r(w, x):
    return sum(_dot(w, p) for p in _split3(x))


def _sigmoid(x):
    return 1.0 / (1.0 + jnp.exp(-x))


def _rms(x, g):
    ms = jnp.mean(x * x, axis=-1, keepdims=True)
    return x * lax.rsqrt(ms + EPS) * g


def _rms_rows_to(h_ref, x_ref, g_ref, rows):
    step = min(rows, 256)
    for r in range(0, rows, step):
        h_ref[r:r + step, :] = _rms(x_ref[r:r + step, :], g_ref[...]).astype(h_ref.dtype)


def _params(*sem):
    return pltpu.CompilerParams(dimension_semantics=sem, vmem_limit_bytes=VMEM_LIMIT)


def _normmm_kernel(x_ref, g_ref, w_ref, o_ref, h_ref, *, tm):
    @pl.when(pl.program_id(1) == 0)
    def _():
        _rms_rows_to(h_ref, x_ref, g_ref, tm)

    o_ref[...] = _dot(h_ref[...], w_ref[...]).astype(o_ref.dtype)


def norm_matmul(x, g, w, out_dtype, tm, tn):
    t, d = x.shape
    n = w.shape[1]
    return pl.pallas_call(
        functools.partial(_normmm_kernel, tm=tm),
        grid=(t // tm, n // tn),
        in_specs=[pl.BlockSpec((tm, d), lambda i, j: (i, 0)),
                  pl.BlockSpec((1, d), lambda i, j: (0, 0)),
                  pl.BlockSpec((d, tn), lambda i, j: (0, j))],
        out_specs=pl.BlockSpec((tm, tn), lambda i, j: (i, j)),
        out_shape=jax.ShapeDtypeStruct((t, n), out_dtype),
        scratch_shapes=[pltpu.VMEM((tm, d), BF16)],
        compiler_params=_params("parallel", "arbitrary"),
        name="norm_matmul",
    )(x, g.reshape(1, d), w)


def _moba_kernel(q_ref, k_ref, v_ref, tab_ref, o_ref, vt_ref, km_ref, *, nb, n_sel):
    blk = MOBA_BLOCK
    scale = HD ** -0.5 * LOG2E
    nbp = km_ref.shape[0]
    km_ref[...] = jnp.zeros_like(km_ref)
    for j in range(nb):
        rows = slice(j * blk, (j + 1) * blk)
        vt_ref[:, rows] = v_ref[rows, :].astype(F32).T.astype(BF16)
        km_ref[j:j + 1, :] = jnp.mean(k_ref[rows, :].astype(F32), axis=0, keepdims=True)
    km = km_ref[...]
    km_hi = km.astype(BF16)
    km_lo = (km - km_hi.astype(F32)).astype(BF16)
    q_all = q_ref[...]
    gate = _dot_nt(km_hi, q_all) + _dot_nt(km_lo, q_all)
    sub = lax.broadcasted_iota(jnp.int32, (nbp, blk), 0)
    bias_far = tab_ref[2, 0:1, :]

    for i in range(nb):
        rows = slice(i * blk, (i + 1) * blk)
        nk = (i + 1) * blk
        qi = q_ref[rows, :]
        s_all = _dot_nt(k_ref[0:nk, :], qi)
        if i > 0:
            valid = sub < i
            gm = jnp.where(valid, gate[:, rows], -jnp.inf)
            rank = jnp.zeros((nbp, blk), F32)
            for jp in range(i):
                row = gm[jp:jp + 1, :]
                beats = (row > gm) | ((row == gm) & (sub > jp))
                rank = rank + jnp.where(beats, 1.0, 0.0)
            selm = jnp.where(valid & (rank < n_sel), 0.0, -jnp.inf)
        bands = []
        for j in range(i + 1):
            sj = s_all[j * blk:(j + 1) * blk, :] * scale
            if j == i:
                sj = sj + tab_ref[0]
            elif j == i - 1:
                sj = sj + (tab_ref[1] + selm[j:j + 1, :])
            else:
                sj = sj + (bias_far + selm[j:j + 1, :])
            bands.append(sj)
        s = jnp.concatenate(bands, axis=0) if i > 0 else bands[0]
        m = jnp.max(s, axis=0, keepdims=True)
        p = jnp.exp2(s - m)
        l = jnp.sum(p, axis=0, keepdims=True)
        acc = _dot(vt_ref[:, 0:nk], p.astype(BF16))
        o_ref[rows, :] = (acc / l).T.astype(o_ref.dtype)


def _t5_bucket_idx(dist):
    n = jnp.maximum(dist, 0)
    max_exact = REL_BUCKETS // 2
    large = max_exact + (jnp.log(jnp.maximum(n, 1).astype(F32) / max_exact)
                         / math.log(REL_MAX_DIST / max_exact)
                         * (REL_BUCKETS - max_exact)).astype(jnp.int32)
    large = jnp.minimum(large, REL_BUCKETS - 1)
    return jnp.where(n < max_exact, n, large)


def _bias_table_kernel(rb_ref, idx_ref, o_ref):
    h = pl.program_id(0)
    for t in range(3):
        idx = idx_ref[t]
        acc = jnp.full(idx.shape, -jnp.inf, F32)
        for b in range(REL_BUCKETS):
            acc = jnp.where(idx == b, rb_ref[h, b], acc)
        o_ref[t] = acc * LOG2E


def _moba_bias_tables(rel_bias):
    blk = MOBA_BLOCK
    loc = jnp.arange(blk)
    d0 = loc[None, :] - loc[:, None]
    idx = jnp.stack([jnp.where(d0 >= 0, _t5_bucket_idx(d0), -1),
                     _t5_bucket_idx(d0 + blk),
                     _t5_bucket_idx(d0 + 2 * blk)]).astype(jnp.int32)
    return pl.pallas_call(
        _bias_table_kernel,
        grid=(MOBA_HEADS,),
        in_specs=[pl.BlockSpec(memory_space=pltpu.SMEM),
                  pl.BlockSpec((3, blk, blk), lambda h: (0, 0, 0))],
        out_specs=pl.BlockSpec((None, 3, blk, blk), lambda h: (h, 0, 0, 0)),
        out_shape=jax.ShapeDtypeStruct((MOBA_HEADS, 3, blk, blk), F32),
        compiler_params=_params("parallel"),
        name="moba_bias_tables",
    )(rel_bias.T.astype(F32), idx)


def moba_attention(proj, tabs, bsz, s):
    assert s % MOBA_BLOCK == 0
    nb = s // MOBA_BLOCK
    nbp = -(-nb // 8) * 8
    n_sel = min(MOBA_TOPK, nb - 1)
    blk = MOBA_BLOCK

    def col(off):
        return pl.BlockSpec((None, s, HD), lambda h, b: (b, 0, off + h))

    return pl.pallas_call(
        functools.partial(_moba_kernel, nb=nb, n_sel=n_sel),
        grid=(MOBA_HEADS, bsz),
        in_specs=[col(0), col(MOBA_HEADS), col(2 * MOBA_HEADS),
                  pl.BlockSpec((None, 3, blk, blk), lambda h, b: (h, 0, 0, 0))],
        out_specs=pl.BlockSpec((None, s, HD), lambda h, b: (b, 0, h)),
        out_shape=jax.ShapeDtypeStruct((bsz, s, MOBA_WIDTH), BF16),
        scratch_shapes=[pltpu.VMEM((HD, s), BF16),
                        pltpu.VMEM((nbp, HD), F32)],
        compiler_params=_params("parallel", "parallel"),
        name="moba_attention",
    )(proj, proj, proj, tabs)


def _memattn_kernel(q_ref, k_ref, v_ref, o_ref, *, n_chunks, qc):
    scale = HD ** -0.5 * LOG2E
    k = k_ref[...]
    vt = v_ref[...].astype(F32).T.astype(BF16)
    for c in range(n_chunks):
        rows = slice(c * qc, (c + 1) * qc)
        s = _dot_nt(k, q_ref[rows, :]) * scale
        m = jnp.max(s, axis=0, keepdims=True)
        p = jnp.exp2(s - m)
        l = jnp.sum(p, axis=0, keepdims=True)
        o = _dot(vt, p.astype(BF16)) / l
        o_ref[rows, :] = o.T.astype(o_ref.dtype)


def memory_attention(proj, q_off, kv, bsz, s):
    m_len = kv.shape[1]
    qc = 256
    return pl.pallas_call(
        functools.partial(_memattn_kernel, n_chunks=s // qc, qc=qc),
        grid=(bsz, MEM_HEADS),
        in_specs=[pl.BlockSpec((None, s, HD), lambda b, h: (b, 0, q_off + h)),
                  pl.BlockSpec((None, m_len, HD), lambda b, h: (b, 0, h)),
                  pl.BlockSpec((None, m_len, HD), lambda b, h: (b, 0, MEM_HEADS + h))],
        out_specs=pl.BlockSpec((None, s, HD), lambda b, h: (b, 0, h)),
        out_shape=jax.ShapeDtypeStruct((bsz, s, MEM_WIDTH), BF16),
        compiler_params=_params("parallel", "parallel"),
        name="memory_attention",
    )(proj, kv, kv)


def _outproj_kernel(x_ref, ya_ref, yb_ref, wa_ref, wb_ref, o_ref):
    o_ref[...] = x_ref[...] + _dot(ya_ref[...], wa_ref[...]) + _dot(yb_ref[...], wb_ref[...])


def out_projection(x, ya, yb, wa, wb, tm):
    t, d = x.shape
    ka, kb = ya.shape[1], yb.shape[1]
    return pl.pallas_call(
        _outproj_kernel,
        grid=(t // tm,),
        in_specs=[pl.BlockSpec((tm, d), lambda i: (i, 0)),
                  pl.BlockSpec((tm, ka), lambda i: (i, 0)),
                  pl.BlockSpec((tm, kb), lambda i: (i, 0)),
                  pl.BlockSpec((ka, d), lambda i: (0, 0)),
                  pl.BlockSpec((kb, d), lambda i: (0, 0))],
        out_specs=pl.BlockSpec((tm, d), lambda i: (i, 0)),
        out_shape=jax.ShapeDtypeStruct((t, d), F32),
        compiler_params=_params("parallel"),
        name="out_projection",
    )(x, ya, yb, wa, wb)


def _ffn_kernel(x_ref, g_ref, wg_ref, wu_ref, wd_ref, o_ref, h_ref, acc_ref, *, tm):
    f = pl.program_id(1)

    @pl.when(f == 0)
    def _():
        _rms_rows_to(h_ref, x_ref, g_ref, tm)
        acc_ref[...] = x_ref[...]

    h = h_ref[...]
    a = _dot(h, wg_ref[...])
    u = _dot(h, wu_ref[...])
    act = (a * _sigmoid(a) * u).astype(BF16)
    acc_ref[...] += _dot(act, wd_ref[...])

    @pl.when(f == pl.num_programs(1) - 1)
    def _():
        o_ref[...] = acc_ref[...]


def dense_ffn(x, g, wg, wu, wd, tm, tf):
    t, d = x.shape
    ff = wg.shape[1]
    return pl.pallas_call(
        functools.partial(_ffn_kernel, tm=tm),
        grid=(t // tm, ff // tf),
        in_specs=[pl.BlockSpec((tm, d), lambda i, f: (i, 0)),
                  pl.BlockSpec((1, d), lambda i, f: (0, 0)),
                  pl.BlockSpec((d, tf), lambda i, f: (0, f)),
                  pl.BlockSpec((d, tf), lambda i, f: (0, f)),
                  pl.BlockSpec((tf, d), lambda i, f: (f, 0))],
        out_specs=pl.BlockSpec((tm, d), lambda i, f: (i, 0)),
        out_shape=jax.ShapeDtypeStruct((t, d), F32),
        scratch_shapes=[pltpu.VMEM((tm, d), BF16), pltpu.VMEM((tm, d), F32)],
        compiler_params=_params("parallel", "arbitrary"),
        name="dense_ffn",
    )(x, g.reshape(1, d), wg, wu, wd)


def _ssd_kernel(z_ref, xs_ref, b_ref, c_ref, dt_ref, cw_ref, cb_ref, dtb_ref, alog_ref, dsk_ref,
                gout_ref, exp_ref, o_ref, h_ref, pad_ref, *, nc):
    L = SSD_CHUNK
    gw = SSD_GW
    n = SSD_STATE
    h_ref[...] = jnp.zeros_like(h_ref)
    a_neg = -jnp.exp(alog_ref[...])
    causal = (lax.broadcasted_iota(jnp.int32, (L, L), 0) >= lax.broadcasted_iota(jnp.int32, (L, L), 1))
    tri = jnp.where(causal, 1.0, 0.0).astype(BF16)
    lane_head = lax.broadcasted_iota(jnp.int32, (1, gw), 1) // SSD_HD
    expm = exp_ref[...]
    srcs = ((xs_ref, 0, gw), (b_ref, gw, gw + n), (c_ref, gw + n, gw + 2 * n))

    def chunk(c, carry):
        r0 = pl.multiple_of(c * L, L)
        rp = pl.multiple_of(jnp.maximum(r0 - 16, 0), 16)
        for ref, lo, hi in srcs:
            pad_ref[0:16, lo:hi] = jnp.where(c > 0, ref[pl.ds(rp, 16), :].astype(F32), 0.0)
            pad_ref[16:16 + L, lo:hi] = ref[pl.ds(r0, L), :].astype(F32)
        conv = cb_ref[...]
        for k in range(SSD_CONV):
            off = 16 - (SSD_CONV - 1) + k
            conv = conv + cw_ref[k:k + 1, :] * pad_ref[off:off + L, :]
        act = conv * _sigmoid(conv)
        xs = act[:, 0:gw]
        bm = act[:, gw:gw + n]
        cm = act[:, gw + n:gw + 2 * n]

        dtr = dt_ref[pl.ds(r0, L), :] + dtb_ref[...]
        dt = jnp.maximum(dtr, 0.0) + jnp.log(1.0 + jnp.exp(-jnp.abs(dtr)))
        la = dt * a_neg
        cs = _dot3r(tri, la)
        cs_t = cs.T
        dt_t = dt.T
        bm16 = bm.astype(BF16)
        cm16 = cm.astype(BF16)
        scores = _dot_nt(cm16, bm16)
        y = _dot_nt(cm16, h_ref[...].astype(BF16)) * _dot3l(jnp.exp(cs), expm)
        w_end = dt * jnp.exp(cs[L - 1:L, :] - cs)
        xs_t = xs.T
        for e in range(SSD_GHEADS):
            diff = cs[:, e:e + 1] - cs_t[e:e + 1, :]
            dec = jnp.exp(jnp.where(causal, diff, -jnp.inf))
            mm = (scores * dec * dt_t[e:e + 1, :]).astype(BF16)
            xe = jnp.where(lane_head == e, xs, 0.0).astype(BF16)
            y = y + _dot(mm, xe)
            bw = (bm * w_end[:, e:e + 1]).astype(BF16)
            st = _dot(xs_t[e * SSD_HD:(e + 1) * SSD_HD, :].astype(BF16), bw)
            cdec = jnp.exp(cs[L - 1:L, e:e + 1])
            hs = slice(e * SSD_HD, (e + 1) * SSD_HD)
            h_ref[hs, :] = h_ref[hs, :] * cdec + st
        y = y + xs * dsk_ref[...]
        z = z_ref[pl.ds(r0, L), :].astype(F32)
        u = y * (z * _sigmoid(z))
        ms = jnp.mean(u * u, axis=-1, keepdims=True)
        o_ref[pl.ds(r0, L), :] = (u * lax.rsqrt(ms + EPS) * gout_ref[...]).astype(o_ref.dtype)
        return carry

    lax.fori_loop(0, nc, chunk, 0)


def ssd_mixer(proj, dt_all, cw_g, cb_g, dtb_g, alog_g, dsk_g, gout_g, expand, bsz, s):
    assert s % SSD_CHUNK == 0
    gw = SSD_GW
    z_blk = 0
    xs_blk = SSD_INNER // gw
    b_blk = 2 * SSD_INNER // SSD_STATE
    c_blk = b_blk + SSD_GROUPS

    def seq(width, off):
        return pl.BlockSpec((None, s, width), lambda b, g: (b, 0, off + g))

    def par(rows, width):
        return pl.BlockSpec((None, rows, width), lambda b, g: (g, 0, 0))

    return pl.pallas_call(
        functools.partial(_ssd_kernel, nc=s // SSD_CHUNK),
        grid=(bsz, SSD_GROUPS),
        in_specs=[seq(gw, z_blk), seq(gw, xs_blk), seq(SSD_STATE, b_blk), seq(SSD_STATE, c_blk),
                  seq(LANES, 0),
                  par(SSD_CONV, SSD_CONV_GW), par(1, SSD_CONV_GW), par(1, LANES), par(1, LANES),
                  par(1, gw), par(1, gw),
                  pl.BlockSpec((LANES, gw), lambda b, g: (0, 0))],
        out_specs=seq(gw, 0),
        out_shape=jax.ShapeDtypeStruct((bsz, s, SSD_INNER), BF16),
        scratch_shapes=[pltpu.VMEM((gw, SSD_STATE), F32),
                        pltpu.VMEM((16 + SSD_CHUNK, SSD_CONV_GW), F32)],
        compiler_params=_params("parallel", "parallel"),
        name="ssd_mixer",
    )(proj, proj, proj, proj, dt_all, cw_g, cb_g, dtb_g, alog_g, dsk_g, gout_g, expand)


def _ssd_group_params(conv_w, conv_b, dt_bias, a_log, d_skip, g_out):
    g, gh, gw, n = SSD_GROUPS, SSD_GHEADS, SSD_GW, SSD_STATE

    def conv_cols(a):
        xs = a[..., :SSD_INNER].reshape(a.shape[:-1] + (g, gw))
        bb = a[..., SSD_INNER:SSD_INNER + g * n].reshape(a.shape[:-1] + (g, n))
        cc = a[..., SSD_INNER + g * n:].reshape(a.shape[:-1] + (g, n))
        return jnp.moveaxis(jnp.concatenate([xs, bb, cc], axis=-1), -2, 0)

    cw_g = conv_cols(conv_w.astype(F32))
    cb_g = conv_cols(conv_b.astype(F32)[None, :])

    def per_head(a):
        return jnp.pad(a.astype(F32).reshape(g, 1, gh), ((0, 0), (0, 0), (0, LANES - gh)))

    dsk_g = jnp.repeat(d_skip.astype(F32), SSD_HD).reshape(g, 1, gw)
    gout_g = g_out.astype(F32).reshape(g, 1, gw)
    expand = (jnp.arange(LANES)[:, None] == (jnp.arange(gw)[None, :] // SSD_HD)).astype(BF16)
    return cw_g, cb_g, per_head(dt_bias), per_head(a_log), dsk_g, gout_g, expand


def _router_kernel(x_ref, g_ref, wrt_ref, br_ref, h_ref, route_ref, gates_ref, cnt_ref, carry_ref, *, tm):
    @pl.when(pl.program_id(0) == 0)
    def _():
        carry_ref[...] = jnp.zeros_like(carry_ref)

    ne = N_EXPERTS
    hn = _rms(x_ref[...], g_ref[...])
    h_ref[...] = hn.astype(h_ref.dtype)
    h_hi = hn.astype(BF16)
    h_lo = (hn - h_hi.astype(F32)).astype(BF16)
    wr = wrt_ref[...]
    w_hi = wr.astype(BF16)
    w_lo = (wr - w_hi.astype(F32)).astype(BF16)
    logits = _dot_nt(w_hi, h_hi) + _dot_nt(w_hi, h_lo) + _dot_nt(w_lo, h_hi) + br_ref[:, 0:1]
    sub = lax.broadcasted_iota(jnp.int32, (ne, tm), 0)
    l1 = jnp.max(logits, axis=0, keepdims=True)
    i1 = jnp.min(jnp.where(logits == l1, sub, ne), axis=0, keepdims=True)
    rest = jnp.where(sub == i1, -jnp.inf, logits)
    l2 = jnp.max(rest, axis=0, keepdims=True)
    i2 = jnp.min(jnp.where(rest == l2, sub, ne), axis=0, keepdims=True)
    e2 = jnp.exp(l2 - l1)
    g1 = 1.0 / (1.0 + e2)
    g2 = e2 / (1.0 + e2)
    sel = jnp.where((sub == i1) | (sub == i2), 1.0, 0.0)
    before = (lax.broadcasted_iota(jnp.int32, (tm, tm), 0) < lax.broadcasted_iota(jnp.int32, (tm, tm), 1))
    prefix = _dot(sel.astype(BF16), jnp.where(before, 1.0, 0.0).astype(BF16)) + carry_ref[:, 0:1]
    r1 = jnp.sum(jnp.where(sub == i1, prefix, 0.0), axis=0, keepdims=True)
    r2 = jnp.sum(jnp.where(sub == i2, prefix, 0.0), axis=0, keepdims=True)
    route = jnp.where(sub == 0, i1, jnp.where(sub == 1, i2, jnp.where(
        sub == 2, r1.astype(jnp.int32), jnp.where(sub == 3, r2.astype(jnp.int32), 0))))
    route_ref[...] = route
    gates_ref[...] = jnp.where(sub == 0, g1, jnp.where(sub == 1, g2, 0.0))
    carry_ref[...] = carry_ref[...] + jnp.sum(sel, axis=1, keepdims=True)
    cnt_ref[...] = carry_ref[...]


def moe_router(x, g, w_router, b_router, tm, h_dtype):
    t, d = x.shape
    ne = N_EXPERTS
    wrt = w_router.astype(F32).T
    br = jnp.broadcast_to(b_router.astype(F32)[:, None], (ne, LANES))
    return pl.pallas_call(
        functools.partial(_router_kernel, tm=tm),
        grid=(t // tm,),
        in_specs=[pl.BlockSpec((tm, d), lambda i: (i, 0)),
                  pl.BlockSpec((1, d), lambda i: (0, 0)),
                  pl.BlockSpec((ne, d), lambda i: (0, 0)),
                  pl.BlockSpec((ne, LANES), lambda i: (0, 0))],
        out_specs=[pl.BlockSpec((tm, d), lambda i: (i, 0)),
                   pl.BlockSpec((ne, tm), lambda i: (0, i)),
                   pl.BlockSpec((ne, tm), lambda i: (0, i)),
                   pl.BlockSpec((ne, LANES), lambda i: (0, 0))],
        out_shape=[jax.ShapeDtypeStruct((t, d), h_dtype),
                   jax.ShapeDtypeStruct((ne, t), jnp.int32),
                   jax.ShapeDtypeStruct((ne, t), F32),
                   jax.ShapeDtypeStruct((ne, LANES), F32)],
        scratch_shapes=[pltpu.VMEM((ne, LANES), F32)],
        compiler_params=_params("arbitrary"),
        name="moe_router",
    )(x, g.reshape(1, d), wrt, br)


def _expert_kernel(te_ref, nu_ref, xb_ref, wg_ref, wu_ref, wd_ref, o_ref, acc_ref):
    c = pl.program_id(0)
    f = pl.program_id(1)
    last = pl.num_programs(1) - 1
    used = c < nu_ref[0]

    @pl.when(used)
    def _():
        @pl.when(f == 0)
        def _():
            acc_ref[...] = jnp.zeros_like(acc_ref)

        h = xb_ref[...].astype(BF16)
        a = _dot(h, wg_ref[...].astype(BF16))
        u = _dot(h, wu_ref[...].astype(BF16))
        act = (a * _sigmoid(a) * u).astype(BF16)
        acc_ref[...] += _dot(act, wd_ref[...].astype(BF16))

        @pl.when(f == last)
        def _():
            o_ref[...] = acc_ref[...]

    @pl.when(jnp.logical_and(jnp.logical_not(used), f == last))
    def _():
        o_ref[...] = jnp.zeros_like(o_ref)


def expert_ffn(tile_expert, n_used, xb, wg, wu, wd, tf):
    n_buf, d = xb.shape
    ff = wg.shape[2]
    r = EXPERT_ROWS
    grid_spec = pltpu.PrefetchScalarGridSpec(
        num_scalar_prefetch=2,
        grid=(n_buf // r, ff // tf),
        in_specs=[pl.BlockSpec((r, d), lambda c, f, te, nu: (c, 0)),
                  pl.BlockSpec((None, d, tf), lambda c, f, te, nu: (te[c], 0, f)),
                  pl.BlockSpec((None, d, tf), lambda c, f, te, nu: (te[c], 0, f)),
                  pl.BlockSpec((None, tf, d), lambda c, f, te, nu: (te[c], f, 0))],
        out_specs=pl.BlockSpec((r, d), lambda c, f, te, nu: (c, 0)),
        scratch_shapes=[pltpu.VMEM((r, d), F32)],
    )
    return pl.pallas_call(
        _expert_kernel,
        grid_spec=grid_spec,
        out_shape=jax.ShapeDtypeStruct((n_buf, d), F32),
        compiler_params=_params("parallel", "arbitrary"),
        name="expert_ffn",
    )(tile_expert, n_used, xb, wg, wu, wd)


def _combine_kernel(x_ref, y1_ref, y2_ref, g1_ref, g2_ref, g_ref, o_ref):
    y = x_ref[...] + (g1_ref[...] * y1_ref[...] + g2_ref[...] * y2_ref[...])
    o_ref[...] = _rms(y, g_ref[...])


def combine_final(x, y1, y2, g1, g2, g, tm):
    t, d = x.shape
    row = pl.BlockSpec((tm, d), lambda i: (i, 0))
    colv = pl.BlockSpec((tm, 1), lambda i: (i, 0))
    return pl.pallas_call(
        _combine_kernel,
        grid=(t // tm,),
        in_specs=[row, row, row, colv, colv, pl.BlockSpec((1, d), lambda i: (0, 0))],
        out_specs=row,
        out_shape=jax.ShapeDtypeStruct((t, d), F32),
        compiler_params=_params("parallel"),
        name="combine_final",
    )(x, y1, y2, g1, g2, g.reshape(1, d))


def _pick(n, prefs):
    for p in prefs:
        if n % p == 0:
            return p
    return n


def kernel(x, mem, g_mix, g_mem, w_in_moba, w_in_ssd, w_mem_kv, w_out, rel_bias, conv_w, conv_b, dt_bias, a_log, d_skip, g_ssd_out, g_ffn, w_ffn_gate, w_ffn_up, w_ffn_down, w_router, b_router, w_exp_gate, w_exp_up, w_exp_down, g_final):
    bsz, s, d = x.shape
    t = bsz * s
    m_len = mem.shape[1]
    tm = _pick(t, (1024, 512, 256))
    xf = x.reshape(t, d).astype(F32)
    memf = mem.reshape(bsz * m_len, d).astype(F32)
    tmm = _pick(bsz * m_len, (1024, 512, 256))

    proj0 = norm_matmul(xf, g_mix[0], w_in_moba[0].astype(BF16), BF16, tm, 1280)
    kv0 = norm_matmul(memf, g_mem[0], w_mem_kv[0].astype(BF16), BF16, tmm, 1024)
    proj0 = proj0.reshape(bsz, s, -1)
    y_tok = moba_attention(proj0, _moba_bias_tables(rel_bias), bsz, s)
    y_mem = memory_attention(proj0, 3 * MOBA_HEADS, kv0.reshape(bsz, m_len, -1), bsz, s)
    wo = w_out[0].astype(BF16)
    x1 = out_projection(xf, y_tok.reshape(t, -1), y_mem.reshape(t, -1), wo[:MOBA_WIDTH], wo[MOBA_WIDTH:], 512)
    x1 = dense_ffn(x1, g_ffn[0], w_ffn_gate[0].astype(BF16), w_ffn_up[0].astype(BF16),
                   w_ffn_down[0].astype(BF16), tm, 512)

    n_zx = SSD_INNER + SSD_INNER + 2 * SSD_GROUPS * SSD_STATE
    w1 = w_in_ssd[0]
    w_main = jnp.concatenate([w1[:, :n_zx], w1[:, n_zx + SSD_HEADS:]], axis=1).astype(BF16)
    w_dt = jnp.pad(w1[:, n_zx:n_zx + SSD_HEADS].reshape(d, SSD_GROUPS, SSD_GHEADS),
                   ((0, 0), (0, 0), (0, LANES - SSD_GHEADS))).reshape(d, SSD_GROUPS * LANES).astype(BF16)
    proj1 = norm_matmul(x1, g_mix[1], w_main, BF16, tm, 1536).reshape(bsz, s, -1)
    dt_all = norm_matmul(x1, g_mix[1], w_dt, F32, tm, SSD_GROUPS * LANES).reshape(bsz, s, -1)
    kv1 = norm_matmul(memf, g_mem[1], w_mem_kv[1].astype(BF16), BF16, tmm, 1024)
    y_tok = ssd_mixer(proj1, dt_all, *_ssd_group_params(conv_w[0], conv_b[0], dt_bias[0], a_log[0],
                                                         d_skip[0], g_ssd_out[0]), bsz, s)
    y_mem = memory_attention(proj1, n_zx // HD, kv1.reshape(bsz, m_len, -1), bsz, s)
    wo = w_out[1].astype(BF16)
    x2 = out_projection(x1, y_tok.reshape(t, -1), y_mem.reshape(t, -1), wo[:SSD_INNER], wo[SSD_INNER:], 512)

    h, route, gates, counts = moe_router(x2, g_ffn[1], w_router[0], b_router[0], 512, BF16)
    r = EXPERT_ROWS
    n_buf = t * TOP_K + N_EXPERTS * r
    n_tiles = n_buf // r
    cnt = counts[:, 0].astype(jnp.int32)
    padded = (cnt + r - 1) // r * r
    ends = jnp.cumsum(padded)
    starts = ends - padded
    ids = route[0:2]
    dest = jnp.sum(jnp.where(ids[:, :, None] == jnp.arange(N_EXPERTS)[None, None, :],
                             starts[None, None, :], 0), axis=-1) + route[2:4]
    tile_start = jnp.arange(n_tiles, dtype=jnp.int32) * r
    n_used = (ends[-1] // r).astype(jnp.int32)
    tile_expert = jnp.sum(tile_start[:, None] >= ends[None, :], axis=1).astype(jnp.int32)
    last_expert = jnp.sum(jnp.maximum(n_used - 1, 0) * r >= ends).astype(jnp.int32)
    tile_expert = jnp.where(jnp.arange(n_tiles) < n_used, tile_expert, last_expert)
    tok = jnp.arange(t, dtype=jnp.int32)
    src = jnp.zeros((n_buf,), jnp.int32).at[dest.reshape(-1)].set(jnp.concatenate([tok, tok]))
    xb = h[src]
    yb = expert_ffn(tile_expert, n_used.reshape(1), xb, w_exp_gate[0], w_exp_up[0], w_exp_down[0], 512)
    out = combine_final(x2, yb[dest[0]], yb[dest[1]], gates[0][:, None], gates[1][:, None], g_final, 512)
    return out.reshape(bsz, s, d).astype(x.dtype)
```

```python
import functools
import math

import jax
import jax.numpy as jnp
from jax import lax
from jax.experimental import pallas as pl
from jax.experimental.pallas import tpu as pltpu

F32 = jnp.float32
BF16 = jnp.bfloat16

D_MODEL = 1024
HD = 128
MOBA_HEADS = 12
MOBA_WIDTH = MOBA_HEADS * HD
MOBA_BLOCK = 256
MOBA_TOPK = 3
REL_BUCKETS = 32
REL_MAX_DIST = 128
MEM_HEADS = 4
MEM_WIDTH = MEM_HEADS * HD
SSD_HEADS = 24
SSD_HD = 64
SSD_INNER = SSD_HEADS * SSD_HD
SSD_GROUPS = 4
SSD_STATE = 128
SSD_CONV = 4
SSD_CHUNK = 128
SSD_GHEADS = SSD_HEADS // SSD_GROUPS
SSD_GW = SSD_GHEADS * SSD_HD
SSD_CONV_GW = SSD_GW + 2 * SSD_STATE
D_FF = 3584
N_EXPERTS = 8
TOP_K = 2
EPS = 1e-6

LOG2E = math.log2(math.e)
LANES = 128
SUBLANES = 8
VMEM_LIMIT = 48 * 1024 * 1024
EXPERT_ROWS = 1024


def _dot(a, b):
    return jnp.dot(a, b, preferred_element_type=F32)


def _dot_nt(a, b):
    return lax.dot_general(a, b, (((1,), (1,)), ((), ())), preferred_element_type=F32)


def _split3(x):
    p0 = x.astype(BF16)
    r0 = x - p0.astype(F32)
    p1 = r0.astype(BF16)
    p2 = (r0 - p1.astype(F32)).astype(BF16)
    return p0, p1, p2


def _dot3l(x, w):
    return sum(_dot(p, w) for p in _split3(x))


def _dot3r(w, x):
    return sum(_dot(w, p) for p in _split3(x))


def _sigmoid(x):
    return 1.0 / (1.0 + jnp.exp(-x))


def _rms(x, g):
    ms = jnp.mean(x * x, axis=-1, keepdims=True)
    return x * lax.rsqrt(ms + EPS) * g


def _rms_rows_to(h_ref, x_ref, g_ref, rows):
    step = min(rows, 256)
    for r in range(0, rows, step):
        h_ref[r:r + step, :] = _rms(x_ref[r:r + step, :], g_ref[...]).astype(h_ref.dtype)


def _params(*sem):
    return pltpu.CompilerParams(dimension_semantics=sem, vmem_limit_bytes=VMEM_LIMIT)


def _normmm_kernel(x_ref, g_ref, w_ref, o_ref, h_ref, *, tm):
    @pl.when(pl.program_id(1) == 0)
    def _():
        _rms_rows_to(h_ref, x_ref, g_ref, tm)

    o_ref[...] = _dot(h_ref[...], w_ref[...]).astype(o_ref.dtype)


def norm_matmul(x, g, w, out_dtype, tm, tn):
    t, d = x.shape
    n = w.shape[1]
    return pl.pallas_call(
        functools.partial(_normmm_kernel, tm=tm),
        grid=(t // tm, n // tn),
        in_specs=[pl.BlockSpec((tm, d), lambda i, j: (i, 0)),
                  pl.BlockSpec((1, d), lambda i, j: (0, 0)),
                  pl.BlockSpec((d, tn), lambda i, j: (0, j))],
        out_specs=pl.BlockSpec((tm, tn), lambda i, j: (i, j)),
        out_shape=jax.ShapeDtypeStruct((t, n), out_dtype),
        scratch_shapes=[pltpu.VMEM((tm, d), BF16)],
        compiler_params=_params("parallel", "arbitrary"),
        name="norm_matmul",
    )(x, g.reshape(1, d), w)


def _moba_kernel(q_ref, k_ref, v_ref, tab_ref, o_ref, vt_ref, km_ref, *, nb, n_sel):
    blk = MOBA_BLOCK
    scale = HD ** -0.5 * LOG2E
    nbp = km_ref.shape[0]
    km_ref[...] = jnp.zeros_like(km_ref)
    for j in range(nb):
        rows = slice(j * blk, (j + 1) * blk)
        vt_ref[:, rows] = v_ref[rows, :].astype(F32).T.astype(BF16)
        km_ref[j:j + 1, :] = jnp.mean(k_ref[rows, :].astype(F32), axis=0, keepdims=True)
    km = km_ref[...]
    km_hi = km.astype(BF16)
    km_lo = (km - km_hi.astype(F32)).astype(BF16)
    q_all = q_ref[...]
    gate = _dot_nt(km_hi, q_all) + _dot_nt(km_lo, q_all)
    sub = lax.broadcasted_iota(jnp.int32, (nbp, blk), 0)
    bias_far = tab_ref[2, 0:1, :]

    for i in range(nb):
        rows = slice(i * blk, (i + 1) * blk)
        nk = (i + 1) * blk
        qi = q_ref[rows, :]
        s_all = _dot_nt(k_ref[0:nk, :], qi)
        if i > 0:
            valid = sub < i
            gm = jnp.where(valid, gate[:, rows], -jnp.inf)
            rank = jnp.zeros((nbp, blk), F32)
            for jp in range(i):
                row = gm[jp:jp + 1, :]
                beats = (row > gm) | ((row == gm) & (sub > jp))
                rank = rank + jnp.where(beats, 1.0, 0.0)
            selm = jnp.where(valid & (rank < n_sel), 0.0, -jnp.inf)
        bands = []
        for j in range(i + 1):
            sj = s_all[j * blk:(j + 1) * blk, :] * scale
            if j == i:
                sj = sj + tab_ref[0]
            elif j == i - 1:
                sj = sj + (tab_ref[1] + selm[j:j + 1, :])
            else:
                sj = sj + (bias_far + selm[j:j + 1, :])
            bands.append(sj)
        s = jnp.concatenate(bands, axis=0) if i > 0 else bands[0]
        m = jnp.max(s, axis=0, keepdims=True)
        p = jnp.exp2(s - m)
        l = jnp.sum(p, axis=0, keepdims=True)
        acc = _dot(vt_ref[:, 0:nk], p.astype(BF16))
        o_ref[rows, :] = (acc / l).T.astype(o_ref.dtype)


def _t5_bucket_idx(dist):
    n = jnp.maximum(dist, 0)
    max_exact = REL_BUCKETS // 2
    large = max_exact + (jnp.log(jnp.maximum(n, 1).astype(F32) / max_exact)
                         / math.log(REL_MAX_DIST / max_exact)
                         * (REL_BUCKETS - max_exact)).astype(jnp.int32)
    large = jnp.minimum(large, REL_BUCKETS - 1)
    return jnp.where(n < max_exact, n, large)


def _bias_table_kernel(rb_ref, idx_ref, o_ref):
    h = pl.program_id(0)
    for t in range(3):
        idx = idx_ref[t]
        acc = jnp.full(idx.shape, -jnp.inf, F32)
        for b in range(REL_BUCKETS):
            acc = jnp.where(idx == b, rb_ref[h, b], acc)
        o_ref[t] = acc * LOG2E


def _moba_bias_tables(rel_bias):
    blk = MOBA_BLOCK
    loc = jnp.arange(blk)
    d0 = loc[None, :] - loc[:, None]
    idx = jnp.stack([jnp.where(d0 >= 0, _t5_bucket_idx(d0), -1),
                     _t5_bucket_idx(d0 + blk),
                     _t5_bucket_idx(d0 + 2 * blk)]).astype(jnp.int32)
    return pl.pallas_call(
        _bias_table_kernel,
        grid=(MOBA_HEADS,),
        in_specs=[pl.BlockSpec(memory_space=pltpu.SMEM),
                  pl.BlockSpec((3, blk, blk), lambda h: (0, 0, 0))],
        out_specs=pl.BlockSpec((None, 3, blk, blk), lambda h: (h, 0, 0, 0)),
        out_shape=jax.ShapeDtypeStruct((MOBA_HEADS, 3, blk, blk), F32),
        compiler_params=_params("parallel"),
        name="moba_bias_tables",
    )(rel_bias.T.astype(F32), idx)


def moba_attention(proj, tabs, bsz, s):
    assert s % MOBA_BLOCK == 0
    nb = s // MOBA_BLOCK
    nbp = -(-nb // 8) * 8
    n_sel = min(MOBA_TOPK, nb - 1)
    blk = MOBA_BLOCK

    def col(off):
        return pl.BlockSpec((None, s, HD), lambda h, b: (b, 0, off + h))

    return pl.pallas_call(
        functools.partial(_moba_kernel, nb=nb, n_sel=n_sel),
        grid=(MOBA_HEADS, bsz),
        in_specs=[col(0), col(MOBA_HEADS), col(2 * MOBA_HEADS),
                  pl.BlockSpec((None, 3, blk, blk), lambda h, b: (h, 0, 0, 0))],
        out_specs=pl.BlockSpec((None, s, HD), lambda h, b: (b, 0, h)),
        out_shape=jax.ShapeDtypeStruct((bsz, s, MOBA_WIDTH), BF16),
        scratch_shapes=[pltpu.VMEM((HD, s), BF16),
                        pltpu.VMEM((nbp, HD), F32)],
        compiler_params=_params("parallel", "parallel"),
        name="moba_attention",
    )(proj, proj, proj, tabs)


def _memattn_kernel(q_ref, k_ref, v_ref, o_ref, *, n_chunks, qc):
    scale = HD ** -0.5 * LOG2E
    k = k_ref[...]
    vt = v_ref[...].astype(F32).T.astype(BF16)
    for c in range(n_chunks):
        rows = slice(c * qc, (c + 1) * qc)
        s = _dot_nt(k, q_ref[rows, :]) * scale
        m = jnp.max(s, axis=0, keepdims=True)
        p = jnp.exp2(s - m)
        l = jnp.sum(p, axis=0, keepdims=True)
        o = _dot(vt, p.astype(BF16)) / l
        o_ref[rows, :] = o.T.astype(o_ref.dtype)


def memory_attention(proj, q_off, kv, bsz, s):
    m_len = kv.shape[1]
    qc = 256
    return pl.pallas_call(
        functools.partial(_memattn_kernel, n_chunks=s // qc, qc=qc),
        grid=(bsz, MEM_HEADS),
        in_specs=[pl.BlockSpec((None, s, HD), lambda b, h: (b, 0, q_off + h)),
                  pl.BlockSpec((None, m_len, HD), lambda b, h: (b, 0, h)),
                  pl.BlockSpec((None, m_len, HD), lambda b, h: (b, 0, MEM_HEADS + h))],
        out_specs=pl.BlockSpec((None, s, HD), lambda b, h: (b, 0, h)),
        out_shape=jax.ShapeDtypeStruct((bsz, s, MEM_WIDTH), BF16),
        compiler_params=_params("parallel", "parallel"),
        name="memory_attention",
    )(proj, kv, kv)


def _outproj_kernel(x_ref, ya_ref, yb_ref, wa_ref, wb_ref, o_ref):
    o_ref[...] = x_ref[...] + _dot(ya_ref[...], wa_ref[...]) + _dot(yb_ref[...], wb_ref[...])


def out_projection(x, ya, yb, wa, wb, tm):
    t, d = x.shape
    ka, kb = ya.shape[1], yb.shape[1]
    return pl.pallas_call(
        _outproj_kernel,
        grid=(t // tm,),
        in_specs=[pl.BlockSpec((tm, d), lambda i: (i, 0)),
                  pl.BlockSpec((tm, ka), lambda i: (i, 0)),
                  pl.BlockSpec((tm, kb), lambda i: (i, 0)),
                  pl.BlockSpec((ka, d), lambda i: (0, 0)),
                  pl.BlockSpec((kb, d), lambda i: (0, 0))],
        out_specs=pl.BlockSpec((tm, d), lambda i: (i, 0)),
        out_shape=jax.ShapeDtypeStruct((t, d), F32),
        compiler_params=_params("parallel"),
        name="out_projection",
    )(x, ya, yb, wa, wb)


def _ffn_kernel(x_ref, g_ref, wg_ref, wu_ref, wd_ref, o_ref, h_ref, acc_ref, *, tm):
    f = pl.program_id(1)

    @pl.when(f == 0)
    def _():
        _rms_rows_to(h_ref, x_ref, g_ref, tm)
        acc_ref[...] = x_ref[...]

    h = h_ref[...]
    a = _dot(h, wg_ref[...])
    u = _dot(h, wu_ref[...])
    act = (a * _sigmoid(a) * u).astype(BF16)
    acc_ref[...] += _dot(act, wd_ref[...])

    @pl.when(f == pl.num_programs(1) - 1)
    def _():
        o_ref[...] = acc_ref[...]


def dense_ffn(x, g, wg, wu, wd, tm, tf):
    t, d = x.shape
    ff = wg.shape[1]
    return pl.pallas_call(
        functools.partial(_ffn_kernel, tm=tm),
        grid=(t // tm, ff // tf),
        in_specs=[pl.BlockSpec((tm, d), lambda i, f: (i, 0)),
                  pl.BlockSpec((1, d), lambda i, f: (0, 0)),
                  pl.BlockSpec((d, tf), lambda i, f: (0, f)),
                  pl.BlockSpec((d, tf), lambda i, f: (0, f)),
                  pl.BlockSpec((tf, d), lambda i, f: (f, 0))],
        out_specs=pl.BlockSpec((tm, d), lambda i, f: (i, 0)),
        out_shape=jax.ShapeDtypeStruct((t, d), F32),
        scratch_shapes=[pltpu.VMEM((tm, d), BF16), pltpu.VMEM((tm, d), F32)],
        compiler_params=_params("parallel", "arbitrary"),
        name="dense_ffn",
    )(x, g.reshape(1, d), wg, wu, wd)


def _ssd_kernel(z_ref, xs_ref, b_ref, c_ref, dt_ref, cw_ref, cb_ref, dtb_ref, alog_ref, dsk_ref,
                gout_ref, exp_ref, o_ref, h_ref, pad_ref, *, nc):
    L = SSD_CHUNK
    gw = SSD_GW
    n = SSD_STATE
    h_ref[...] = jnp.zeros_like(h_ref)
    a_neg = -jnp.exp(alog_ref[...])
    causal = (lax.broadcasted_iota(jnp.int32, (L, L), 0) >= lax.broadcasted_iota(jnp.int32, (L, L), 1))
    tri = jnp.where(causal, 1.0, 0.0).astype(BF16)
    lane_head = lax.broadcasted_iota(jnp.int32, (1, gw), 1) // SSD_HD
    expm = exp_ref[...]
    srcs = ((xs_ref, 0, gw), (b_ref, gw, gw + n), (c_ref, gw + n, gw + 2 * n))

    def chunk(c, carry):
        r0 = pl.multiple_of(c * L, L)
        rp = pl.multiple_of(jnp.maximum(r0 - 16, 0), 16)
        for ref, lo, hi in srcs:
            pad_ref[0:16, lo:hi] = jnp.where(c > 0, ref[pl.ds(rp, 16), :].astype(F32), 0.0)
            pad_ref[16:16 + L, lo:hi] = ref[pl.ds(r0, L), :].astype(F32)
        conv = cb_ref[...]
        for k in range(SSD_CONV):
            off = 16 - (SSD_CONV - 1) + k
            conv = conv + cw_ref[k:k + 1, :] * pad_ref[off:off + L, :]
        act = conv * _sigmoid(conv)
        xs = act[:, 0:gw]
        bm = act[:, gw:gw + n]
        cm = act[:, gw + n:gw + 2 * n]

        dtr = dt_ref[pl.ds(r0, L), :] + dtb_ref[...]
        dt = jnp.maximum(dtr, 0.0) + jnp.log(1.0 + jnp.exp(-jnp.abs(dtr)))
        la = dt * a_neg
        cs = _dot3r(tri, la)
        cs_t = cs.T
        dt_t = dt.T
        bm16 = bm.astype(BF16)
        cm16 = cm.astype(BF16)
        scores = _dot_nt(cm16, bm16)
        y = _dot_nt(cm16, h_ref[...].astype(BF16)) * _dot3l(jnp.exp(cs), expm)
        w_end = dt * jnp.exp(cs[L - 1:L, :] - cs)
        xs_t = xs.T
        for e in range(SSD_GHEADS):
            diff = cs[:, e:e + 1] - cs_t[e:e + 1, :]
            dec = jnp.exp(jnp.where(causal, diff, -jnp.inf))
            mm = (scores * dec * dt_t[e:e + 1, :]).astype(BF16)
            xe = jnp.where(lane_head == e, xs, 0.0).astype(BF16)
            y = y + _dot(mm, xe)
            bw = (bm * w_end[:, e:e + 1]).astype(BF16)
            st = _dot(xs_t[e * SSD_HD:(e + 1) * SSD_HD, :].astype(BF16), bw)
            cdec = jnp.exp(cs[L - 1:L, e:e + 1])
            hs = slice(e * SSD_HD, (e + 1) * SSD_HD)
            h_ref[hs, :] = h_ref[hs, :] * cdec + st
        y = y + xs * dsk_ref[...]
        z = z_ref[pl.ds(r0, L), :].astype(F32)
        u = y * (z * _sigmoid(z))
        ms = jnp.mean(u * u, axis=-1, keepdims=True)
        o_ref[pl.ds(r0, L), :] = (u * lax.rsqrt(ms + EPS) * gout_ref[...]).astype(o_ref.dtype)
        return carry

    lax.fori_loop(0, nc, chunk, 0)


def ssd_mixer(proj, dt_all, cw_g, cb_g, dtb_g, alog_g, dsk_g, gout_g, expand, bsz, s):
    assert s % SSD_CHUNK == 0
    gw = SSD_GW
    z_blk = 0
    xs_blk = SSD_INNER // gw
    b_blk = 2 * SSD_INNER // SSD_STATE
    c_blk = b_blk + SSD_GROUPS

    def seq(width, off):
        return pl.BlockSpec((None, s, width), lambda b, g: (b, 0, off + g))

    def par(rows, width):
        return pl.BlockSpec((None, rows, width), lambda b, g: (g, 0, 0))

    return pl.pallas_call(
        functools.partial(_ssd_kernel, nc=s // SSD_CHUNK),
        grid=(bsz, SSD_GROUPS),
        in_specs=[seq(gw, z_blk), seq(gw, xs_blk), seq(SSD_STATE, b_blk), seq(SSD_STATE, c_blk),
                  seq(LANES, 0),
                  par(SSD_CONV, SSD_CONV_GW), par(1, SSD_CONV_GW), par(1, LANES), par(1, LANES),
                  par(1, gw), par(1, gw),
                  pl.BlockSpec((LANES, gw), lambda b, g: (0, 0))],
        out_specs=seq(gw, 0),
        out_shape=jax.ShapeDtypeStruct((bsz, s, SSD_INNER), BF16),
        scratch_shapes=[pltpu.VMEM((gw, SSD_STATE), F32),
                        pltpu.VMEM((16 + SSD_CHUNK, SSD_CONV_GW), F32)],
        compiler_params=_params("parallel", "parallel"),
        name="ssd_mixer",
    )(proj, proj, proj, proj, dt_all, cw_g, cb_g, dtb_g, alog_g, dsk_g, gout_g, expand)


def _ssd_group_params(conv_w, conv_b, dt_bias, a_log, d_skip, g_out):
    g, gh, gw, n = SSD_GROUPS, SSD_GHEADS, SSD_GW, SSD_STATE

    def conv_cols(a):
        xs = a[..., :SSD_INNER].reshape(a.shape[:-1] + (g, gw))
        bb = a[..., SSD_INNER:SSD_INNER + g * n].reshape(a.shape[:-1] + (g, n))
        cc = a[..., SSD_INNER + g * n:].reshape(a.shape[:-1] + (g, n))
        return jnp.moveaxis(jnp.concatenate([xs, bb, cc], axis=-1), -2, 0)

    cw_g = conv_cols(conv_w.astype(F32))
    cb_g = conv_cols(conv_b.astype(F32)[None, :])

    def per_head(a):
        return jnp.pad(a.astype(F32).reshape(g, 1, gh), ((0, 0), (0, 0), (0, LANES - gh)))

    dsk_g = jnp.repeat(d_skip.astype(F32), SSD_HD).reshape(g, 1, gw)
    gout_g = g_out.astype(F32).reshape(g, 1, gw)
    expand = (jnp.arange(LANES)[:, None] == (jnp.arange(gw)[None, :] // SSD_HD)).astype(BF16)
    return cw_g, cb_g, per_head(dt_bias), per_head(a_log), dsk_g, gout_g, expand


def _router_kernel(x_ref, g_ref, wrt_ref, br_ref, h_ref, route_ref, gates_ref, cnt_ref, carry_ref, *, tm):
    @pl.when(pl.program_id(0) == 0)
    def _():
        carry_ref[...] = jnp.zeros_like(carry_ref)

    ne = N_EXPERTS
    hn = _rms(x_ref[...], g_ref[...])
    h_ref[...] = hn.astype(h_ref.dtype)
    h_hi = hn.astype(BF16)
    h_lo = (hn - h_hi.astype(F32)).astype(BF16)
    wr = wrt_ref[...]
    w_hi = wr.astype(BF16)
    w_lo = (wr - w_hi.astype(F32)).astype(BF16)
    logits = _dot_nt(w_hi, h_hi) + _dot_nt(w_hi, h_lo) + _dot_nt(w_lo, h_hi) + br_ref[:, 0:1]
    sub = lax.broadcasted_iota(jnp.int32, (ne, tm), 0)
    l1 = jnp.max(logits, axis=0, keepdims=True)
    i1 = jnp.min(jnp.where(logits == l1, sub, ne), axis=0, keepdims=True)
    rest = jnp.where(sub == i1, -jnp.inf, logits)
    l2 = jnp.max(rest, axis=0, keepdims=True)
    i2 = jnp.min(jnp.where(rest == l2, sub, ne), axis=0, keepdims=True)
    e2 = jnp.exp(l2 - l1)
    g1 = 1.0 / (1.0 + e2)
    g2 = e2 / (1.0 + e2)
    sel = jnp.where((sub == i1) | (sub == i2), 1.0, 0.0)
    before = (lax.broadcasted_iota(jnp.int32, (tm, tm), 0) < lax.broadcasted_iota(jnp.int32, (tm, tm), 1))
    prefix = _dot(sel.astype(BF16), jnp.where(before, 1.0, 0.0).astype(BF16)) + carry_ref[:, 0:1]
    r1 = jnp.sum(jnp.where(sub == i1, prefix, 0.0), axis=0, keepdims=True)
    r2 = jnp.sum(jnp.where(sub == i2, prefix, 0.0), axis=0, keepdims=True)
    route = jnp.where(sub == 0, i1, jnp.where(sub == 1, i2, jnp.where(
        sub == 2, r1.astype(jnp.int32), jnp.where(sub == 3, r2.astype(jnp.int32), 0))))
    route_ref[...] = route
    gates_ref[...] = jnp.where(sub == 0, g1, jnp.where(sub == 1, g2, 0.0))
    carry_ref[...] = carry_ref[...] + jnp.sum(sel, axis=1, keepdims=True)
    cnt_ref[...] = carry_ref[...]


def moe_router(x, g, w_router, b_router, tm, h_dtype):
    t, d = x.shape
    ne = N_EXPERTS
    wrt = w_router.astype(F32).T
    br = jnp.broadcast_to(b_router.astype(F32)[:, None], (ne, LANES))
    return pl.pallas_call(
        functools.partial(_router_kernel, tm=tm),
        grid=(t // tm,),
        in_specs=[pl.BlockSpec((tm, d), lambda i: (i, 0)),
                  pl.BlockSpec((1, d), lambda i: (0, 0)),
                  pl.BlockSpec((ne, d), lambda i: (0, 0)),
                  pl.BlockSpec((ne, LANES), lambda i: (0, 0))],
        out_specs=[pl.BlockSpec((tm, d), lambda i: (i, 0)),
                   pl.BlockSpec((ne, tm), lambda i: (0, i)),
                   pl.BlockSpec((ne, tm), lambda i: (0, i)),
                   pl.BlockSpec((ne, LANES), lambda i: (0, 0))],
        out_shape=[jax.ShapeDtypeStruct((t, d), h_dtype),
                   jax.ShapeDtypeStruct((ne, t), jnp.int32),
                   jax.ShapeDtypeStruct((ne, t), F32),
                   jax.ShapeDtypeStruct((ne, LANES), F32)],
        scratch_shapes=[pltpu.VMEM((ne, LANES), F32)],
        compiler_params=_params("arbitrary"),
        name="moe_router",
    )(x, g.reshape(1, d), wrt, br)


def _row_copy(src_ref, src_row, dst_ref, dst_row, sem):
    return pltpu.make_async_copy(src_ref.at[pl.ds(src_row, 1), :], dst_ref.at[pl.ds(dst_row, 1), :], sem)


def _dispatch_kernel(dest_ref, meta_ref, h_ref, xb_ref, zero_ref, sem, zsem, *, tm, n_tiles):
    def issue(r, carry):
        for k in range(TOP_K):
            _row_copy(h_ref, r, xb_ref, dest_ref[k * tm + r], sem).start()
        return carry

    lax.fori_loop(0, tm, issue, 0)

    @pl.when(pl.program_id(0) == pl.num_programs(0) - 1)
    def _():
        zero_ref[...] = jnp.zeros_like(zero_ref)
        zrows = zero_ref.shape[0]
        for wait in (False, True):
            def go(cp, cond):
                @pl.when(cond)
                def _():
                    if wait:
                        cp.wait()
                    else:
                        cp.start()

            for e in range(N_EXPERTS):
                c = meta_ref[e]
                first = meta_ref[N_EXPERTS + e]
                c8 = (c + (SUBLANES - 1)) & (-SUBLANES)
                for j in range(SUBLANES - 1):
                    go(_row_copy(zero_ref, 0, xb_ref, first + c + j, zsem), j < c8 - c)
                tail = (-c8) & (EXPERT_ROWS - 1)
                for b in range(SUBLANES.bit_length() - 1, EXPERT_ROWS.bit_length() - 1):
                    n = 1 << b
                    row0 = pl.multiple_of(first + c8 + (tail & (n - 1)), SUBLANES)
                    go(pltpu.make_async_copy(zero_ref.at[pl.ds(0, n), :], xb_ref.at[pl.ds(row0, n), :], zsem),
                       ((tail >> b) & 1) == 1)
            per = EXPERT_ROWS // zrows
            first_unused = meta_ref[2 * N_EXPERTS] * per
            for j in range(N_EXPERTS * per):
                row0 = pl.multiple_of(jnp.minimum(first_unused + j, n_tiles * per - 1) * zrows, zrows)
                go(pltpu.make_async_copy(zero_ref, xb_ref.at[pl.ds(row0, zrows), :], zsem),
                   first_unused + j < n_tiles * per)

    def drain(r, carry):
        for k in range(TOP_K):
            _row_copy(h_ref, 0, xb_ref, 0, sem).wait()
        return carry

    lax.fori_loop(0, tm, drain, 0)


def moe_dispatch(dest_tiles, meta, h, tm, n_tiles):
    t, d = h.shape
    return pl.pallas_call(
        functools.partial(_dispatch_kernel, tm=tm, n_tiles=n_tiles),
        grid=(t // tm,),
        in_specs=[pl.BlockSpec((TOP_K * tm,), lambda i: (i,), memory_space=pltpu.SMEM),
                  pl.BlockSpec(memory_space=pltpu.SMEM),
                  pl.BlockSpec((tm, d), lambda i: (i, 0))],
        out_specs=pl.BlockSpec(memory_space=pl.ANY),
        out_shape=jax.ShapeDtypeStruct((n_tiles * EXPERT_ROWS, d), F32),
        scratch_shapes=[pltpu.VMEM((EXPERT_ROWS // 2, d), F32),
                        pltpu.SemaphoreType.DMA(()), pltpu.SemaphoreType.DMA(())],
        compiler_params=_params("arbitrary"),
        name="moe_dispatch",
    )(dest_tiles, meta, h)


def _expert_kernel(te_ref, nu_ref, xb_ref, wg_ref, wu_ref, wd_ref, o_ref, acc_ref):
    c = pl.program_id(0)
    f = pl.program_id(1)
    last = pl.num_programs(1) - 1
    used = c < nu_ref[0]

    @pl.when(used)
    def _():
        @pl.when(f == 0)
        def _():
            acc_ref[...] = jnp.zeros_like(acc_ref)

        h = xb_ref[...].astype(BF16)
        a = _dot(h, wg_ref[...].astype(BF16))
        u = _dot(h, wu_ref[...].astype(BF16))
        act = (a * _sigmoid(a) * u).astype(BF16)
        acc_ref[...] += _dot(act, wd_ref[...].astype(BF16))

        @pl.when(f == last)
        def _():
            o_ref[...] = acc_ref[...]

    @pl.when(jnp.logical_and(jnp.logical_not(used), f == last))
    def _():
        o_ref[...] = jnp.zeros_like(o_ref)


def expert_ffn(tile_expert, n_used, xb, wg, wu, wd, tf):
    n_rows, d = xb.shape
    ff = wg.shape[2]
    r = EXPERT_ROWS
    nf = ff // tf

    def wcol(c, f, te, nu):
        return (te[c], 0, jnp.where(c < nu[0], f, nf - 1))

    def wrow(c, f, te, nu):
        return (te[c], jnp.where(c < nu[0], f, nf - 1), 0)

    grid_spec = pltpu.PrefetchScalarGridSpec(
        num_scalar_prefetch=2,
        grid=(n_rows // r, nf),
        in_specs=[pl.BlockSpec((r, d), lambda c, f, te, nu: (c, 0)),
                  pl.BlockSpec((None, d, tf), wcol),
                  pl.BlockSpec((None, d, tf), wcol),
                  pl.BlockSpec((None, tf, d), wrow)],
        out_specs=pl.BlockSpec((r, d), lambda c, f, te, nu: (c, 0)),
        scratch_shapes=[pltpu.VMEM((r, d), F32)],
    )
    return pl.pallas_call(
        _expert_kernel,
        grid_spec=grid_spec,
        out_shape=jax.ShapeDtypeStruct((n_rows, d), F32),
        compiler_params=_params("parallel", "arbitrary"),
        name="expert_ffn",
    )(tile_expert, n_used, xb, wg, wu, wd)


def _combine_kernel(dest_ref, x_ref, gates_ref, g_ref, yb_ref, o_ref, buf_ref, sem, *, tm):
    def issue(r, carry):
        for k in range(TOP_K):
            _row_copy(yb_ref, dest_ref[k * tm + r], buf_ref.at[k], r, sem).start()
        return carry

    lax.fori_loop(0, tm, issue, 0)
    g8 = gates_ref[...]
    gt = jnp.concatenate([g8, jnp.zeros((LANES - g8.shape[0], tm), F32)], axis=0).T

    def drain(r, carry):
        for k in range(TOP_K):
            _row_copy(yb_ref, 0, buf_ref.at[k], 0, sem).wait()
        return carry

    lax.fori_loop(0, tm, drain, 0)
    y = x_ref[...] + (gt[:, 0:1] * buf_ref[0] + gt[:, 1:2] * buf_ref[1])
    o_ref[...] = _rms(y, g_ref[...])


def combine_final(dest_tiles, x, gates, g, yb, tm):
    t, d = x.shape
    ne = gates.shape[0]
    return pl.pallas_call(
        functools.partial(_combine_kernel, tm=tm),
        grid=(t // tm,),
        in_specs=[pl.BlockSpec((TOP_K * tm,), lambda i: (i,), memory_space=pltpu.SMEM),
                  pl.BlockSpec((tm, d), lambda i: (i, 0)),
                  pl.BlockSpec((ne, tm), lambda i: (0, i)),
                  pl.BlockSpec((1, d), lambda i: (0, 0)),
                  pl.BlockSpec(memory_space=pl.ANY)],
        out_specs=pl.BlockSpec((tm, d), lambda i: (i, 0)),
        out_shape=jax.ShapeDtypeStruct((t, d), F32),
        scratch_shapes=[pltpu.VMEM((TOP_K, tm, d), F32), pltpu.SemaphoreType.DMA(())],
        compiler_params=_params("arbitrary"),
        name="combine_final",
    )(dest_tiles, x, gates, g.reshape(1, d), yb)


def _pick(n, prefs):
    for p in prefs:
        if n % p == 0:
            return p
    return n


def kernel(x, mem, g_mix, g_mem, w_in_moba, w_in_ssd, w_mem_kv, w_out, rel_bias, conv_w, conv_b, dt_bias, a_log, d_skip, g_ssd_out, g_ffn, w_ffn_gate, w_ffn_up, w_ffn_down, w_router, b_router, w_exp_gate, w_exp_up, w_exp_down, g_final):
    bsz, s, d = x.shape
    t = bsz * s
    m_len = mem.shape[1]
    tm = _pick(t, (1024, 512, 256))
    xf = x.reshape(t, d).astype(F32)
    memf = mem.reshape(bsz * m_len, d).astype(F32)
    tmm = _pick(bsz * m_len, (1024, 512, 256))

    proj0 = norm_matmul(xf, g_mix[0], w_in_moba[0].astype(BF16), BF16, tm, 1280)
    kv0 = norm_matmul(memf, g_mem[0], w_mem_kv[0].astype(BF16), BF16, tmm, 1024)
    proj0 = proj0.reshape(bsz, s, -1)
    y_tok = moba_attention(proj0, _moba_bias_tables(rel_bias), bsz, s)
    y_mem = memory_attention(proj0, 3 * MOBA_HEADS, kv0.reshape(bsz, m_len, -1), bsz, s)
    wo = w_out[0].astype(BF16)
    x1 = out_projection(xf, y_tok.reshape(t, -1), y_mem.reshape(t, -1), wo[:MOBA_WIDTH], wo[MOBA_WIDTH:], 512)
    x1 = dense_ffn(x1, g_ffn[0], w_ffn_gate[0].astype(BF16), w_ffn_up[0].astype(BF16),
                   w_ffn_down[0].astype(BF16), tm, 512)

    n_zx = SSD_INNER + SSD_INNER + 2 * SSD_GROUPS * SSD_STATE
    w1 = w_in_ssd[0]
    w_main = jnp.concatenate([w1[:, :n_zx], w1[:, n_zx + SSD_HEADS:]], axis=1).astype(BF16)
    w_dt = jnp.pad(w1[:, n_zx:n_zx + SSD_HEADS].reshape(d, SSD_GROUPS, SSD_GHEADS),
                   ((0, 0), (0, 0), (0, LANES - SSD_GHEADS))).reshape(d, SSD_GROUPS * LANES).astype(BF16)
    proj1 = norm_matmul(x1, g_mix[1], w_main, BF16, tm, 1536).reshape(bsz, s, -1)
    dt_all = norm_matmul(x1, g_mix[1], w_dt, F32, tm, SSD_GROUPS * LANES).reshape(bsz, s, -1)
    kv1 = norm_matmul(memf, g_mem[1], w_mem_kv[1].astype(BF16), BF16, tmm, 1024)
    y_tok = ssd_mixer(proj1, dt_all, *_ssd_group_params(conv_w[0], conv_b[0], dt_bias[0], a_log[0],
                                                         d_skip[0], g_ssd_out[0]), bsz, s)
    y_mem = memory_attention(proj1, n_zx // HD, kv1.reshape(bsz, m_len, -1), bsz, s)
    wo = w_out[1].astype(BF16)
    x2 = out_projection(x1, y_tok.reshape(t, -1), y_mem.reshape(t, -1), wo[:SSD_INNER], wo[SSD_INNER:], 512)

    tmr = _pick(t, (512, 256))
    h, route, gates, counts = moe_router(x2, g_ffn[1], w_router[0], b_router[0], tmr, F32)
    r = EXPERT_ROWS
    n_tiles = -(-t * TOP_K // r) + N_EXPERTS
    cnt = counts[:, 0].astype(jnp.int32)
    tiles_per = (cnt + r - 1) // r
    ends = jnp.cumsum(tiles_per)
    first_row = (ends - tiles_per) * r
    n_used = ends[-1].astype(jnp.int32)
    step = jnp.minimum(jnp.arange(n_tiles, dtype=jnp.int32), n_used - 1)
    tile_expert = jnp.sum(step[:, None] >= ends[None, :], axis=1).astype(jnp.int32)
    dest = jnp.sum(jnp.where(route[0:2, :, None] == jnp.arange(N_EXPERTS)[None, None, :],
                             first_row[None, None, :], 0), axis=-1) + route[2:4]
    dest_tiles = dest.reshape(TOP_K, t // tmr, tmr).transpose(1, 0, 2).reshape(-1)
    meta = jnp.concatenate([cnt, first_row, n_used.reshape(1)]).astype(jnp.int32)
    xb = moe_dispatch(dest_tiles, meta, h, tmr, n_tiles)
    yb = expert_ffn(tile_expert, n_used.reshape(1), xb, w_exp_gate[0], w_exp_up[0], w_exp_down[0], 512)
    out = combine_final(dest_tiles, x2, gates, g_final, yb, tmr)
    return out.reshape(bsz, s, d).astype(x.dtype)
```

```python
import functools
import math

import jax
import jax.numpy as jnp
from jax import lax
from jax.experimental import pallas as pl
from jax.experimental.pallas import tpu as pltpu

F32 = jnp.float32
BF16 = jnp.bfloat16

D_MODEL = 1024
HD = 128
MOBA_HEADS = 12
MOBA_WIDTH = MOBA_HEADS * HD
MOBA_BLOCK = 256
MOBA_TOPK = 3
REL_BUCKETS = 32
REL_MAX_DIST = 128
MEM_HEADS = 4
MEM_WIDTH = MEM_HEADS * HD
SSD_HEADS = 24
SSD_HD = 64
SSD_INNER = SSD_HEADS * SSD_HD
SSD_GROUPS = 4
SSD_STATE = 128
SSD_CONV = 4
SSD_CHUNK = 128
SSD_GHEADS = SSD_HEADS // SSD_GROUPS
SSD_GW = SSD_GHEADS * SSD_HD
SSD_CONV_GW = SSD_GW + 2 * SSD_STATE
D_FF = 3584
N_EXPERTS = 8
TOP_K = 2
EPS = 1e-6

LOG2E = math.log2(math.e)
LANES = 128
SUBLANES = 8
VMEM_LIMIT = 48 * 1024 * 1024
EXPERT_ROWS = 1024


def _dot(a, b):
    return jnp.dot(a, b, preferred_element_type=F32)


def _dot_nt(a, b):
    return lax.dot_general(a, b, (((1,), (1,)), ((), ())), preferred_element_type=F32)


def _split3(x):
    p0 = x.astype(BF16)
    r0 = x - p0.astype(F32)
    p1 = r0.astype(BF16)
    p2 = (r0 - p1.astype(F32)).astype(BF16)
    return p0, p1, p2


def _dot3l(x, w):
    return sum(_dot(p, w) for p in _split3(x))


def _dot3r(w, x):
    return sum(_dot(w, p) for p in _split3(x))


def _sigmoid(x):
    return 1.0 / (1.0 + jnp.exp(-x))


def _rms(x, g):
    ms = jnp.mean(x * x, axis=-1, keepdims=True)
    return x * lax.rsqrt(ms + EPS) * g


def _rms_rows_to(h_ref, x_ref, g_ref, rows):
    step = min(rows, 256)
    for r in range(0, rows, step):
        h_ref[r:r + step, :] = _rms(x_ref[r:r + step, :], g_ref[...]).astype(h_ref.dtype)


def _params(*sem):
    return pltpu.CompilerParams(dimension_semantics=sem, vmem_limit_bytes=VMEM_LIMIT)


def _normmm_kernel(x_ref, g_ref, w_ref, o_ref, h_ref, *, tm):
    @pl.when(pl.program_id(1) == 0)
    def _():
        _rms_rows_to(h_ref, x_ref, g_ref, tm)

    o_ref[...] = _dot(h_ref[...], w_ref[...]).astype(o_ref.dtype)


def norm_matmul(x, g, w, out_dtype, tm, tn):
    t, d = x.shape
    n = w.shape[1]
    return pl.pallas_call(
        functools.partial(_normmm_kernel, tm=tm),
        grid=(t // tm, n // tn),
        in_specs=[pl.BlockSpec((tm, d), lambda i, j: (i, 0)),
                  pl.BlockSpec((1, d), lambda i, j: (0, 0)),
                  pl.BlockSpec((d, tn), lambda i, j: (0, j))],
        out_specs=pl.BlockSpec((tm, tn), lambda i, j: (i, j)),
        out_shape=jax.ShapeDtypeStruct((t, n), out_dtype),
        scratch_shapes=[pltpu.VMEM((tm, d), BF16)],
        compiler_params=_params("parallel", "arbitrary"),
        name="norm_matmul",
    )(x, g.reshape(1, d), w)


def _moba_kernel(q_ref, k_ref, v_ref, tab_ref, o_ref, vt_ref, km_ref, *, nb, n_sel):
    blk = MOBA_BLOCK
    scale = HD ** -0.5 * LOG2E
    nbp = km_ref.shape[0]
    km_ref[...] = jnp.zeros_like(km_ref)
    for j in range(nb):
        rows = slice(j * blk, (j + 1) * blk)
        vt_ref[:, rows] = v_ref[rows, :].astype(F32).T.astype(BF16)
        km_ref[j:j + 1, :] = jnp.mean(k_ref[rows, :].astype(F32), axis=0, keepdims=True)
    km = km_ref[...]
    km_hi = km.astype(BF16)
    km_lo = (km - km_hi.astype(F32)).astype(BF16)
    q_all = q_ref[...]
    gate = _dot_nt(km_hi, q_all) + _dot_nt(km_lo, q_all)
    sub = lax.broadcasted_iota(jnp.int32, (nbp, blk), 0)
    bias_far = tab_ref[2, 0:1, :]

    for i in range(nb):
        rows = slice(i * blk, (i + 1) * blk)
        nk = (i + 1) * blk
        qi = q_ref[rows, :]
        s_all = _dot_nt(k_ref[0:nk, :], qi)
        if i > 0:
            valid = sub < i
            gm = jnp.where(valid, gate[:, rows], -jnp.inf)
            rank = jnp.zeros((nbp, blk), F32)
            for jp in range(i):
                row = gm[jp:jp + 1, :]
                beats = (row > gm) | ((row == gm) & (sub > jp))
                rank = rank + jnp.where(beats, 1.0, 0.0)
            selm = jnp.where(valid & (rank < n_sel), 0.0, -jnp.inf)
        bands = []
        for j in range(i + 1):
            sj = s_all[j * blk:(j + 1) * blk, :] * scale
            if j == i:
                sj = sj + tab_ref[0]
            elif j == i - 1:
                sj = sj + (tab_ref[1] + selm[j:j + 1, :])
            else:
                sj = sj + (bias_far + selm[j:j + 1, :])
            bands.append(sj)
        s = jnp.concatenate(bands, axis=0) if i > 0 else bands[0]
        m = jnp.max(s, axis=0, keepdims=True)
        p = jnp.exp2(s - m)
        l = jnp.sum(p, axis=0, keepdims=True)
        acc = _dot(vt_ref[:, 0:nk], p.astype(BF16))
        o_ref[rows, :] = (acc / l).T.astype(o_ref.dtype)


def _t5_bucket_idx(dist):
    n = jnp.maximum(dist, 0)
    max_exact = REL_BUCKETS // 2
    large = max_exact + (jnp.log(jnp.maximum(n, 1).astype(F32) / max_exact)
                         / math.log(REL_MAX_DIST / max_exact)
                         * (REL_BUCKETS - max_exact)).astype(jnp.int32)
    large = jnp.minimum(large, REL_BUCKETS - 1)
    return jnp.where(n < max_exact, n, large)


def _bias_table_kernel(rb_ref, idx_ref, o_ref):
    h = pl.program_id(0)
    for t in range(3):
        idx = idx_ref[t]
        acc = jnp.full(idx.shape, -jnp.inf, F32)
        for b in range(REL_BUCKETS):
            acc = jnp.where(idx == b, rb_ref[h, b], acc)
        o_ref[t] = acc * LOG2E


def _moba_bias_tables(rel_bias):
    blk = MOBA_BLOCK
    loc = jnp.arange(blk)
    d0 = loc[None, :] - loc[:, None]
    idx = jnp.stack([jnp.where(d0 >= 0, _t5_bucket_idx(d0), -1),
                     _t5_bucket_idx(d0 + blk),
                     _t5_bucket_idx(d0 + 2 * blk)]).astype(jnp.int32)
    return pl.pallas_call(
        _bias_table_kernel,
        grid=(MOBA_HEADS,),
        in_specs=[pl.BlockSpec(memory_space=pltpu.SMEM),
                  pl.BlockSpec((3, blk, blk), lambda h: (0, 0, 0))],
        out_specs=pl.BlockSpec((None, 3, blk, blk), lambda h: (h, 0, 0, 0)),
        out_shape=jax.ShapeDtypeStruct((MOBA_HEADS, 3, blk, blk), F32),
        compiler_params=_params("parallel"),
        name="moba_bias_tables",
    )(rel_bias.T.astype(F32), idx)


def moba_attention(proj, tabs, bsz, s):
    assert s % MOBA_BLOCK == 0
    nb = s // MOBA_BLOCK
    nbp = -(-nb // 8) * 8
    n_sel = min(MOBA_TOPK, nb - 1)
    blk = MOBA_BLOCK

    def col(off):
        return pl.BlockSpec((None, s, HD), lambda h, b: (b, 0, off + h))

    return pl.pallas_call(
        functools.partial(_moba_kernel, nb=nb, n_sel=n_sel),
        grid=(MOBA_HEADS, bsz),
        in_specs=[col(0), col(MOBA_HEADS), col(2 * MOBA_HEADS),
                  pl.BlockSpec((None, 3, blk, blk), lambda h, b: (h, 0, 0, 0))],
        out_specs=pl.BlockSpec((None, s, HD), lambda h, b: (b, 0, h)),
        out_shape=jax.ShapeDtypeStruct((bsz, s, MOBA_WIDTH), BF16),
        scratch_shapes=[pltpu.VMEM((HD, s), BF16),
                        pltpu.VMEM((nbp, HD), F32)],
        compiler_params=_params("parallel", "parallel"),
        name="moba_attention",
    )(proj, proj, proj, tabs)


def _memattn_kernel(q_ref, k_ref, v_ref, o_ref, *, n_chunks, qc):
    scale = HD ** -0.5 * LOG2E
    k = k_ref[...]
    vt = v_ref[...].astype(F32).T.astype(BF16)
    for c in range(n_chunks):
        rows = slice(c * qc, (c + 1) * qc)
        s = _dot_nt(k, q_ref[rows, :]) * scale
        m = jnp.max(s, axis=0, keepdims=True)
        p = jnp.exp2(s - m)
        l = jnp.sum(p, axis=0, keepdims=True)
        o = _dot(vt, p.astype(BF16)) / l
        o_ref[rows, :] = o.T.astype(o_ref.dtype)


def memory_attention(proj, q_off, kv, bsz, s):
    m_len = kv.shape[1]
    qc = 256
    return pl.pallas_call(
        functools.partial(_memattn_kernel, n_chunks=s // qc, qc=qc),
        grid=(bsz, MEM_HEADS),
        in_specs=[pl.BlockSpec((None, s, HD), lambda b, h: (b, 0, q_off + h)),
                  pl.BlockSpec((None, m_len, HD), lambda b, h: (b, 0, h)),
                  pl.BlockSpec((None, m_len, HD), lambda b, h: (b, 0, MEM_HEADS + h))],
        out_specs=pl.BlockSpec((None, s, HD), lambda b, h: (b, 0, h)),
        out_shape=jax.ShapeDtypeStruct((bsz, s, MEM_WIDTH), BF16),
        compiler_params=_params("parallel", "parallel"),
        name="memory_attention",
    )(proj, kv, kv)


def _outproj_kernel(x_ref, ya_ref, yb_ref, wa_ref, wb_ref, o_ref):
    o_ref[...] = x_ref[...] + _dot(ya_ref[...], wa_ref[...]) + _dot(yb_ref[...], wb_ref[...])


def out_projection(x, ya, yb, wa, wb, tm):
    t, d = x.shape
    ka, kb = ya.shape[1], yb.shape[1]
    return pl.pallas_call(
        _outproj_kernel,
        grid=(t // tm,),
        in_specs=[pl.BlockSpec((tm, d), lambda i: (i, 0)),
                  pl.BlockSpec((tm, ka), lambda i: (i, 0)),
                  pl.BlockSpec((tm, kb), lambda i: (i, 0)),
                  pl.BlockSpec((ka, d), lambda i: (0, 0)),
                  pl.BlockSpec((kb, d), lambda i: (0, 0))],
        out_specs=pl.BlockSpec((tm, d), lambda i: (i, 0)),
        out_shape=jax.ShapeDtypeStruct((t, d), F32),
        compiler_params=_params("parallel"),
        name="out_projection",
    )(x, ya, yb, wa, wb)


def _ffn_kernel(x_ref, g_ref, wg_ref, wu_ref, wd_ref, o_ref, h_ref, acc_ref, *, tm):
    f = pl.program_id(1)

    @pl.when(f == 0)
    def _():
        _rms_rows_to(h_ref, x_ref, g_ref, tm)
        acc_ref[...] = x_ref[...]

    h = h_ref[...]
    a = _dot(h, wg_ref[...])
    u = _dot(h, wu_ref[...])
    act = (a * _sigmoid(a) * u).astype(BF16)
    acc_ref[...] += _dot(act, wd_ref[...])

    @pl.when(f == pl.num_programs(1) - 1)
    def _():
        o_ref[...] = acc_ref[...]


def dense_ffn(x, g, wg, wu, wd, tm, tf):
    t, d = x.shape
    ff = wg.shape[1]
    return pl.pallas_call(
        functools.partial(_ffn_kernel, tm=tm),
        grid=(t // tm, ff // tf),
        in_specs=[pl.BlockSpec((tm, d), lambda i, f: (i, 0)),
                  pl.BlockSpec((1, d), lambda i, f: (0, 0)),
                  pl.BlockSpec((d, tf), lambda i, f: (0, f)),
                  pl.BlockSpec((d, tf), lambda i, f: (0, f)),
                  pl.BlockSpec((tf, d), lambda i, f: (f, 0))],
        out_specs=pl.BlockSpec((tm, d), lambda i, f: (i, 0)),
        out_shape=jax.ShapeDtypeStruct((t, d), F32),
        scratch_shapes=[pltpu.VMEM((tm, d), BF16), pltpu.VMEM((tm, d), F32)],
        compiler_params=_params("parallel", "arbitrary"),
        name="dense_ffn",
    )(x, g.reshape(1, d), wg, wu, wd)


def _ssd_kernel(z_ref, xs_ref, b_ref, c_ref, dt_ref, cw_ref, cb_ref, dtb_ref, alog_ref, dsk_ref,
                gout_ref, exp_ref, o_ref, h_ref, *, nc):
    L = SSD_CHUNK
    gw = SSD_GW
    n = SSD_STATE
    h_ref[...] = jnp.zeros_like(h_ref)
    a_neg = -jnp.exp(alog_ref[...])
    causal = (lax.broadcasted_iota(jnp.int32, (L, L), 0) >= lax.broadcasted_iota(jnp.int32, (L, L), 1))
    tri = jnp.where(causal, 1.0, 0.0).astype(BF16)
    lane_head = lax.broadcasted_iota(jnp.int32, (1, gw), 1) // SSD_HD
    expm = exp_ref[...]
    srcs = (xs_ref, b_ref, c_ref)
    sh_r = lax.broadcasted_iota(jnp.int32, (L, 2 * L), 0)
    sh_c = lax.broadcasted_iota(jnp.int32, (L, 2 * L), 1)
    shifts = [jnp.where(sh_c == sh_r + (L - s), 1.0, 0.0).astype(BF16) for s in range(1, SSD_CONV)]

    def chunk(c, carry):
        r0 = pl.multiple_of(c * L, L)
        rq = pl.multiple_of(jnp.maximum(r0 - L, 0), L)
        cur = jnp.concatenate([ref[pl.ds(r0, L), :] for ref in srcs], axis=1)
        prev = jnp.concatenate([ref[pl.ds(rq, L), :] for ref in srcs], axis=1)
        prev = jnp.where(c > 0, prev, jnp.zeros_like(prev))
        ext = jnp.concatenate([prev, cur], axis=0)
        conv = cb_ref[...] + cw_ref[SSD_CONV - 1:SSD_CONV, :] * cur.astype(F32)
        for s in range(1, SSD_CONV):
            k = SSD_CONV - 1 - s
            conv = conv + cw_ref[k:k + 1, :] * _dot(shifts[s - 1], ext)
        act = conv * _sigmoid(conv)
        xs = act[:, 0:gw]
        bm = act[:, gw:gw + n]
        cm = act[:, gw + n:gw + 2 * n]

        dtr = dt_ref[pl.ds(r0, L), :] + dtb_ref[...]
        dt = jnp.maximum(dtr, 0.0) + jnp.log(1.0 + jnp.exp(-jnp.abs(dtr)))
        la = dt * a_neg
        cs = _dot3r(tri, la)
        cs_t = cs.T
        dt_t = dt.T
        bm16 = bm.astype(BF16)
        cm16 = cm.astype(BF16)
        scores = _dot_nt(cm16, bm16)
        y = _dot_nt(cm16, h_ref[...].astype(BF16)) * _dot3l(jnp.exp(cs), expm)
        w_end = dt * jnp.exp(cs[L - 1:L, :] - cs)
        xs_t = xs.T
        for e in range(SSD_GHEADS):
            diff = cs[:, e:e + 1] - cs_t[e:e + 1, :]
            dec = jnp.exp(jnp.where(causal, diff, -jnp.inf))
            mm = (scores * dec * dt_t[e:e + 1, :]).astype(BF16)
            xe = jnp.where(lane_head == e, xs, 0.0).astype(BF16)
            y = y + _dot(mm, xe)
            bw = (bm * w_end[:, e:e + 1]).astype(BF16)
            st = _dot(xs_t[e * SSD_HD:(e + 1) * SSD_HD, :].astype(BF16), bw)
            cdec = jnp.exp(cs[L - 1:L, e:e + 1])
            hs = slice(e * SSD_HD, (e + 1) * SSD_HD)
            h_ref[hs, :] = h_ref[hs, :] * cdec + st
        y = y + xs * dsk_ref[...]
        z = z_ref[pl.ds(r0, L), :].astype(F32)
        u = y * (z * _sigmoid(z))
        ms = jnp.mean(u * u, axis=-1, keepdims=True)
        o_ref[pl.ds(r0, L), :] = (u * lax.rsqrt(ms + EPS) * gout_ref[...]).astype(o_ref.dtype)
        return carry

    lax.fori_loop(0, nc, chunk, 0, unroll=4)


def ssd_mixer(proj, dt_all, cw_g, cb_g, dtb_g, alog_g, dsk_g, gout_g, expand, bsz, s):
    assert s % SSD_CHUNK == 0
    gw = SSD_GW
    z_blk = 0
    xs_blk = SSD_INNER // gw
    b_blk = 2 * SSD_INNER // SSD_STATE
    c_blk = b_blk + SSD_GROUPS

    def seq(width, off):
        return pl.BlockSpec((None, s, width), lambda b, g: (b, 0, off + g))

    def par(rows, width):
        return pl.BlockSpec((None, rows, width), lambda b, g: (g, 0, 0))

    return pl.pallas_call(
        functools.partial(_ssd_kernel, nc=s // SSD_CHUNK),
        grid=(bsz, SSD_GROUPS),
        in_specs=[seq(gw, z_blk), seq(gw, xs_blk), seq(SSD_STATE, b_blk), seq(SSD_STATE, c_blk),
                  seq(LANES, 0),
                  par(SSD_CONV, SSD_CONV_GW), par(1, SSD_CONV_GW), par(1, LANES), par(1, LANES),
                  par(1, gw), par(1, gw),
                  pl.BlockSpec((LANES, gw), lambda b, g: (0, 0))],
        out_specs=seq(gw, 0),
        out_shape=jax.ShapeDtypeStruct((bsz, s, SSD_INNER), BF16),
        scratch_shapes=[pltpu.VMEM((gw, SSD_STATE), F32)],
        compiler_params=_params("parallel", "parallel"),
        name="ssd_mixer",
    )(proj, proj, proj, proj, dt_all, cw_g, cb_g, dtb_g, alog_g, dsk_g, gout_g, expand)


def _ssd_group_params(conv_w, conv_b, dt_bias, a_log, d_skip, g_out):
    g, gh, gw, n = SSD_GROUPS, SSD_GHEADS, SSD_GW, SSD_STATE

    def conv_cols(a):
        xs = a[..., :SSD_INNER].reshape(a.shape[:-1] + (g, gw))
        bb = a[..., SSD_INNER:SSD_INNER + g * n].reshape(a.shape[:-1] + (g, n))
        cc = a[..., SSD_INNER + g * n:].reshape(a.shape[:-1] + (g, n))
        return jnp.moveaxis(jnp.concatenate([xs, bb, cc], axis=-1), -2, 0)

    cw_g = conv_cols(conv_w.astype(F32))
    cb_g = conv_cols(conv_b.astype(F32)[None, :])

    def per_head(a):
        return jnp.pad(a.astype(F32).reshape(g, 1, gh), ((0, 0), (0, 0), (0, LANES - gh)))

    dsk_g = jnp.repeat(d_skip.astype(F32), SSD_HD).reshape(g, 1, gw)
    gout_g = g_out.astype(F32).reshape(g, 1, gw)
    expand = (jnp.arange(LANES)[:, None] == (jnp.arange(gw)[None, :] // SSD_HD)).astype(BF16)
    return cw_g, cb_g, per_head(dt_bias), per_head(a_log), dsk_g, gout_g, expand


def _router_kernel(x_ref, g_ref, wrt_ref, br_ref, h_ref, route_ref, runs_ref, cnt_ref, carry_ref, *, tm):
    @pl.when(pl.program_id(0) == 0)
    def _():
        carry_ref[...] = jnp.zeros_like(carry_ref)

    ne = N_EXPERTS
    hn = _rms(x_ref[...], g_ref[...])
    h_hi = hn.astype(BF16)
    h_ref[...] = h_hi
    h_lo = (hn - h_hi.astype(F32)).astype(BF16)
    wr = wrt_ref[...]
    w_hi = wr.astype(BF16)
    w_lo = (wr - w_hi.astype(F32)).astype(BF16)
    logits = _dot_nt(w_hi, h_hi) + _dot_nt(w_hi, h_lo) + _dot_nt(w_lo, h_hi) + br_ref[:, 0:1]
    sub = lax.broadcasted_iota(jnp.int32, (ne, tm), 0)
    l1 = jnp.max(logits, axis=0, keepdims=True)
    i1 = jnp.min(jnp.where(logits == l1, sub, ne), axis=0, keepdims=True)
    rest = jnp.where(sub == i1, -jnp.inf, logits)
    l2 = jnp.max(rest, axis=0, keepdims=True)
    i2 = jnp.min(jnp.where(rest == l2, sub, ne), axis=0, keepdims=True)
    e2 = jnp.exp(l2 - l1)
    g1 = 1.0 / (1.0 + e2)
    g2 = e2 / (1.0 + e2)
    sel = jnp.where((sub == i1) | (sub == i2), 1.0, 0.0)
    before = (lax.broadcasted_iota(jnp.int32, (tm, tm), 0) < lax.broadcasted_iota(jnp.int32, (tm, tm), 1))
    prefix = _dot(sel.astype(BF16), jnp.where(before, 1.0, 0.0).astype(BF16))
    lane = lax.broadcasted_iota(jnp.int32, (ne, LANES), 1)
    subl = lax.broadcasted_iota(jnp.int32, (ne, LANES), 0)
    n_e = (jnp.sum(sel, axis=1, keepdims=True) + jnp.zeros((ne, LANES), F32)).astype(jnp.int32)
    n_pad = (n_e + (SUBLANES - 1)) & (-SUBLANES)
    off = jnp.zeros((ne, LANES), jnp.int32)
    for j in range(ne - 1):
        off = off + jnp.where(subl > j, n_pad[j:j + 1, :], 0)
    slot = prefix + off[:, 0:1].astype(F32)
    p1 = jnp.sum(jnp.where(sub == i1, slot, 0.0), axis=0, keepdims=True)
    p2 = jnp.sum(jnp.where(sub == i2, slot, 0.0), axis=0, keepdims=True)
    route_ref[...] = jnp.where(sub == 0, p1, jnp.where(sub == 1, p2, jnp.where(
        sub == 2, g1, jnp.where(sub == 3, g2, 0.0))))
    carry = carry_ref[...]
    runs_ref[...] = jnp.where(lane == 0, off, jnp.where(lane == 1, carry, jnp.where(lane == 2, n_pad, 0)))
    carry_ref[...] = carry + n_pad
    cnt_ref[...] = carry + n_pad


def moe_router(x, g, w_router, b_router, tm):
    t, d = x.shape
    ne = N_EXPERTS
    wrt = w_router.astype(F32).T
    br = jnp.broadcast_to(b_router.astype(F32)[:, None], (ne, LANES))
    return pl.pallas_call(
        functools.partial(_router_kernel, tm=tm),
        grid=(t // tm,),
        in_specs=[pl.BlockSpec((tm, d), lambda i: (i, 0)),
                  pl.BlockSpec((1, d), lambda i: (0, 0)),
                  pl.BlockSpec((ne, d), lambda i: (0, 0)),
                  pl.BlockSpec((ne, LANES), lambda i: (0, 0))],
        out_specs=[pl.BlockSpec((tm, d), lambda i: (i, 0)),
                   pl.BlockSpec((ne, tm), lambda i: (0, i)),
                   pl.BlockSpec((ne, LANES), lambda i: (i, 0)),
                   pl.BlockSpec((ne, LANES), lambda i: (0, 0))],
        out_shape=[jax.ShapeDtypeStruct((t, d), BF16),
                   jax.ShapeDtypeStruct((ne, t), F32),
                   jax.ShapeDtypeStruct((t // tm * ne, LANES), jnp.int32),
                   jax.ShapeDtypeStruct((ne, LANES), jnp.int32)],
        scratch_shapes=[pltpu.VMEM((ne, LANES), jnp.int32)],
        compiler_params=_params("arbitrary"),
        name="moe_router",
    )(x, g.reshape(1, d), wrt, br)


def _pieces(length, lo_bit, hi_bit):
    for b in range(lo_bit, hi_bit):
        n = 1 << b
        yield n, length & (n - 1), ((length >> b) & 1) == 1


def _run_copies(tab_ref, tile, tile_ref, buf_ref, sem, tm, to_buf, wait):
    for e in range(N_EXPERTS):
        base = (tile * N_EXPERTS + e) * 3
        off, dst, rows = tab_ref[base], tab_ref[base + 1], tab_ref[base + 2]
        for n, lo, present in _pieces(rows, SUBLANES.bit_length() - 1, tm.bit_length()):
            in_tile = tile_ref.at[pl.ds(pl.multiple_of(off + lo, SUBLANES), n), :]
            in_buf = buf_ref.at[pl.ds(pl.multiple_of(dst + lo, SUBLANES), n), :]
            cp = pltpu.make_async_copy(in_tile, in_buf, sem) if to_buf else pltpu.make_async_copy(in_buf, in_tile, sem)

            @pl.when(present)
            def _():
                if wait:
                    cp.wait()
                else:
                    cp.start()


def _dispatch_kernel(tab_ref, meta_ref, route_ref, h_ref, xb_ref, srt_ref, zero_ref, sem, zsem, *, tm, n_tiles):
    i = pl.program_id(0)
    nt = pl.num_programs(0)
    slot = i % 2
    rs = srt_ref.shape[1]

    def runs(tile, slot, wait):
        _run_copies(tab_ref, tile, srt_ref.at[slot], xb_ref, sem.at[slot], tm, True, wait)

    @pl.when(i >= 2)
    def _():
        runs(i - 2, slot, True)

    pos = route_ref[0:2, :].astype(jnp.int32)
    row = lax.broadcasted_iota(jnp.int32, (rs, tm), 0)
    onehot = jnp.where((row == pos[0:1, :]) | (row == pos[1:2, :]), 1.0, 0.0).astype(BF16)
    srt_ref[slot] = _dot(onehot, h_ref[...])
    runs(i, slot, False)

    @pl.when(i == nt - 1)
    def _():
        zero_ref[...] = jnp.zeros_like(zero_ref)
        zrows = zero_ref.shape[0]
        per = EXPERT_ROWS // zrows
        first_unused = meta_ref[2 * N_EXPERTS] * per
        for wait in (False, True):
            def go(cp, cond):
                @pl.when(cond)
                def _():
                    if wait:
                        cp.wait()
                    else:
                        cp.start()

            for e in range(N_EXPERTS):
                c = meta_ref[e]
                first = meta_ref[N_EXPERTS + e]
                tail = (-c) & (EXPERT_ROWS - 1)
                for n, lo, present in _pieces(tail, SUBLANES.bit_length() - 1, EXPERT_ROWS.bit_length() - 1):
                    row0 = pl.multiple_of(first + c + lo, SUBLANES)
                    go(pltpu.make_async_copy(zero_ref.at[pl.ds(0, n), :], xb_ref.at[pl.ds(row0, n), :], zsem), present)
            for j in range(N_EXPERTS * per):
                row0 = pl.multiple_of(jnp.minimum(first_unused + j, n_tiles * per - 1) * zrows, zrows)
                go(pltpu.make_async_copy(zero_ref, xb_ref.at[pl.ds(row0, zrows), :], zsem),
                   first_unused + j < n_tiles * per)

        @pl.when(i >= 1)
        def _():
            runs(i - 1, 1 - slot, True)

        runs(i, slot, True)


def moe_dispatch(tab, meta, route, h, tm, n_tiles):
    t, d = h.shape
    ne = route.shape[0]
    rs = TOP_K * tm + LANES
    return pl.pallas_call(
        functools.partial(_dispatch_kernel, tm=tm, n_tiles=n_tiles),
        grid=(t // tm,),
        in_specs=[pl.BlockSpec(memory_space=pltpu.SMEM),
                  pl.BlockSpec(memory_space=pltpu.SMEM),
                  pl.BlockSpec((ne, tm), lambda i: (0, i)),
                  pl.BlockSpec((tm, d), lambda i: (i, 0))],
        out_specs=pl.BlockSpec(memory_space=pl.ANY),
        out_shape=jax.ShapeDtypeStruct((n_tiles * EXPERT_ROWS, d), F32),
        scratch_shapes=[pltpu.VMEM((2, rs, d), F32),
                        pltpu.VMEM((EXPERT_ROWS // 2, d), F32),
                        pltpu.SemaphoreType.DMA((2,)), pltpu.SemaphoreType.DMA(())],
        compiler_params=_params("arbitrary"),
        name="moe_dispatch",
    )(tab, meta, route, h)


def _expert_kernel(te_ref, nu_ref, xb_ref, wg_ref, wu_ref, wd_ref, o_ref, acc_ref):
    c = pl.program_id(0)
    f = pl.program_id(1)
    last = pl.num_programs(1) - 1
    used = c < nu_ref[0]

    @pl.when(used)
    def _():
        @pl.when(f == 0)
        def _():
            acc_ref[...] = jnp.zeros_like(acc_ref)

        h = xb_ref[...].astype(BF16)
        a = _dot(h, wg_ref[...].astype(BF16))
        u = _dot(h, wu_ref[...].astype(BF16))
        act = (a * _sigmoid(a) * u).astype(BF16)
        acc_ref[...] += _dot(act, wd_ref[...].astype(BF16))

        @pl.when(f == last)
        def _():
            o_ref[...] = acc_ref[...]

    @pl.when(jnp.logical_and(jnp.logical_not(used), f == last))
    def _():
        o_ref[...] = jnp.zeros_like(o_ref)


def expert_ffn(tile_expert, n_used, xb, wg, wu, wd, tf):
    n_rows, d = xb.shape
    ff = wg.shape[2]
    r = EXPERT_ROWS
    nf = ff // tf

    def wcol(c, f, te, nu):
        return (te[c], 0, jnp.where(c < nu[0], f, nf - 1))

    def wrow(c, f, te, nu):
        return (te[c], jnp.where(c < nu[0], f, nf - 1), 0)

    grid_spec = pltpu.PrefetchScalarGridSpec(
        num_scalar_prefetch=2,
        grid=(n_rows // r, nf),
        in_specs=[pl.BlockSpec((r, d), lambda c, f, te, nu: (c, 0)),
                  pl.BlockSpec((None, d, tf), wcol),
                  pl.BlockSpec((None, d, tf), wcol),
                  pl.BlockSpec((None, tf, d), wrow)],
        out_specs=pl.BlockSpec((r, d), lambda c, f, te, nu: (c, 0)),
        scratch_shapes=[pltpu.VMEM((r, d), F32)],
    )
    return pl.pallas_call(
        _expert_kernel,
        grid_spec=grid_spec,
        out_shape=jax.ShapeDtypeStruct((n_rows, d), F32),
        compiler_params=_params("parallel", "arbitrary"),
        name="expert_ffn",
    )(tile_expert, n_used, xb, wg, wu, wd)


def _combine_kernel(tab_ref, route_ref, x_ref, g_ref, yb_ref, o_ref, srt_ref, sem, *, tm):
    i = pl.program_id(0)
    nt = pl.num_programs(0)
    slot = i % 2
    rs = srt_ref.shape[1]

    def runs(tile, slot, wait):
        _run_copies(tab_ref, tile, srt_ref.at[slot], yb_ref, sem.at[slot], tm, False, wait)

    @pl.when(i == 0)
    def _():
        srt_ref[...] = jnp.zeros_like(srt_ref)
        runs(0, 0, False)

    @pl.when(i + 1 < nt)
    def _():
        runs(i + 1, 1 - slot, False)

    runs(i, slot, True)
    aux = route_ref[...]
    aux_t = jnp.concatenate([aux, jnp.zeros((LANES - aux.shape[0], tm), F32)], axis=0).T
    srt = srt_ref[slot].astype(BF16)
    col = lax.broadcasted_iota(jnp.int32, (tm, rs), 1)
    ys = []
    for k in range(TOP_K):
        onehot = jnp.where(col == aux_t[:, k:k + 1].astype(jnp.int32), 1.0, 0.0).astype(BF16)
        ys.append(aux_t[:, TOP_K + k:TOP_K + k + 1] * _dot(onehot, srt))
    o_ref[...] = _rms(x_ref[...] + (ys[0] + ys[1]), g_ref[...])


def combine_final(tab, route, x, g, yb, tm):
    t, d = x.shape
    ne = route.shape[0]
    rs = TOP_K * tm + LANES
    return pl.pallas_call(
        functools.partial(_combine_kernel, tm=tm),
        grid=(t // tm,),
        in_specs=[pl.BlockSpec(memory_space=pltpu.SMEM),
                  pl.BlockSpec((ne, tm), lambda i: (0, i)),
                  pl.BlockSpec((tm, d), lambda i: (i, 0)),
                  pl.BlockSpec((1, d), lambda i: (0, 0)),
                  pl.BlockSpec(memory_space=pl.ANY)],
        out_specs=pl.BlockSpec((tm, d), lambda i: (i, 0)),
        out_shape=jax.ShapeDtypeStruct((t, d), F32),
        scratch_shapes=[pltpu.VMEM((2, rs, d), F32), pltpu.SemaphoreType.DMA((2,))],
        compiler_params=_params("arbitrary"),
        name="combine_final",
    )(tab, route, x, g.reshape(1, d), yb)


def _pick(n, prefs):
    for p in prefs:
        if n % p == 0:
            return p
    return n


def kernel(x, mem, g_mix, g_mem, w_in_moba, w_in_ssd, w_mem_kv, w_out, rel_bias, conv_w, conv_b, dt_bias, a_log, d_skip, g_ssd_out, g_ffn, w_ffn_gate, w_ffn_up, w_ffn_down, w_router, b_router, w_exp_gate, w_exp_up, w_exp_down, g_final):
    bsz, s, d = x.shape
    t = bsz * s
    m_len = mem.shape[1]
    tm = _pick(t, (1024, 512, 256))
    xf = x.reshape(t, d).astype(F32)
    memf = mem.reshape(bsz * m_len, d).astype(F32)
    tmm = _pick(bsz * m_len, (1024, 512, 256))

    proj0 = norm_matmul(xf, g_mix[0], w_in_moba[0].astype(BF16), BF16, tm, 1280)
    kv0 = norm_matmul(memf, g_mem[0], w_mem_kv[0].astype(BF16), BF16, tmm, 1024)
    proj0 = proj0.reshape(bsz, s, -1)
    y_tok = moba_attention(proj0, _moba_bias_tables(rel_bias), bsz, s)
    y_mem = memory_attention(proj0, 3 * MOBA_HEADS, kv0.reshape(bsz, m_len, -1), bsz, s)
    wo = w_out[0].astype(BF16)
    x1 = out_projection(xf, y_tok.reshape(t, -1), y_mem.reshape(t, -1), wo[:MOBA_WIDTH], wo[MOBA_WIDTH:], 512)
    x1 = dense_ffn(x1, g_ffn[0], w_ffn_gate[0].astype(BF16), w_ffn_up[0].astype(BF16),
                   w_ffn_down[0].astype(BF16), tm, 512)

    n_zx = SSD_INNER + SSD_INNER + 2 * SSD_GROUPS * SSD_STATE
    w1 = w_in_ssd[0]
    w_main = jnp.concatenate([w1[:, :n_zx], w1[:, n_zx + SSD_HEADS:]], axis=1).astype(BF16)
    w_dt = jnp.pad(w1[:, n_zx:n_zx + SSD_HEADS].reshape(d, SSD_GROUPS, SSD_GHEADS),
                   ((0, 0), (0, 0), (0, LANES - SSD_GHEADS))).reshape(d, SSD_GROUPS * LANES).astype(BF16)
    proj1 = norm_matmul(x1, g_mix[1], w_main, BF16, tm, 1536).reshape(bsz, s, -1)
    dt_all = norm_matmul(x1, g_mix[1], w_dt, F32, tm, SSD_GROUPS * LANES).reshape(bsz, s, -1)
    kv1 = norm_matmul(memf, g_mem[1], w_mem_kv[1].astype(BF16), BF16, tmm, 1024)
    y_tok = ssd_mixer(proj1, dt_all, *_ssd_group_params(conv_w[0], conv_b[0], dt_bias[0], a_log[0],
                                                         d_skip[0], g_ssd_out[0]), bsz, s)
    y_mem = memory_attention(proj1, n_zx // HD, kv1.reshape(bsz, m_len, -1), bsz, s)
    wo = w_out[1].astype(BF16)
    x2 = out_projection(x1, y_tok.reshape(t, -1), y_mem.reshape(t, -1), wo[:SSD_INNER], wo[SSD_INNER:], 512)

    tmr = _pick(t, (512, 256))
    nt = t // tmr
    h, route, runs, counts = moe_router(x2, g_ffn[1], w_router[0], b_router[0], tmr)
    r = EXPERT_ROWS
    n_tiles = -(-(t * TOP_K + nt * N_EXPERTS * (SUBLANES - 1)) // r) + N_EXPERTS
    cnt = counts[:, 0]
    tiles_per = (cnt + r - 1) // r
    ends = jnp.cumsum(tiles_per)
    first_row = (ends - tiles_per) * r
    n_used = ends[-1].astype(jnp.int32)
    step = jnp.minimum(jnp.arange(n_tiles, dtype=jnp.int32), n_used - 1)
    tile_expert = jnp.sum(step[:, None] >= ends[None, :], axis=1).astype(jnp.int32)
    tab = runs.reshape(nt, N_EXPERTS, LANES)[:, :, :3]
    tab = tab.at[:, :, 1].add(first_row[None, :]).reshape(-1).astype(jnp.int32)
    meta = jnp.concatenate([cnt, first_row, n_used.reshape(1)]).astype(jnp.int32)
    xb = moe_dispatch(tab, meta, route, h, tmr, n_tiles)
    yb = expert_ffn(tile_expert, n_used.reshape(1), xb, w_exp_gate[0], w_exp_up[0], w_exp_down[0], 512)
    out = combine_final(tab, route, x2, g_final, yb, tmr)
    return out.reshape(bsz, s, d).astype(x.dtype)
```

```python
import functools
import math

import jax
import jax.numpy as jnp
from jax import lax
from jax.experimental import pallas as pl
from jax.experimental.pallas import tpu as pltpu

F32 = jnp.float32
BF16 = jnp.bfloat16

D_MODEL = 1024
HD = 128
MOBA_HEADS = 12
MOBA_WIDTH = MOBA_HEADS * HD
MOBA_BLOCK = 256
MOBA_TOPK = 3
REL_BUCKETS = 32
REL_MAX_DIST = 128
MEM_HEADS = 4
MEM_WIDTH = MEM_HEADS * HD
SSD_HEADS = 24
SSD_HD = 64
SSD_INNER = SSD_HEADS * SSD_HD
SSD_GROUPS = 4
SSD_STATE = 128
SSD_CONV = 4
SSD_CHUNK = 128
SSD_GHEADS = SSD_HEADS // SSD_GROUPS
SSD_GW = SSD_GHEADS * SSD_HD
SSD_CONV_GW = SSD_GW + 2 * SSD_STATE
D_FF = 3584
N_EXPERTS = 8
TOP_K = 2
EPS = 1e-6

LOG2E = math.log2(math.e)
LANES = 128
SUBLANES = 8
VMEM_LIMIT = 48 * 1024 * 1024
EXPERT_ROWS = 1024


def _dot(a, b):
    return jnp.dot(a, b, preferred_element_type=F32)


def _dot_nt(a, b):
    return lax.dot_general(a, b, (((1,), (1,)), ((), ())), preferred_element_type=F32)


def _split3(x):
    p0 = x.astype(BF16)
    r0 = x - p0.astype(F32)
    p1 = r0.astype(BF16)
    p2 = (r0 - p1.astype(F32)).astype(BF16)
    return p0, p1, p2


def _dot3l(x, w):
    return sum(_dot(p, w) for p in _split3(x))


def _dot3r(w, x):
    return sum(_dot(w, p) for p in _split3(x))


def _sigmoid(x):
    return 1.0 / (1.0 + jnp.exp(-x))


def _rms(x, g):
    ms = jnp.mean(x * x, axis=-1, keepdims=True)
    return x * lax.rsqrt(ms + EPS) * g


def _rms_rows_to(h_ref, x_ref, g_ref, rows):
    step = min(rows, 256)
    for r in range(0, rows, step):
        h_ref[r:r + step, :] = _rms(x_ref[r:r + step, :], g_ref[...]).astype(h_ref.dtype)


def _params(*sem):
    return pltpu.CompilerParams(dimension_semantics=sem, vmem_limit_bytes=VMEM_LIMIT)


def _normmm_kernel(x_ref, g_ref, w_ref, o_ref, *, rc, tn):
    tm, n = o_ref.shape

    def norm(r):
        return _rms(x_ref[r * rc:(r + 1) * rc, :], g_ref[...]).astype(BF16)

    def matmuls(r, h):
        for j in range(n // tn):
            cols = slice(j * tn, (j + 1) * tn)
            o_ref[r * rc:(r + 1) * rc, cols] = _dot(h, w_ref[:, cols]).astype(o_ref.dtype)

    live = {}
    for step in range(tm // rc + 1):
        if step < tm // rc:
            live[step] = norm(step)
        if step >= 1:
            matmuls(step - 1, live.pop(step - 1))


def norm_matmul(x, g, w, out_dtype, tm, tn):
    t, d = x.shape
    n = w.shape[1]
    return pl.pallas_call(
        functools.partial(_normmm_kernel, rc=min(tm, 256), tn=tn),
        grid=(t // tm,),
        in_specs=[pl.BlockSpec((tm, d), lambda i: (i, 0)),
                  pl.BlockSpec((1, d), lambda i: (0, 0)),
                  pl.BlockSpec((d, n), lambda i: (0, 0))],
        out_specs=pl.BlockSpec((tm, n), lambda i: (i, 0)),
        out_shape=jax.ShapeDtypeStruct((t, n), out_dtype),
        compiler_params=_params("parallel"),
        name="norm_matmul",
    )(x, g.reshape(1, d), w)


def _moba_kernel(q_ref, k_ref, v_ref, tab_ref, o_ref, vt_ref, km_ref, *, nb, n_sel):
    blk = MOBA_BLOCK
    scale = HD ** -0.5 * LOG2E
    nbp = km_ref.shape[0]
    km_ref[...] = jnp.zeros_like(km_ref)
    for j in range(nb):
        rows = slice(j * blk, (j + 1) * blk)
        vt_ref[:, rows] = v_ref[rows, :].astype(F32).T.astype(BF16)
        km_ref[j:j + 1, :] = jnp.mean(k_ref[rows, :].astype(F32), axis=0, keepdims=True)
    km = km_ref[...]
    km_hi = km.astype(BF16)
    km_lo = (km - km_hi.astype(F32)).astype(BF16)
    q_all = q_ref[...]
    gate = _dot_nt(km_hi, q_all) + _dot_nt(km_lo, q_all)
    sub = lax.broadcasted_iota(jnp.int32, (nbp, blk), 0)
    bias_far = tab_ref[2, 0:1, :]

    def scores(i):
        return _dot_nt(k_ref[0:(i + 1) * blk, :], q_ref[i * blk:(i + 1) * blk, :])

    def softmax(i, s_all):
        rows = slice(i * blk, (i + 1) * blk)
        if i > 0:
            valid = sub < i
            gm = jnp.where(valid, gate[:, rows], -jnp.inf)
            rank = jnp.zeros((nbp, blk), F32)
            for jp in range(i):
                row = gm[jp:jp + 1, :]
                beats = (row > gm) | ((row == gm) & (sub > jp))
                rank = rank + jnp.where(beats, 1.0, 0.0)
            selm = jnp.where(valid & (rank < n_sel), 0.0, -jnp.inf)
        bands = []
        for j in range(i + 1):
            sj = s_all[j * blk:(j + 1) * blk, :] * scale
            if j == i:
                sj = sj + tab_ref[0]
            elif j == i - 1:
                sj = sj + (tab_ref[1] + selm[j:j + 1, :])
            else:
                sj = sj + (bias_far + selm[j:j + 1, :])
            bands.append(sj)
        s = jnp.concatenate(bands, axis=0) if i > 0 else bands[0]
        m = jnp.max(s, axis=0, keepdims=True)
        p = jnp.exp2(s - m)
        return p.astype(BF16), jnp.sum(p, axis=0, keepdims=True)

    def output(i, p, l):
        acc = _dot(vt_ref[:, 0:(i + 1) * blk], p)
        o_ref[i * blk:(i + 1) * blk, :] = (acc / l).T.astype(o_ref.dtype)

    s_live, p_live = {}, {}
    units = [sorted({u, nb - 1 - u}) for u in range((nb + 1) // 2)]
    for step in range(len(units) + 2):
        if step < len(units):
            for i in units[step]:
                s_live[i] = scores(i)
        if 1 <= step <= len(units):
            for i in units[step - 1]:
                p_live[i] = softmax(i, s_live.pop(i))
        if step >= 2:
            for i in units[step - 2]:
                output(i, *p_live.pop(i))


def _t5_bucket_idx(dist):
    n = jnp.maximum(dist, 0)
    max_exact = REL_BUCKETS // 2
    large = max_exact + (jnp.log(jnp.maximum(n, 1).astype(F32) / max_exact)
                         / math.log(REL_MAX_DIST / max_exact)
                         * (REL_BUCKETS - max_exact)).astype(jnp.int32)
    large = jnp.minimum(large, REL_BUCKETS - 1)
    return jnp.where(n < max_exact, n, large)


def _bias_table_kernel(rb_ref, idx_ref, o_ref):
    h = pl.program_id(0)
    for t in range(3):
        idx = idx_ref[t]
        acc = jnp.full(idx.shape, -jnp.inf, F32)
        for b in range(REL_BUCKETS):
            acc = jnp.where(idx == b, rb_ref[h, b], acc)
        o_ref[t] = acc * LOG2E


def _moba_bias_tables(rel_bias):
    blk = MOBA_BLOCK
    loc = jnp.arange(blk)
    d0 = loc[None, :] - loc[:, None]
    idx = jnp.stack([jnp.where(d0 >= 0, _t5_bucket_idx(d0), -1),
                     _t5_bucket_idx(d0 + blk),
                     _t5_bucket_idx(d0 + 2 * blk)]).astype(jnp.int32)
    return pl.pallas_call(
        _bias_table_kernel,
        grid=(MOBA_HEADS,),
        in_specs=[pl.BlockSpec(memory_space=pltpu.SMEM),
                  pl.BlockSpec((3, blk, blk), lambda h: (0, 0, 0))],
        out_specs=pl.BlockSpec((None, 3, blk, blk), lambda h: (h, 0, 0, 0)),
        out_shape=jax.ShapeDtypeStruct((MOBA_HEADS, 3, blk, blk), F32),
        compiler_params=_params("parallel"),
        name="moba_bias_tables",
    )(rel_bias.T.astype(F32), idx)


def moba_attention(proj, tabs, bsz, s):
    assert s % MOBA_BLOCK == 0
    nb = s // MOBA_BLOCK
    nbp = -(-nb // 8) * 8
    n_sel = min(MOBA_TOPK, nb - 1)
    blk = MOBA_BLOCK

    def col(off):
        return pl.BlockSpec((None, s, HD), lambda h, b: (b, 0, off + h))

    return pl.pallas_call(
        functools.partial(_moba_kernel, nb=nb, n_sel=n_sel),
        grid=(MOBA_HEADS, bsz),
        in_specs=[col(0), col(MOBA_HEADS), col(2 * MOBA_HEADS),
                  pl.BlockSpec((None, 3, blk, blk), lambda h, b: (h, 0, 0, 0))],
        out_specs=pl.BlockSpec((None, s, HD), lambda h, b: (b, 0, h)),
        out_shape=jax.ShapeDtypeStruct((bsz, s, MOBA_WIDTH), BF16),
        scratch_shapes=[pltpu.VMEM((HD, s), BF16),
                        pltpu.VMEM((nbp, HD), F32)],
        compiler_params=_params("parallel", "parallel"),
        name="moba_attention",
    )(proj, proj, proj, tabs)


def _memattn_kernel(q_ref, k_ref, v_ref, o_ref, *, n_chunks, qc):
    scale = HD ** -0.5 * LOG2E
    k = k_ref[...]
    vt = v_ref[...].astype(F32).T.astype(BF16)
    def scores(c):
        return _dot_nt(k, q_ref[c * qc:(c + 1) * qc, :])

    def softmax(s):
        s = s * scale
        p = jnp.exp2(s - jnp.max(s, axis=0, keepdims=True))
        return p.astype(BF16), jnp.sum(p, axis=0, keepdims=True)

    def output(c, p, l):
        o_ref[c * qc:(c + 1) * qc, :] = (_dot(vt, p) / l).T.astype(o_ref.dtype)

    s_live, p_live = {}, {}
    for step in range(n_chunks + 2):
        if step < n_chunks:
            s_live[step] = scores(step)
        if 1 <= step <= n_chunks:
            p_live[step - 1] = softmax(s_live.pop(step - 1))
        if step >= 2:
            output(step - 2, *p_live.pop(step - 2))


def memory_attention(proj, q_off, kv, bsz, s):
    m_len = kv.shape[1]
    qc = 256
    return pl.pallas_call(
        functools.partial(_memattn_kernel, n_chunks=s // qc, qc=qc),
        grid=(bsz, MEM_HEADS),
        in_specs=[pl.BlockSpec((None, s, HD), lambda b, h: (b, 0, q_off + h)),
                  pl.BlockSpec((None, m_len, HD), lambda b, h: (b, 0, h)),
                  pl.BlockSpec((None, m_len, HD), lambda b, h: (b, 0, MEM_HEADS + h))],
        out_specs=pl.BlockSpec((None, s, HD), lambda b, h: (b, 0, h)),
        out_shape=jax.ShapeDtypeStruct((bsz, s, MEM_WIDTH), BF16),
        compiler_params=_params("parallel", "parallel"),
        name="memory_attention",
    )(proj, kv, kv)


def _outproj_kernel(x_ref, ya_ref, yb_ref, wa_ref, wb_ref, o_ref):
    o_ref[...] = x_ref[...] + _dot(ya_ref[...], wa_ref[...]) + _dot(yb_ref[...], wb_ref[...])


def out_projection(x, ya, yb, wa, wb, tm):
    t, d = x.shape
    ka, kb = ya.shape[1], yb.shape[1]
    return pl.pallas_call(
        _outproj_kernel,
        grid=(t // tm,),
        in_specs=[pl.BlockSpec((tm, d), lambda i: (i, 0)),
                  pl.BlockSpec((tm, ka), lambda i: (i, 0)),
                  pl.BlockSpec((tm, kb), lambda i: (i, 0)),
                  pl.BlockSpec((ka, d), lambda i: (0, 0)),
                  pl.BlockSpec((kb, d), lambda i: (0, 0))],
        out_specs=pl.BlockSpec((tm, d), lambda i: (i, 0)),
        out_shape=jax.ShapeDtypeStruct((t, d), F32),
        compiler_params=_params("parallel"),
        name="out_projection",
    )(x, ya, yb, wa, wb)


def _swiglu_accumulate(h, wg, wu, wd, acc_ref):
    a = _dot(h, wg)
    u = _dot(h, wu)
    acc_ref[...] += _dot((a * _sigmoid(a) * u).astype(BF16), wd)


def _ffn_kernel(x_ref, g_ref, wg_ref, wu_ref, wd_ref, o_ref, h_ref, acc_ref, *, tm):
    f = pl.program_id(1)

    @pl.when(f == 0)
    def _():
        _rms_rows_to(h_ref, x_ref, g_ref, tm)
        acc_ref[...] = x_ref[...]

    _swiglu_accumulate(h_ref[...], wg_ref[...], wu_ref[...], wd_ref[...], acc_ref)

    @pl.when(f == pl.num_programs(1) - 1)
    def _():
        o_ref[...] = acc_ref[...]


def dense_ffn(x, g, wg, wu, wd, tm, tf):
    t, d = x.shape
    ff = wg.shape[1]
    return pl.pallas_call(
        functools.partial(_ffn_kernel, tm=tm),
        grid=(t // tm, ff // tf),
        in_specs=[pl.BlockSpec((tm, d), lambda i, f: (i, 0)),
                  pl.BlockSpec((1, d), lambda i, f: (0, 0)),
                  pl.BlockSpec((d, tf), lambda i, f: (0, f)),
                  pl.BlockSpec((d, tf), lambda i, f: (0, f)),
                  pl.BlockSpec((tf, d), lambda i, f: (f, 0))],
        out_specs=pl.BlockSpec((tm, d), lambda i, f: (i, 0)),
        out_shape=jax.ShapeDtypeStruct((t, d), F32),
        scratch_shapes=[pltpu.VMEM((tm, d), BF16), pltpu.VMEM((tm, d), F32)],
        compiler_params=_params("parallel", "arbitrary"),
        name="dense_ffn",
    )(x, g.reshape(1, d), wg, wu, wd)


def _ssd_kernel(z_ref, xs_ref, b_ref, c_ref, dt_ref, cw_ref, cb_ref, dtb_ref, alog_ref, dsk_ref,
                gout_ref, exp_ref, o_ref, h_ref, *, nc, unroll):
    L = SSD_CHUNK
    gw = SSD_GW
    n = SSD_STATE
    h_ref[...] = jnp.zeros_like(h_ref)
    a_neg = -jnp.exp(alog_ref[...])
    causal = (lax.broadcasted_iota(jnp.int32, (L, L), 0) >= lax.broadcasted_iota(jnp.int32, (L, L), 1))
    tri = jnp.where(causal, 1.0, 0.0).astype(BF16)
    first_half = lax.broadcasted_iota(jnp.int32, (1, 2 * SSD_HD), 1) < SSD_HD
    expm = exp_ref[...]
    srcs = (xs_ref, b_ref, c_ref)
    sh_r = lax.broadcasted_iota(jnp.int32, (L, 2 * L), 0)
    sh_c = lax.broadcasted_iota(jnp.int32, (L, 2 * L), 1)
    shifts = [jnp.where(sh_c == sh_r + (L - s), 1.0, 0.0).astype(BF16) for s in range(1, SSD_CONV)]

    def chunk(c, carry):
        r0 = pl.multiple_of(c * L, L)
        rq = pl.multiple_of(jnp.maximum(r0 - L, 0), L)
        cur = jnp.concatenate([ref[pl.ds(r0, L), :] for ref in srcs], axis=1)
        prev = jnp.concatenate([ref[pl.ds(rq, L), :] for ref in srcs], axis=1)
        prev = jnp.where(c > 0, prev, jnp.zeros_like(prev))
        ext = jnp.concatenate([prev, cur], axis=0)
        conv = cb_ref[...] + cw_ref[SSD_CONV - 1:SSD_CONV, :] * cur.astype(F32)
        for s in range(1, SSD_CONV):
            k = SSD_CONV - 1 - s
            conv = conv + cw_ref[k:k + 1, :] * _dot(shifts[s - 1], ext)
        act = conv * _sigmoid(conv)
        xs = act[:, 0:gw]
        bm = act[:, gw:gw + n]
        cm = act[:, gw + n:gw + 2 * n]

        dtr = dt_ref[pl.ds(r0, L), :] + dtb_ref[...]
        dt = jnp.maximum(dtr, 0.0) + jnp.log(1.0 + jnp.exp(-jnp.abs(dtr)))
        la = dt * a_neg
        cs = _dot3r(tri, la)
        cs_t = cs.T
        dt_t = dt.T
        bm16 = bm.astype(BF16)
        cm16 = cm.astype(BF16)
        scores = _dot_nt(cm16, bm16)
        y = _dot_nt(cm16, h_ref[...].astype(BF16)) * _dot3l(jnp.exp(cs), expm)
        w_end = dt * jnp.exp(cs[L - 1:L, :] - cs)
        xs_t = xs.T
        y_in = []
        for e in range(SSD_GHEADS):
            diff = cs[:, e:e + 1] - cs_t[e:e + 1, :]
            dec = jnp.exp(jnp.where(causal, diff, -jnp.inf))
            mm = (scores * dec * dt_t[e:e + 1, :]).astype(BF16)
            if e % 2 == 0:
                mm_even = mm
            else:
                xp = xs[:, (e - 1) * SSD_HD:(e + 1) * SSD_HD]
                rhs = jnp.concatenate([jnp.where(first_half, xp, 0.0), jnp.where(first_half, 0.0, xp)], axis=0)
                y_in.append(_dot(jnp.concatenate([mm_even, mm], axis=1), rhs.astype(BF16)))
            bw = (bm * w_end[:, e:e + 1]).astype(BF16)
            st = _dot(xs_t[e * SSD_HD:(e + 1) * SSD_HD, :].astype(BF16), bw)
            cdec = jnp.exp(cs[L - 1:L, e:e + 1])
            hs = slice(e * SSD_HD, (e + 1) * SSD_HD)
            h_ref[hs, :] = h_ref[hs, :] * cdec + st
        y = y + jnp.concatenate(y_in, axis=1) + xs * dsk_ref[...]
        z = z_ref[pl.ds(r0, L), :].astype(F32)
        u = y * (z * _sigmoid(z))
        ms = jnp.mean(u * u, axis=-1, keepdims=True)
        o_ref[pl.ds(r0, L), :] = (u * lax.rsqrt(ms + EPS) * gout_ref[...]).astype(o_ref.dtype)
        return carry

    lax.fori_loop(0, nc, chunk, 0, unroll=unroll)


def ssd_mixer(proj, dt_all, cw_g, cb_g, dtb_g, alog_g, dsk_g, gout_g, expand, bsz, s):
    assert s % SSD_CHUNK == 0
    nc = s // SSD_CHUNK
    gw = SSD_GW
    z_blk = 0
    xs_blk = SSD_INNER // gw
    b_blk = 2 * SSD_INNER // SSD_STATE
    c_blk = b_blk + SSD_GROUPS

    def seq(width, off):
        return pl.BlockSpec((None, s, width), lambda b, g: (b, 0, off + g))

    def par(rows, width):
        return pl.BlockSpec((None, rows, width), lambda b, g: (g, 0, 0))

    return pl.pallas_call(
        functools.partial(_ssd_kernel, nc=nc, unroll=_pick(nc, (4, 2, 1))),
        grid=(bsz, SSD_GROUPS),
        in_specs=[seq(gw, z_blk), seq(gw, xs_blk), seq(SSD_STATE, b_blk), seq(SSD_STATE, c_blk),
                  seq(LANES, 0),
                  par(SSD_CONV, SSD_CONV_GW), par(1, SSD_CONV_GW), par(1, LANES), par(1, LANES),
                  par(1, gw), par(1, gw),
                  pl.BlockSpec((LANES, gw), lambda b, g: (0, 0))],
        out_specs=seq(gw, 0),
        out_shape=jax.ShapeDtypeStruct((bsz, s, SSD_INNER), BF16),
        scratch_shapes=[pltpu.VMEM((gw, SSD_STATE), F32)],
        compiler_params=_params("parallel", "parallel"),
        name="ssd_mixer",
    )(proj, proj, proj, proj, dt_all, cw_g, cb_g, dtb_g, alog_g, dsk_g, gout_g, expand)


def _ssd_group_params(conv_w, conv_b, dt_bias, a_log, d_skip, g_out):
    g, gh, gw, n = SSD_GROUPS, SSD_GHEADS, SSD_GW, SSD_STATE

    def conv_cols(a):
        xs = a[..., :SSD_INNER].reshape(a.shape[:-1] + (g, gw))
        bb = a[..., SSD_INNER:SSD_INNER + g * n].reshape(a.shape[:-1] + (g, n))
        cc = a[..., SSD_INNER + g * n:].reshape(a.shape[:-1] + (g, n))
        return jnp.moveaxis(jnp.concatenate([xs, bb, cc], axis=-1), -2, 0)

    cw_g = conv_cols(conv_w.astype(F32))
    cb_g = conv_cols(conv_b.astype(F32)[None, :])

    def per_head(a):
        return jnp.pad(a.astype(F32).reshape(g, 1, gh), ((0, 0), (0, 0), (0, LANES - gh)))

    dsk_g = jnp.repeat(d_skip.astype(F32), SSD_HD).reshape(g, 1, gw)
    gout_g = g_out.astype(F32).reshape(g, 1, gw)
    expand = (jnp.arange(LANES)[:, None] == (jnp.arange(gw)[None, :] // SSD_HD)).astype(BF16)
    return cw_g, cb_g, per_head(dt_bias), per_head(a_log), dsk_g, gout_g, expand


def _router_kernel(x_ref, g_ref, wrt_ref, br_ref, h_ref, route_ref, runs_ref, cnt_ref, carry_ref, *, tm):
    @pl.when(pl.program_id(0) == 0)
    def _():
        carry_ref[...] = jnp.zeros_like(carry_ref)

    ne = N_EXPERTS
    hn = _rms(x_ref[...], g_ref[...])
    h_hi = hn.astype(BF16)
    h_ref[...] = h_hi
    h_lo = (hn - h_hi.astype(F32)).astype(BF16)
    wr = wrt_ref[...]
    w_hi = wr.astype(BF16)
    w_lo = (wr - w_hi.astype(F32)).astype(BF16)
    logits = _dot_nt(w_hi, h_hi) + _dot_nt(w_hi, h_lo) + _dot_nt(w_lo, h_hi) + br_ref[:, 0:1]
    sub = lax.broadcasted_iota(jnp.int32, (ne, tm), 0)
    l1 = jnp.max(logits, axis=0, keepdims=True)
    i1 = jnp.min(jnp.where(logits == l1, sub, ne), axis=0, keepdims=True)
    rest = jnp.where(sub == i1, -jnp.inf, logits)
    l2 = jnp.max(rest, axis=0, keepdims=True)
    i2 = jnp.min(jnp.where(rest == l2, sub, ne), axis=0, keepdims=True)
    e2 = jnp.exp(l2 - l1)
    g1 = 1.0 / (1.0 + e2)
    g2 = e2 / (1.0 + e2)
    sel = jnp.where((sub == i1) | (sub == i2), 1.0, 0.0)
    before = (lax.broadcasted_iota(jnp.int32, (tm, tm), 0) < lax.broadcasted_iota(jnp.int32, (tm, tm), 1))
    prefix = _dot(sel.astype(BF16), jnp.where(before, 1.0, 0.0).astype(BF16))
    lane = lax.broadcasted_iota(jnp.int32, (ne, LANES), 1)
    subl = lax.broadcasted_iota(jnp.int32, (ne, LANES), 0)
    n_e = (jnp.sum(sel, axis=1, keepdims=True) + jnp.zeros((ne, LANES), F32)).astype(jnp.int32)
    n_pad = (n_e + (SUBLANES - 1)) & (-SUBLANES)
    off = jnp.zeros((ne, LANES), jnp.int32)
    for j in range(ne - 1):
        off = off + jnp.where(subl > j, n_pad[j:j + 1, :], 0)
    slot = prefix + off[:, 0:1].astype(F32)
    p1 = jnp.sum(jnp.where(sub == i1, slot, 0.0), axis=0, keepdims=True)
    p2 = jnp.sum(jnp.where(sub == i2, slot, 0.0), axis=0, keepdims=True)
    route_ref[...] = jnp.where(sub == 0, p1, jnp.where(sub == 1, p2, jnp.where(
        sub == 2, g1, jnp.where(sub == 3, g2, 0.0))))
    carry = carry_ref[...]
    runs_ref[...] = jnp.where(lane == 0, off, jnp.where(lane == 1, carry, jnp.where(lane == 2, n_pad, 0)))
    carry_ref[...] = carry + n_pad
    cnt_ref[...] = carry + n_pad


def moe_router(x, g, w_router, b_router, tm):
    t, d = x.shape
    ne = N_EXPERTS
    wrt = w_router.astype(F32).T
    br = jnp.broadcast_to(b_router.astype(F32)[:, None], (ne, LANES))
    return pl.pallas_call(
        functools.partial(_router_kernel, tm=tm),
        grid=(t // tm,),
        in_specs=[pl.BlockSpec((tm, d), lambda i: (i, 0)),
                  pl.BlockSpec((1, d), lambda i: (0, 0)),
                  pl.BlockSpec((ne, d), lambda i: (0, 0)),
                  pl.BlockSpec((ne, LANES), lambda i: (0, 0))],
        out_specs=[pl.BlockSpec((tm, d), lambda i: (i, 0)),
                   pl.BlockSpec((ne, tm), lambda i: (0, i)),
                   pl.BlockSpec((ne, LANES), lambda i: (i, 0)),
                   pl.BlockSpec((ne, LANES), lambda i: (0, 0))],
        out_shape=[jax.ShapeDtypeStruct((t, d), BF16),
                   jax.ShapeDtypeStruct((ne, t), F32),
                   jax.ShapeDtypeStruct((t // tm * ne, LANES), jnp.int32),
                   jax.ShapeDtypeStruct((ne, LANES), jnp.int32)],
        scratch_shapes=[pltpu.VMEM((ne, LANES), jnp.int32)],
        compiler_params=_params("arbitrary"),
        name="moe_router",
    )(x, g.reshape(1, d), wrt, br)


def _pieces(length, lo_bit, hi_bit):
    for b in range(lo_bit, hi_bit):
        n = 1 << b
        yield n, length & (n - 1), ((length >> b) & 1) == 1


def _run_copies(tab_ref, tile, tile_ref, buf_ref, sem, tm, to_buf, wait):
    for e in range(N_EXPERTS):
        base = (tile * N_EXPERTS + e) * 3
        off, dst, rows = tab_ref[base], tab_ref[base + 1], tab_ref[base + 2]
        for n, lo, present in _pieces(rows, SUBLANES.bit_length() - 1, tm.bit_length()):
            in_tile = tile_ref.at[pl.ds(pl.multiple_of(off + lo, SUBLANES), n), :]
            in_buf = buf_ref.at[pl.ds(pl.multiple_of(dst + lo, SUBLANES), n), :]
            cp = pltpu.make_async_copy(in_tile, in_buf, sem) if to_buf else pltpu.make_async_copy(in_buf, in_tile, sem)

            @pl.when(present)
            def _():
                if wait:
                    cp.wait()
                else:
                    cp.start()


def _dispatch_kernel(tab_ref, meta_ref, route_ref, h_ref, xb_ref, srt_ref, zero_ref, sem, zsem, *, tm, n_tiles):
    i = pl.program_id(0)
    nt = pl.num_programs(0)
    slot = i % 2
    rs = srt_ref.shape[1]

    def runs(tile, slot, wait):
        _run_copies(tab_ref, tile, srt_ref.at[slot], xb_ref, sem.at[slot], tm, True, wait)

    @pl.when(i >= 2)
    def _():
        runs(i - 2, slot, True)

    pos = route_ref[0:2, :].astype(jnp.int32)
    row = lax.broadcasted_iota(jnp.int32, (rs, tm), 0)
    onehot = jnp.where((row == pos[0:1, :]) | (row == pos[1:2, :]), 1.0, 0.0).astype(BF16)
    srt_ref[slot] = _dot(onehot, h_ref[...])
    runs(i, slot, False)

    @pl.when(i == nt - 1)
    def _():
        zero_ref[...] = jnp.zeros_like(zero_ref)
        zrows = zero_ref.shape[0]
        per = EXPERT_ROWS // zrows
        first_unused = meta_ref[2 * N_EXPERTS] * per
        for wait in (False, True):
            def go(cp, cond):
                @pl.when(cond)
                def _():
                    if wait:
                        cp.wait()
                    else:
                        cp.start()

            for e in range(N_EXPERTS):
                c = meta_ref[e]
                first = meta_ref[N_EXPERTS + e]
                tail = (-c) & (EXPERT_ROWS - 1)
                for n, lo, present in _pieces(tail, SUBLANES.bit_length() - 1, EXPERT_ROWS.bit_length() - 1):
                    row0 = pl.multiple_of(first + c + lo, SUBLANES)
                    go(pltpu.make_async_copy(zero_ref.at[pl.ds(0, n), :], xb_ref.at[pl.ds(row0, n), :], zsem), present)
            for j in range(N_EXPERTS * per):
                row0 = pl.multiple_of(jnp.minimum(first_unused + j, n_tiles * per - 1) * zrows, zrows)
                go(pltpu.make_async_copy(zero_ref, xb_ref.at[pl.ds(row0, zrows), :], zsem),
                   first_unused + j < n_tiles * per)

        @pl.when(i >= 1)
        def _():
            runs(i - 1, 1 - slot, True)

        runs(i, slot, True)


def moe_dispatch(tab, meta, route, h, tm, n_tiles):
    t, d = h.shape
    ne = route.shape[0]
    rs = TOP_K * tm + LANES
    return pl.pallas_call(
        functools.partial(_dispatch_kernel, tm=tm, n_tiles=n_tiles),
        grid=(t // tm,),
        in_specs=[pl.BlockSpec(memory_space=pltpu.SMEM),
                  pl.BlockSpec(memory_space=pltpu.SMEM),
                  pl.BlockSpec((ne, tm), lambda i: (0, i)),
                  pl.BlockSpec((tm, d), lambda i: (i, 0))],
        out_specs=pl.BlockSpec(memory_space=pl.ANY),
        out_shape=jax.ShapeDtypeStruct((n_tiles * EXPERT_ROWS, d), F32),
        scratch_shapes=[pltpu.VMEM((2, rs, d), F32),
                        pltpu.VMEM((EXPERT_ROWS // 2, d), F32),
                        pltpu.SemaphoreType.DMA((2,)), pltpu.SemaphoreType.DMA(())],
        compiler_params=_params("arbitrary"),
        name="moe_dispatch",
    )(tab, meta, route, h)


def _expert_kernel(te_ref, nu_ref, xb_ref, wg_ref, wu_ref, wd_ref, o_ref, acc_ref):
    c = pl.program_id(0)
    f = pl.program_id(1)
    last = pl.num_programs(1) - 1
    used = c < nu_ref[0]

    @pl.when(used)
    def _():
        @pl.when(f == 0)
        def _():
            acc_ref[...] = jnp.zeros_like(acc_ref)

        _swiglu_accumulate(xb_ref[...].astype(BF16), wg_ref[...].astype(BF16), wu_ref[...].astype(BF16),
                           wd_ref[...].astype(BF16), acc_ref)

        @pl.when(f == last)
        def _():
            o_ref[...] = acc_ref[...]

    @pl.when(jnp.logical_and(jnp.logical_not(used), f == last))
    def _():
        o_ref[...] = jnp.zeros_like(o_ref)


def expert_ffn(tile_expert, n_used, xb, wg, wu, wd, tf):
    n_rows, d = xb.shape
    ff = wg.shape[2]
    r = EXPERT_ROWS
    nf = ff // tf

    def wcol(c, f, te, nu):
        return (te[c], 0, jnp.where(c < nu[0], f, nf - 1))

    def wrow(c, f, te, nu):
        return (te[c], jnp.where(c < nu[0], f, nf - 1), 0)

    grid_spec = pltpu.PrefetchScalarGridSpec(
        num_scalar_prefetch=2,
        grid=(n_rows // r, nf),
        in_specs=[pl.BlockSpec((r, d), lambda c, f, te, nu: (c, 0)),
                  pl.BlockSpec((None, d, tf), wcol),
                  pl.BlockSpec((None, d, tf), wcol),
                  pl.BlockSpec((None, tf, d), wrow)],
        out_specs=pl.BlockSpec((r, d), lambda c, f, te, nu: (c, 0)),
        scratch_shapes=[pltpu.VMEM((r, d), F32)],
    )
    return pl.pallas_call(
        _expert_kernel,
        grid_spec=grid_spec,
        out_shape=jax.ShapeDtypeStruct((n_rows, d), F32),
        compiler_params=_params("parallel", "arbitrary"),
        name="expert_ffn",
    )(tile_expert, n_used, xb, wg, wu, wd)


def _combine_kernel(tab_ref, route_ref, x_ref, g_ref, yb_ref, o_ref, srt_ref, sem, *, tm):
    i = pl.program_id(0)
    nt = pl.num_programs(0)
    slot = i % 2
    rs = srt_ref.shape[1]

    def runs(tile, slot, wait):
        _run_copies(tab_ref, tile, srt_ref.at[slot], yb_ref, sem.at[slot], tm, False, wait)

    @pl.when(i == 0)
    def _():
        srt_ref[...] = jnp.zeros_like(srt_ref)
        runs(0, 0, False)

    @pl.when(i + 1 < nt)
    def _():
        runs(i + 1, 1 - slot, False)

    runs(i, slot, True)
    aux = route_ref[...]
    aux_t = jnp.concatenate([aux, jnp.zeros((LANES - aux.shape[0], tm), F32)], axis=0).T
    srt = srt_ref[slot].astype(BF16)
    col = lax.broadcasted_iota(jnp.int32, (tm, rs), 1)
    ys = []
    for k in range(TOP_K):
        onehot = jnp.where(col == aux_t[:, k:k + 1].astype(jnp.int32), 1.0, 0.0).astype(BF16)
        ys.append(aux_t[:, TOP_K + k:TOP_K + k + 1] * _dot(onehot, srt))
    o_ref[...] = _rms(x_ref[...] + (ys[0] + ys[1]), g_ref[...])


def combine_final(tab, route, x, g, yb, tm):
    t, d = x.shape
    ne = route.shape[0]
    rs = TOP_K * tm + LANES
    return pl.pallas_call(
        functools.partial(_combine_kernel, tm=tm),
        grid=(t // tm,),
        in_specs=[pl.BlockSpec(memory_space=pltpu.SMEM),
                  pl.BlockSpec((ne, tm), lambda i: (0, i)),
                  pl.BlockSpec((tm, d), lambda i: (i, 0)),
                  pl.BlockSpec((1, d), lambda i: (0, 0)),
                  pl.BlockSpec(memory_space=pl.ANY)],
        out_specs=pl.BlockSpec((tm, d), lambda i: (i, 0)),
        out_shape=jax.ShapeDtypeStruct((t, d), F32),
        scratch_shapes=[pltpu.VMEM((2, rs, d), F32), pltpu.SemaphoreType.DMA((2,))],
        compiler_params=_params("arbitrary"),
        name="combine_final",
    )(tab, route, x, g.reshape(1, d), yb)


def _pick(n, prefs):
    for p in prefs:
        if n % p == 0:
            return p
    return n


def kernel(x, mem, g_mix, g_mem, w_in_moba, w_in_ssd, w_mem_kv, w_out, rel_bias, conv_w, conv_b, dt_bias, a_log, d_skip, g_ssd_out, g_ffn, w_ffn_gate, w_ffn_up, w_ffn_down, w_router, b_router, w_exp_gate, w_exp_up, w_exp_down, g_final):
    bsz, s, d = x.shape
    t = bsz * s
    m_len = mem.shape[1]
    tm = _pick(t, (1024, 512, 256))
    xf = x.reshape(t, d).astype(F32)
    memf = mem.reshape(bsz * m_len, d).astype(F32)
    tmm = _pick(bsz * m_len, (1024, 512, 256))

    tmn = _pick(t, (512, 256))
    proj0 = norm_matmul(xf, g_mix[0], w_in_moba[0].astype(BF16), BF16, tmn, 1280)
    kv0 = norm_matmul(memf, g_mem[0], w_mem_kv[0].astype(BF16), BF16, tmm, 1024)
    proj0 = proj0.reshape(bsz, s, -1)
    y_tok = moba_attention(proj0, _moba_bias_tables(rel_bias), bsz, s)
    y_mem = memory_attention(proj0, 3 * MOBA_HEADS, kv0.reshape(bsz, m_len, -1), bsz, s)
    wo = w_out[0].astype(BF16)
    x1 = out_projection(xf, y_tok.reshape(t, -1), y_mem.reshape(t, -1), wo[:MOBA_WIDTH], wo[MOBA_WIDTH:], 512)
    x1 = dense_ffn(x1, g_ffn[0], w_ffn_gate[0].astype(BF16), w_ffn_up[0].astype(BF16),
                   w_ffn_down[0].astype(BF16), tm, 512)

    n_zx = SSD_INNER + SSD_INNER + 2 * SSD_GROUPS * SSD_STATE
    w1 = w_in_ssd[0]
    w_main = jnp.concatenate([w1[:, :n_zx], w1[:, n_zx + SSD_HEADS:]], axis=1).astype(BF16)
    w_dt = jnp.pad(w1[:, n_zx:n_zx + SSD_HEADS].reshape(d, SSD_GROUPS, SSD_GHEADS),
                   ((0, 0), (0, 0), (0, LANES - SSD_GHEADS))).reshape(d, SSD_GROUPS * LANES).astype(BF16)
    proj1 = norm_matmul(x1, g_mix[1], w_main, BF16, tmn, 1536).reshape(bsz, s, -1)
    dt_all = norm_matmul(x1, g_mix[1], w_dt, F32, tmn, SSD_GROUPS * LANES).reshape(bsz, s, -1)
    kv1 = norm_matmul(memf, g_mem[1], w_mem_kv[1].astype(BF16), BF16, tmm, 1024)
    y_tok = ssd_mixer(proj1, dt_all, *_ssd_group_params(conv_w[0], conv_b[0], dt_bias[0], a_log[0],
                                                         d_skip[0], g_ssd_out[0]), bsz, s)
    y_mem = memory_attention(proj1, n_zx // HD, kv1.reshape(bsz, m_len, -1), bsz, s)
    wo = w_out[1].astype(BF16)
    x2 = out_projection(x1, y_tok.reshape(t, -1), y_mem.reshape(t, -1), wo[:SSD_INNER], wo[SSD_INNER:], 512)

    tmr = _pick(t, (512, 256))
    nt = t // tmr
    h, route, runs, counts = moe_router(x2, g_ffn[1], w_router[0], b_router[0], tmr)
    r = EXPERT_ROWS
    n_tiles = -(-(t * TOP_K + nt * N_EXPERTS * (SUBLANES - 1)) // r) + N_EXPERTS
    cnt = counts[:, 0]
    tiles_per = (cnt + r - 1) // r
    ends = jnp.cumsum(tiles_per)
    first_row = (ends - tiles_per) * r
    n_used = ends[-1].astype(jnp.int32)
    step = jnp.minimum(jnp.arange(n_tiles, dtype=jnp.int32), n_used - 1)
    tile_expert = jnp.sum(step[:, None] >= ends[None, :], axis=1).astype(jnp.int32)
    tab = runs.reshape(nt, N_EXPERTS, LANES)[:, :, :3]
    tab = tab.at[:, :, 1].add(first_row[None, :]).reshape(-1).astype(jnp.int32)
    meta = jnp.concatenate([cnt, first_row, n_used.reshape(1)]).astype(jnp.int32)
    xb = moe_dispatch(tab, meta, route, h, tmr, n_tiles)
    yb = expert_ffn(tile_expert, n_used.reshape(1), xb, w_exp_gate[0], w_exp_up[0], w_exp_down[0], 512)
    out = combine_final(tab, route, x2, g_final, yb, tmr)
    return out.reshape(bsz, s, d).astype(x.dtype)
```

```python
import functools
import math

import jax
import jax.numpy as jnp
from jax import lax
from jax.experimental import pallas as pl
from jax.experimental.pallas import tpu as pltpu

F32 = jnp.float32
BF16 = jnp.bfloat16

D_MODEL = 1024
HD = 128
MOBA_HEADS = 12
MOBA_WIDTH = MOBA_HEADS * HD
MOBA_BLOCK = 256
MOBA_TOPK = 3
MOBA_HEADS_PER_STEP = 2
REL_BUCKETS = 32
REL_MAX_DIST = 128
MEM_HEADS = 4
MEM_WIDTH = MEM_HEADS * HD
SSD_HEADS = 24
SSD_HD = 64
SSD_INNER = SSD_HEADS * SSD_HD
SSD_GROUPS = 4
SSD_STATE = 128
SSD_CONV = 4
SSD_CHUNK = 128
SSD_GHEADS = SSD_HEADS // SSD_GROUPS
SSD_GW = SSD_GHEADS * SSD_HD
SSD_CONV_GW = SSD_GW + 2 * SSD_STATE
D_FF = 3584
N_EXPERTS = 8
TOP_K = 2
EPS = 1e-6

LOG2E = math.log2(math.e)
LANES = 128
SUBLANES = 8
VMEM_LIMIT = 48 * 1024 * 1024
EXPERT_ROWS = 1024
EXPERT_ROW_CHUNK = 256


def _dot(a, b):
    return jnp.dot(a, b, preferred_element_type=F32)


def _dot_nt(a, b):
    return lax.dot_general(a, b, (((1,), (1,)), ((), ())), preferred_element_type=F32)


def _split(x, pieces):
    out = []
    for _ in range(pieces):
        p = x.astype(BF16)
        out.append(p)
        x = x - p.astype(F32)
    return out


def _dot_split_l(x, w, pieces):
    return sum(_dot(p, w) for p in _split(x, pieces))


def _dot_split_r(w, x, pieces):
    return sum(_dot(w, p) for p in _split(x, pieces))


def _sigmoid(x):
    return 1.0 / (1.0 + jnp.exp(-x))


def _rms(x, g):
    ms = jnp.mean(x * x, axis=-1, keepdims=True)
    return x * lax.rsqrt(ms + EPS) * g


def _rms_rows_to(h_ref, x_ref, g_ref, rows):
    step = min(rows, 256)
    for r in range(0, rows, step):
        h_ref[r:r + step, :] = _rms(x_ref[r:r + step, :], g_ref[...]).astype(h_ref.dtype)


def _params(*sem):
    return pltpu.CompilerParams(dimension_semantics=sem, vmem_limit_bytes=VMEM_LIMIT)


def _normmm_kernel(x_ref, g_ref, w_ref, o_ref, *, rc, tn):
    tm, n = o_ref.shape

    def norm(r):
        return _rms(x_ref[r * rc:(r + 1) * rc, :], g_ref[...]).astype(BF16)

    def matmuls(r, h):
        for j in range(n // tn):
            cols = slice(j * tn, (j + 1) * tn)
            o_ref[r * rc:(r + 1) * rc, cols] = _dot(h, w_ref[:, cols]).astype(o_ref.dtype)

    live = {}
    for step in range(tm // rc + 1):
        if step < tm // rc:
            live[step] = norm(step)
        if step >= 1:
            matmuls(step - 1, live.pop(step - 1))


def norm_matmul(x, g, w, out_dtype, tm, tn):
    t, d = x.shape
    n = w.shape[1]
    return pl.pallas_call(
        functools.partial(_normmm_kernel, rc=min(tm, 256), tn=tn),
        grid=(t // tm,),
        in_specs=[pl.BlockSpec((tm, d), lambda i: (i, 0)),
                  pl.BlockSpec((1, d), lambda i: (0, 0)),
                  pl.BlockSpec((d, n), lambda i: (0, 0))],
        out_specs=pl.BlockSpec((tm, n), lambda i: (i, 0)),
        out_shape=jax.ShapeDtypeStruct((t, n), out_dtype),
        compiler_params=_params("parallel"),
        name="norm_matmul",
    )(x, g.reshape(1, d), w)


def _moba_kernel(q_ref, k_ref, v_ref, tab_ref, o_ref, vt_ref, km_ref, *, nb, n_sel, heads):
    for hh in range(heads):
        lanes = pl.ds(hh * HD, HD)
        _moba_head(q_ref.at[:, lanes], k_ref.at[:, lanes], v_ref.at[:, lanes], tab_ref.at[hh],
                   o_ref.at[:, lanes], vt_ref.at[hh], km_ref.at[hh], nb=nb, n_sel=n_sel)


def _moba_head(q_ref, k_ref, v_ref, tab_ref, o_ref, vt_ref, km_ref, *, nb, n_sel):
    blk = MOBA_BLOCK
    scale = HD ** -0.5 * LOG2E
    nbp = km_ref.shape[0]
    km_ref[...] = jnp.zeros_like(km_ref)
    for j in range(nb):
        rows = slice(j * blk, (j + 1) * blk)
        vt_ref[:, rows] = v_ref[rows, :].astype(F32).T.astype(BF16)
        km_ref[j:j + 1, :] = jnp.mean(k_ref[rows, :].astype(F32), axis=0, keepdims=True)
    km = km_ref[...]
    km_hi = km.astype(BF16)
    km_lo = (km - km_hi.astype(F32)).astype(BF16)
    q_all = q_ref[...]
    gate = _dot_nt(km_hi, q_all) + _dot_nt(km_lo, q_all)
    sub = lax.broadcasted_iota(jnp.int32, (nbp, blk), 0)
    bias_far = tab_ref[2, 0:1, :]

    def scores(i):
        return _dot_nt(k_ref[0:(i + 1) * blk, :], q_ref[i * blk:(i + 1) * blk, :])

    def softmax(i, s_all):
        rows = slice(i * blk, (i + 1) * blk)
        if i > 0:
            valid = sub < i
            gm = jnp.where(valid, gate[:, rows], -jnp.inf)
            rank = jnp.zeros((nbp, blk), F32)
            for jp in range(i):
                row = gm[jp:jp + 1, :]
                beats = (row > gm) | ((row == gm) & (sub > jp))
                rank = rank + jnp.where(beats, 1.0, 0.0)
            selm = jnp.where(valid & (rank < n_sel), 0.0, -jnp.inf)
        bands = []
        for j in range(i + 1):
            sj = s_all[j * blk:(j + 1) * blk, :] * scale
            if j == i:
                sj = sj + tab_ref[0]
            elif j == i - 1:
                sj = sj + (tab_ref[1] + selm[j:j + 1, :])
            else:
                sj = sj + (bias_far + selm[j:j + 1, :])
            bands.append(sj)
        s = jnp.concatenate(bands, axis=0) if i > 0 else bands[0]
        m = jnp.max(s, axis=0, keepdims=True)
        p = jnp.exp2(s - m)
        return p.astype(BF16), jnp.sum(p, axis=0, keepdims=True)

    def output(i, p, l):
        acc = _dot(vt_ref[:, 0:(i + 1) * blk], p)
        o_ref[i * blk:(i + 1) * blk, :] = (acc / l).T.astype(o_ref.dtype)

    s_live, p_live = {}, {}
    units = [sorted({u, nb - 1 - u}) for u in range((nb + 1) // 2)]
    for step in range(len(units) + 2):
        if step < len(units):
            for i in units[step]:
                s_live[i] = scores(i)
        if 1 <= step <= len(units):
            for i in units[step - 1]:
                p_live[i] = softmax(i, s_live.pop(i))
        if step >= 2:
            for i in units[step - 2]:
                output(i, *p_live.pop(i))


def _t5_bucket_idx(dist):
    n = jnp.maximum(dist, 0)
    max_exact = REL_BUCKETS // 2
    large = max_exact + (jnp.log(jnp.maximum(n, 1).astype(F32) / max_exact)
                         / math.log(REL_MAX_DIST / max_exact)
                         * (REL_BUCKETS - max_exact)).astype(jnp.int32)
    large = jnp.minimum(large, REL_BUCKETS - 1)
    return jnp.where(n < max_exact, n, large)


def _bias_table_kernel(rb_ref, idx_ref, o_ref):
    h = pl.program_id(0)
    for t in range(3):
        idx = idx_ref[t]
        acc = jnp.full(idx.shape, -jnp.inf, F32)
        for b in range(REL_BUCKETS):
            acc = jnp.where(idx == b, rb_ref[h, b], acc)
        o_ref[t] = acc * LOG2E


def _moba_bias_tables(rel_bias):
    blk = MOBA_BLOCK
    loc = jnp.arange(blk)
    d0 = loc[None, :] - loc[:, None]
    idx = jnp.stack([jnp.where(d0 >= 0, _t5_bucket_idx(d0), -1),
                     _t5_bucket_idx(d0 + blk),
                     _t5_bucket_idx(d0 + 2 * blk)]).astype(jnp.int32)
    return pl.pallas_call(
        _bias_table_kernel,
        grid=(MOBA_HEADS,),
        in_specs=[pl.BlockSpec(memory_space=pltpu.SMEM),
                  pl.BlockSpec((3, blk, blk), lambda h: (0, 0, 0))],
        out_specs=pl.BlockSpec((None, 3, blk, blk), lambda h: (h, 0, 0, 0)),
        out_shape=jax.ShapeDtypeStruct((MOBA_HEADS, 3, blk, blk), F32),
        compiler_params=_params("parallel"),
        name="moba_bias_tables",
    )(rel_bias.T.astype(F32), idx)


def moba_attention(proj, tabs, bsz, s):
    assert s % MOBA_BLOCK == 0
    nb = s // MOBA_BLOCK
    nbp = -(-nb // 8) * 8
    n_sel = min(MOBA_TOPK, nb - 1)
    blk = MOBA_BLOCK

    hps = MOBA_HEADS_PER_STEP
    groups = MOBA_HEADS // hps

    def col(off):
        return pl.BlockSpec((None, s, hps * HD), lambda h, b: (b, 0, off + h))

    return pl.pallas_call(
        functools.partial(_moba_kernel, nb=nb, n_sel=n_sel, heads=hps),
        grid=(groups, bsz),
        in_specs=[col(0), col(groups), col(2 * groups),
                  pl.BlockSpec((hps, 3, blk, blk), lambda h, b: (h, 0, 0, 0))],
        out_specs=pl.BlockSpec((None, s, hps * HD), lambda h, b: (b, 0, h)),
        out_shape=jax.ShapeDtypeStruct((bsz, s, MOBA_WIDTH), BF16),
        scratch_shapes=[pltpu.VMEM((hps, HD, s), BF16),
                        pltpu.VMEM((hps, nbp, HD), F32)],
        compiler_params=_params("parallel", "parallel"),
        name="moba_attention",
    )(proj, proj, proj, tabs)


def _memattn_kernel(q_ref, k_ref, v_ref, o_ref, *, n_chunks, qc):
    scale = HD ** -0.5 * LOG2E
    k = k_ref[...]
    vt = v_ref[...].astype(F32).T.astype(BF16)
    def scores(c):
        return _dot_nt(k, q_ref[c * qc:(c + 1) * qc, :])

    def softmax(s):
        s = s * scale
        p = jnp.exp2(s - jnp.max(s, axis=0, keepdims=True))
        return p.astype(BF16), jnp.sum(p, axis=0, keepdims=True)

    def output(c, p, l):
        o_ref[c * qc:(c + 1) * qc, :] = (_dot(vt, p) / l).T.astype(o_ref.dtype)

    s_live, p_live = {}, {}
    for step in range(n_chunks + 2):
        if step < n_chunks:
            s_live[step] = scores(step)
        if 1 <= step <= n_chunks:
            p_live[step - 1] = softmax(s_live.pop(step - 1))
        if step >= 2:
            output(step - 2, *p_live.pop(step - 2))


def memory_attention(proj, q_off, kv, bsz, s):
    m_len = kv.shape[1]
    qc = 256
    return pl.pallas_call(
        functools.partial(_memattn_kernel, n_chunks=s // qc, qc=qc),
        grid=(bsz, MEM_HEADS),
        in_specs=[pl.BlockSpec((None, s, HD), lambda b, h: (b, 0, q_off + h)),
                  pl.BlockSpec((None, m_len, HD), lambda b, h: (b, 0, h)),
                  pl.BlockSpec((None, m_len, HD), lambda b, h: (b, 0, MEM_HEADS + h))],
        out_specs=pl.BlockSpec((None, s, HD), lambda b, h: (b, 0, h)),
        out_shape=jax.ShapeDtypeStruct((bsz, s, MEM_WIDTH), BF16),
        compiler_params=_params("parallel", "parallel"),
        name="memory_attention",
    )(proj, kv, kv)


def _outproj_kernel(x_ref, ya_ref, yb_ref, wa_ref, wb_ref, o_ref):
    o_ref[...] = x_ref[...] + _dot(ya_ref[...], wa_ref[...]) + _dot(yb_ref[...], wb_ref[...])


def out_projection(x, ya, yb, wa, wb, tm):
    t, d = x.shape
    ka, kb = ya.shape[1], yb.shape[1]
    return pl.pallas_call(
        _outproj_kernel,
        grid=(t // tm,),
        in_specs=[pl.BlockSpec((tm, d), lambda i: (i, 0)),
                  pl.BlockSpec((tm, ka), lambda i: (i, 0)),
                  pl.BlockSpec((tm, kb), lambda i: (i, 0)),
                  pl.BlockSpec((ka, d), lambda i: (0, 0)),
                  pl.BlockSpec((kb, d), lambda i: (0, 0))],
        out_specs=pl.BlockSpec((tm, d), lambda i: (i, 0)),
        out_shape=jax.ShapeDtypeStruct((t, d), F32),
        compiler_params=_params("parallel"),
        name="out_projection",
    )(x, ya, yb, wa, wb)


def _swiglu_accumulate(h, wg, wu, wd, acc_ref):
    a = _dot(h, wg)
    u = _dot(h, wu)
    acc_ref[...] += _dot((a * _sigmoid(a) * u).astype(BF16), wd)


def _ffn_kernel(x_ref, g_ref, wg_ref, wu_ref, wd_ref, o_ref, h_ref, acc_ref, *, tm):
    f = pl.program_id(1)

    @pl.when(f == 0)
    def _():
        _rms_rows_to(h_ref, x_ref, g_ref, tm)
        acc_ref[...] = x_ref[...]

    _swiglu_accumulate(h_ref[...], wg_ref[...], wu_ref[...], wd_ref[...], acc_ref)

    @pl.when(f == pl.num_programs(1) - 1)
    def _():
        o_ref[...] = acc_ref[...]


def dense_ffn(x, g, wg, wu, wd, tm, tf):
    t, d = x.shape
    ff = wg.shape[1]
    return pl.pallas_call(
        functools.partial(_ffn_kernel, tm=tm),
        grid=(t // tm, ff // tf),
        in_specs=[pl.BlockSpec((tm, d), lambda i, f: (i, 0)),
                  pl.BlockSpec((1, d), lambda i, f: (0, 0)),
                  pl.BlockSpec((d, tf), lambda i, f: (0, f)),
                  pl.BlockSpec((d, tf), lambda i, f: (0, f)),
                  pl.BlockSpec((tf, d), lambda i, f: (f, 0))],
        out_specs=pl.BlockSpec((tm, d), lambda i, f: (i, 0)),
        out_shape=jax.ShapeDtypeStruct((t, d), F32),
        scratch_shapes=[pltpu.VMEM((tm, d), BF16), pltpu.VMEM((tm, d), F32)],
        compiler_params=_params("parallel", "arbitrary"),
        name="dense_ffn",
    )(x, g.reshape(1, d), wg, wu, wd)


def _ssd_kernel(z_ref, xs_ref, b_ref, c_ref, dt_ref, cw_ref, cb_ref, dtb_ref, alog_ref, dsk_ref,
                gout_ref, exp_ref, o_ref, h_ref, *, nc, unroll):
    L = SSD_CHUNK
    gw = SSD_GW
    n = SSD_STATE
    h_ref[...] = jnp.zeros_like(h_ref)
    a_neg = -jnp.exp(alog_ref[...])
    causal = (lax.broadcasted_iota(jnp.int32, (L, L), 0) >= lax.broadcasted_iota(jnp.int32, (L, L), 1))
    tri = jnp.where(causal, 1.0, 0.0).astype(BF16)
    first_half = lax.broadcasted_iota(jnp.int32, (1, 2 * SSD_HD), 1) < SSD_HD
    expm = exp_ref[...]
    srcs = (xs_ref, b_ref, c_ref)
    sh_r = lax.broadcasted_iota(jnp.int32, (L, 2 * L), 0)
    sh_c = lax.broadcasted_iota(jnp.int32, (L, 2 * L), 1)
    shifts = [jnp.where(sh_c == sh_r + (L - s), 1.0, 0.0).astype(BF16) for s in range(1, SSD_CONV)]

    def chunk(c, carry):
        r0 = pl.multiple_of(c * L, L)
        rq = pl.multiple_of(jnp.maximum(r0 - L, 0), L)
        cur = jnp.concatenate([ref[pl.ds(r0, L), :] for ref in srcs], axis=1)
        prev = jnp.concatenate([ref[pl.ds(rq, L), :] for ref in srcs], axis=1)
        prev = jnp.where(c > 0, prev, jnp.zeros_like(prev))
        ext = jnp.concatenate([prev, cur], axis=0)
        conv = cb_ref[...] + cw_ref[SSD_CONV - 1:SSD_CONV, :] * cur.astype(F32)
        for s in range(1, SSD_CONV):
            k = SSD_CONV - 1 - s
            conv = conv + cw_ref[k:k + 1, :] * _dot(shifts[s - 1], ext)
        act = conv * _sigmoid(conv)
        xs = act[:, 0:gw]
        bm = act[:, gw:gw + n]
        cm = act[:, gw + n:gw + 2 * n]

        dtr = dt_ref[pl.ds(r0, L), :] + dtb_ref[...]
        dt = jnp.maximum(dtr, 0.0) + jnp.log(1.0 + jnp.exp(-jnp.abs(dtr)))
        la = dt * a_neg
        cs = _dot_split_r(tri, la, 3)
        cs_t = cs.T
        dt_t = dt.T
        bm16 = bm.astype(BF16)
        cm16 = cm.astype(BF16)
        scores = _dot_nt(cm16, bm16)
        y = _dot_nt(cm16, h_ref[...].astype(BF16)) * _dot_split_l(jnp.exp(cs), expm, 2)
        w_end = dt * jnp.exp(cs[L - 1:L, :] - cs)
        xs_t = xs.T
        y_in = []
        for e in range(SSD_GHEADS):
            diff = cs[:, e:e + 1] - cs_t[e:e + 1, :]
            dec = jnp.exp(jnp.where(causal, diff, -jnp.inf))
            mm = (scores * dec * dt_t[e:e + 1, :]).astype(BF16)
            if e % 2 == 0:
                mm_even = mm
            else:
                xp = xs[:, (e - 1) * SSD_HD:(e + 1) * SSD_HD]
                rhs = jnp.concatenate([jnp.where(first_half, xp, 0.0), jnp.where(first_half, 0.0, xp)], axis=0)
                y_in.append(_dot(jnp.concatenate([mm_even, mm], axis=1), rhs.astype(BF16)))
            bw = (bm * w_end[:, e:e + 1]).astype(BF16)
            st = _dot(xs_t[e * SSD_HD:(e + 1) * SSD_HD, :].astype(BF16), bw)
            cdec = jnp.exp(cs[L - 1:L, e:e + 1])
            hs = slice(e * SSD_HD, (e + 1) * SSD_HD)
            h_ref[hs, :] = h_ref[hs, :] * cdec + st
        y = y + jnp.concatenate(y_in, axis=1) + xs * dsk_ref[...]
        z = z_ref[pl.ds(r0, L), :].astype(F32)
        u = y * (z * _sigmoid(z))
        ms = jnp.mean(u * u, axis=-1, keepdims=True)
        o_ref[pl.ds(r0, L), :] = (u * lax.rsqrt(ms + EPS) * gout_ref[...]).astype(o_ref.dtype)
        return carry

    lax.fori_loop(0, nc, chunk, 0, unroll=unroll)


def ssd_mixer(proj, dt_all, cw_g, cb_g, dtb_g, alog_g, dsk_g, gout_g, expand, bsz, s):
    assert s % SSD_CHUNK == 0
    nc = s // SSD_CHUNK
    gw = SSD_GW
    z_blk = 0
    xs_blk = SSD_INNER // gw
    b_blk = 2 * SSD_INNER // SSD_STATE
    c_blk = b_blk + SSD_GROUPS

    def seq(width, off):
        return pl.BlockSpec((None, s, width), lambda b, g: (b, 0, off + g))

    def par(rows, width):
        return pl.BlockSpec((None, rows, width), lambda b, g: (g, 0, 0))

    return pl.pallas_call(
        functools.partial(_ssd_kernel, nc=nc, unroll=_pick(nc, (4, 2, 1))),
        grid=(bsz, SSD_GROUPS),
        in_specs=[seq(gw, z_blk), seq(gw, xs_blk), seq(SSD_STATE, b_blk), seq(SSD_STATE, c_blk),
                  seq(LANES, 0),
                  par(SSD_CONV, SSD_CONV_GW), par(1, SSD_CONV_GW), par(1, LANES), par(1, LANES),
                  par(1, gw), par(1, gw),
                  pl.BlockSpec((LANES, gw), lambda b, g: (0, 0))],
        out_specs=seq(gw, 0),
        out_shape=jax.ShapeDtypeStruct((bsz, s, SSD_INNER), BF16),
        scratch_shapes=[pltpu.VMEM((gw, SSD_STATE), F32)],
        compiler_params=_params("parallel", "parallel"),
        name="ssd_mixer",
    )(proj, proj, proj, proj, dt_all, cw_g, cb_g, dtb_g, alog_g, dsk_g, gout_g, expand)


def _ssd_group_params(conv_w, conv_b, dt_bias, a_log, d_skip, g_out):
    g, gh, gw, n = SSD_GROUPS, SSD_GHEADS, SSD_GW, SSD_STATE

    def conv_cols(a):
        xs = a[..., :SSD_INNER].reshape(a.shape[:-1] + (g, gw))
        bb = a[..., SSD_INNER:SSD_INNER + g * n].reshape(a.shape[:-1] + (g, n))
        cc = a[..., SSD_INNER + g * n:].reshape(a.shape[:-1] + (g, n))
        return jnp.moveaxis(jnp.concatenate([xs, bb, cc], axis=-1), -2, 0)

    cw_g = conv_cols(conv_w.astype(F32))
    cb_g = conv_cols(conv_b.astype(F32)[None, :])

    def per_head(a):
        return jnp.pad(a.astype(F32).reshape(g, 1, gh), ((0, 0), (0, 0), (0, LANES - gh)))

    dsk_g = jnp.repeat(d_skip.astype(F32), SSD_HD).reshape(g, 1, gw)
    gout_g = g_out.astype(F32).reshape(g, 1, gw)
    expand = (jnp.arange(LANES)[:, None] == (jnp.arange(gw)[None, :] // SSD_HD)).astype(BF16)
    return cw_g, cb_g, per_head(dt_bias), per_head(a_log), dsk_g, gout_g, expand


def _router_kernel(x_ref, g_ref, wrt_ref, br_ref, h_ref, route_ref, runs_ref, cnt_ref, carry_ref, *, tm):
    @pl.when(pl.program_id(0) == 0)
    def _():
        carry_ref[...] = jnp.zeros_like(carry_ref)

    ne = N_EXPERTS
    hn = _rms(x_ref[...], g_ref[...])
    h_hi = hn.astype(BF16)
    h_ref[...] = h_hi
    h_lo = (hn - h_hi.astype(F32)).astype(BF16)
    wr = wrt_ref[...]
    w_hi = wr.astype(BF16)
    w_lo = (wr - w_hi.astype(F32)).astype(BF16)
    logits = _dot_nt(w_hi, h_hi) + _dot_nt(w_hi, h_lo) + _dot_nt(w_lo, h_hi) + br_ref[:, 0:1]
    sub = lax.broadcasted_iota(jnp.int32, (ne, tm), 0)
    l1 = jnp.max(logits, axis=0, keepdims=True)
    i1 = jnp.min(jnp.where(logits == l1, sub, ne), axis=0, keepdims=True)
    rest = jnp.where(sub == i1, -jnp.inf, logits)
    l2 = jnp.max(rest, axis=0, keepdims=True)
    i2 = jnp.min(jnp.where(rest == l2, sub, ne), axis=0, keepdims=True)
    e2 = jnp.exp(l2 - l1)
    g1 = 1.0 / (1.0 + e2)
    g2 = e2 / (1.0 + e2)
    sel = jnp.where((sub == i1) | (sub == i2), 1.0, 0.0)
    before = (lax.broadcasted_iota(jnp.int32, (tm, tm), 0) < lax.broadcasted_iota(jnp.int32, (tm, tm), 1))
    prefix = _dot(sel.astype(BF16), jnp.where(before, 1.0, 0.0).astype(BF16))
    lane = lax.broadcasted_iota(jnp.int32, (ne, LANES), 1)
    subl = lax.broadcasted_iota(jnp.int32, (ne, LANES), 0)
    n_e = (jnp.sum(sel, axis=1, keepdims=True) + jnp.zeros((ne, LANES), F32)).astype(jnp.int32)
    n_pad = (n_e + (SUBLANES - 1)) & (-SUBLANES)
    off = jnp.zeros((ne, LANES), jnp.int32)
    for j in range(ne - 1):
        off = off + jnp.where(subl > j, n_pad[j:j + 1, :], 0)
    slot = prefix + off[:, 0:1].astype(F32)
    p1 = jnp.sum(jnp.where(sub == i1, slot, 0.0), axis=0, keepdims=True)
    p2 = jnp.sum(jnp.where(sub == i2, slot, 0.0), axis=0, keepdims=True)
    route_ref[...] = jnp.where(sub == 0, p1, jnp.where(sub == 1, p2, jnp.where(
        sub == 2, g1, jnp.where(sub == 3, g2, 0.0))))
    carry = carry_ref[...]
    runs_ref[...] = jnp.where(lane == 0, off, jnp.where(lane == 1, carry, jnp.where(lane == 2, n_pad, 0)))
    carry_ref[...] = carry + n_pad
    cnt_ref[...] = carry + n_pad


def moe_router(x, g, w_router, b_router, tm):
    t, d = x.shape
    ne = N_EXPERTS
    wrt = w_router.astype(F32).T
    br = jnp.broadcast_to(b_router.astype(F32)[:, None], (ne, LANES))
    return pl.pallas_call(
        functools.partial(_router_kernel, tm=tm),
        grid=(t // tm,),
        in_specs=[pl.BlockSpec((tm, d), lambda i: (i, 0)),
                  pl.BlockSpec((1, d), lambda i: (0, 0)),
                  pl.BlockSpec((ne, d), lambda i: (0, 0)),
                  pl.BlockSpec((ne, LANES), lambda i: (0, 0))],
        out_specs=[pl.BlockSpec((tm, d), lambda i: (i, 0)),
                   pl.BlockSpec((ne, tm), lambda i: (0, i)),
                   pl.BlockSpec((ne, LANES), lambda i: (i, 0)),
                   pl.BlockSpec((ne, LANES), lambda i: (0, 0))],
        out_shape=[jax.ShapeDtypeStruct((t, d), BF16),
                   jax.ShapeDtypeStruct((ne, t), F32),
                   jax.ShapeDtypeStruct((t // tm * ne, LANES), jnp.int32),
                   jax.ShapeDtypeStruct((ne, LANES), jnp.int32)],
        scratch_shapes=[pltpu.VMEM((ne, LANES), jnp.int32)],
        compiler_params=_params("arbitrary"),
        name="moe_router",
    )(x, g.reshape(1, d), wrt, br)


def _pieces(length, lo_bit, hi_bit):
    for b in range(lo_bit, hi_bit):
        n = 1 << b
        yield n, length & (n - 1), ((length >> b) & 1) == 1


def _run_copies(tab_ref, tile, tile_ref, buf_ref, sem, tm, to_buf, wait):
    for e in range(N_EXPERTS):
        base = (tile * N_EXPERTS + e) * 3
        off, dst, rows = tab_ref[base], tab_ref[base + 1], tab_ref[base + 2]
        for n, lo, present in _pieces(rows, SUBLANES.bit_length() - 1, tm.bit_length()):
            in_tile = tile_ref.at[pl.ds(pl.multiple_of(off + lo, SUBLANES), n), :]
            in_buf = buf_ref.at[pl.ds(pl.multiple_of(dst + lo, SUBLANES), n), :]
            cp = pltpu.make_async_copy(in_tile, in_buf, sem) if to_buf else pltpu.make_async_copy(in_buf, in_tile, sem)

            @pl.when(present)
            def _():
                if wait:
                    cp.wait()
                else:
                    cp.start()


def _dispatch_kernel(tab_ref, meta_ref, route_ref, h_ref, xb_ref, srt_ref, zero_ref, sem, zsem, *, tm, n_tiles):
    i = pl.program_id(0)
    nt = pl.num_programs(0)
    slot = i % 2
    rs = srt_ref.shape[1]

    def runs(tile, slot, wait):
        _run_copies(tab_ref, tile, srt_ref.at[slot], xb_ref, sem.at[slot], tm, True, wait)

    @pl.when(i >= 2)
    def _():
        runs(i - 2, slot, True)

    pos = route_ref[0:2, :].astype(jnp.int32)
    row = lax.broadcasted_iota(jnp.int32, (rs, tm), 0)
    onehot = jnp.where((row == pos[0:1, :]) | (row == pos[1:2, :]), 1.0, 0.0).astype(BF16)
    srt_ref[slot] = _dot(onehot, h_ref[...])
    runs(i, slot, False)

    @pl.when(i == nt - 1)
    def _():
        zero_ref[...] = jnp.zeros_like(zero_ref)
        zrows = zero_ref.shape[0]
        per = EXPERT_ROWS // zrows
        first_unused = meta_ref[2 * N_EXPERTS] * per
        for wait in (False, True):
            def go(cp, cond):
                @pl.when(cond)
                def _():
                    if wait:
                        cp.wait()
                    else:
                        cp.start()

            for e in range(N_EXPERTS):
                c = meta_ref[e]
                first = meta_ref[N_EXPERTS + e]
                tail = (-c) & (EXPERT_ROWS - 1)
                for n, lo, present in _pieces(tail, SUBLANES.bit_length() - 1, EXPERT_ROWS.bit_length() - 1):
                    row0 = pl.multiple_of(first + c + lo, SUBLANES)
                    go(pltpu.make_async_copy(zero_ref.at[pl.ds(0, n), :], xb_ref.at[pl.ds(row0, n), :], zsem), present)
            for j in range(N_EXPERTS * per):
                row0 = pl.multiple_of(jnp.minimum(first_unused + j, n_tiles * per - 1) * zrows, zrows)
                go(pltpu.make_async_copy(zero_ref, xb_ref.at[pl.ds(row0, zrows), :], zsem),
                   first_unused + j < n_tiles * per)

        @pl.when(i >= 1)
        def _():
            runs(i - 1, 1 - slot, True)

        runs(i, slot, True)


def moe_dispatch(tab, meta, route, h, tm, n_tiles):
    t, d = h.shape
    ne = route.shape[0]
    rs = TOP_K * tm + LANES
    return pl.pallas_call(
        functools.partial(_dispatch_kernel, tm=tm, n_tiles=n_tiles),
        grid=(t // tm,),
        in_specs=[pl.BlockSpec(memory_space=pltpu.SMEM),
                  pl.BlockSpec(memory_space=pltpu.SMEM),
                  pl.BlockSpec((ne, tm), lambda i: (0, i)),
                  pl.BlockSpec((tm, d), lambda i: (i, 0))],
        out_specs=pl.BlockSpec(memory_space=pl.ANY),
        out_shape=jax.ShapeDtypeStruct((n_tiles * EXPERT_ROWS, d), F32),
        scratch_shapes=[pltpu.VMEM((2, rs, d), F32),
                        pltpu.VMEM((EXPERT_ROWS // 2, d), F32),
                        pltpu.SemaphoreType.DMA((2,)), pltpu.SemaphoreType.DMA(())],
        compiler_params=_params("arbitrary"),
        name="moe_dispatch",
    )(tab, meta, route, h)


def _expert_kernel(te_ref, nr_ref, xb_ref, wg_ref, wu_ref, wd_ref, o_ref, acc_ref):
    c = pl.program_id(0)
    f = pl.program_id(1)
    rows = nr_ref[c]
    r, rc = EXPERT_ROWS, EXPERT_ROW_CHUNK

    @pl.when(f == 0)
    def _():
        acc_ref[...] = jnp.zeros_like(acc_ref)

    @pl.when(rows == r)
    def _():
        _swiglu_accumulate(xb_ref[...].astype(BF16), wg_ref[...].astype(BF16), wu_ref[...].astype(BF16),
                           wd_ref[...].astype(BF16), acc_ref)

    @pl.when(jnp.logical_and(rows > 0, rows < r))
    def _():
        wg, wu, wd = wg_ref[...].astype(BF16), wu_ref[...].astype(BF16), wd_ref[...].astype(BF16)
        for j in range(r // rc):
            @pl.when(j * rc < rows)
            def _():
                _swiglu_accumulate(xb_ref[j * rc:(j + 1) * rc, :].astype(BF16), wg, wu, wd,
                                   acc_ref.at[j * rc:(j + 1) * rc, :])

    @pl.when(f == pl.num_programs(1) - 1)
    def _():
        o_ref[...] = acc_ref[...]


def expert_ffn(tile_expert, tile_rows, xb, wg, wu, wd, tf):
    n_rows, d = xb.shape
    ff = wg.shape[2]
    r = EXPERT_ROWS
    nf = ff // tf

    def wcol(c, f, te, nr):
        return (te[c], 0, jnp.where(nr[c] > 0, f, nf - 1))

    def wrow(c, f, te, nr):
        return (te[c], jnp.where(nr[c] > 0, f, nf - 1), 0)

    grid_spec = pltpu.PrefetchScalarGridSpec(
        num_scalar_prefetch=2,
        grid=(n_rows // r, nf),
        in_specs=[pl.BlockSpec((r, d), lambda c, f, te, nr: (c, 0)),
                  pl.BlockSpec((None, d, tf), wcol),
                  pl.BlockSpec((None, d, tf), wcol),
                  pl.BlockSpec((None, tf, d), wrow)],
        out_specs=pl.BlockSpec((r, d), lambda c, f, te, nr: (c, 0)),
        scratch_shapes=[pltpu.VMEM((r, d), F32)],
    )
    return pl.pallas_call(
        _expert_kernel,
        grid_spec=grid_spec,
        out_shape=jax.ShapeDtypeStruct((n_rows, d), F32),
        compiler_params=_params("parallel", "arbitrary"),
        name="expert_ffn",
    )(tile_expert, tile_rows, xb, wg, wu, wd)


def _combine_kernel(tab_ref, route_ref, x_ref, g_ref, yb_ref, o_ref, srt_ref, sem, *, tm):
    i = pl.program_id(0)
    nt = pl.num_programs(0)
    slot = i % 2
    rs = srt_ref.shape[1]

    def runs(tile, slot, wait):
        _run_copies(tab_ref, tile, srt_ref.at[slot], yb_ref, sem.at[slot], tm, False, wait)

    @pl.when(i == 0)
    def _():
        srt_ref[...] = jnp.zeros_like(srt_ref)
        runs(0, 0, False)

    @pl.when(i + 1 < nt)
    def _():
        runs(i + 1, 1 - slot, False)

    runs(i, slot, True)
    aux = route_ref[...]
    aux_t = jnp.concatenate([aux, jnp.zeros((LANES - aux.shape[0], tm), F32)], axis=0).T
    srt = srt_ref[slot].astype(BF16)
    col = lax.broadcasted_iota(jnp.int32, (tm, rs), 1)
    ys = []
    for k in range(TOP_K):
        onehot = jnp.where(col == aux_t[:, k:k + 1].astype(jnp.int32), 1.0, 0.0).astype(BF16)
        ys.append(aux_t[:, TOP_K + k:TOP_K + k + 1] * _dot(onehot, srt))
    o_ref[...] = _rms(x_ref[...] + (ys[0] + ys[1]), g_ref[...])


def combine_final(tab, route, x, g, yb, tm):
    t, d = x.shape
    ne = route.shape[0]
    rs = TOP_K * tm + LANES
    return pl.pallas_call(
        functools.partial(_combine_kernel, tm=tm),
        grid=(t // tm,),
        in_specs=[pl.BlockSpec(memory_space=pltpu.SMEM),
                  pl.BlockSpec((ne, tm), lambda i: (0, i)),
                  pl.BlockSpec((tm, d), lambda i: (i, 0)),
                  pl.BlockSpec((1, d), lambda i: (0, 0)),
                  pl.BlockSpec(memory_space=pl.ANY)],
        out_specs=pl.BlockSpec((tm, d), lambda i: (i, 0)),
        out_shape=jax.ShapeDtypeStruct((t, d), F32),
        scratch_shapes=[pltpu.VMEM((2, rs, d), F32), pltpu.SemaphoreType.DMA((2,))],
        compiler_params=_params("arbitrary"),
        name="combine_final",
    )(tab, route, x, g.reshape(1, d), yb)


def _pick(n, prefs):
    for p in prefs:
        if n % p == 0:
            return p
    return n


def kernel(x, mem, g_mix, g_mem, w_in_moba, w_in_ssd, w_mem_kv, w_out, rel_bias, conv_w, conv_b, dt_bias, a_log, d_skip, g_ssd_out, g_ffn, w_ffn_gate, w_ffn_up, w_ffn_down, w_router, b_router, w_exp_gate, w_exp_up, w_exp_down, g_final):
    bsz, s, d = x.shape
    t = bsz * s
    m_len = mem.shape[1]
    tm = _pick(t, (1024, 512, 256))
    xf = x.reshape(t, d).astype(F32)
    memf = mem.reshape(bsz * m_len, d).astype(F32)
    tmm = _pick(bsz * m_len, (1024, 512, 256))

    tmn = _pick(t, (512, 256))
    proj0 = norm_matmul(xf, g_mix[0], w_in_moba[0].astype(BF16), BF16, tmn, 1280)
    kv0 = norm_matmul(memf, g_mem[0], w_mem_kv[0].astype(BF16), BF16, tmm, 1024)
    proj0 = proj0.reshape(bsz, s, -1)
    y_tok = moba_attention(proj0, _moba_bias_tables(rel_bias), bsz, s)
    y_mem = memory_attention(proj0, 3 * MOBA_HEADS, kv0.reshape(bsz, m_len, -1), bsz, s)
    wo = w_out[0].astype(BF16)
    x1 = out_projection(xf, y_tok.reshape(t, -1), y_mem.reshape(t, -1), wo[:MOBA_WIDTH], wo[MOBA_WIDTH:], tm)
    x1 = dense_ffn(x1, g_ffn[0], w_ffn_gate[0].astype(BF16), w_ffn_up[0].astype(BF16),
                   w_ffn_down[0].astype(BF16), tmn, 1792)

    n_zx = SSD_INNER + SSD_INNER + 2 * SSD_GROUPS * SSD_STATE
    w1 = w_in_ssd[0]
    w_main = jnp.concatenate([w1[:, :n_zx], w1[:, n_zx + SSD_HEADS:]], axis=1).astype(BF16)
    w_dt = jnp.pad(w1[:, n_zx:n_zx + SSD_HEADS].reshape(d, SSD_GROUPS, SSD_GHEADS),
                   ((0, 0), (0, 0), (0, LANES - SSD_GHEADS))).reshape(d, SSD_GROUPS * LANES).astype(BF16)
    proj1 = norm_matmul(x1, g_mix[1], w_main, BF16, tmn, 1536).reshape(bsz, s, -1)
    dt_all = norm_matmul(x1, g_mix[1], w_dt, F32, tmn, SSD_GROUPS * LANES).reshape(bsz, s, -1)
    kv1 = norm_matmul(memf, g_mem[1], w_mem_kv[1].astype(BF16), BF16, tmm, 1024)
    y_tok = ssd_mixer(proj1, dt_all, *_ssd_group_params(conv_w[0], conv_b[0], dt_bias[0], a_log[0],
                                                         d_skip[0], g_ssd_out[0]), bsz, s)
    y_mem = memory_attention(proj1, n_zx // HD, kv1.reshape(bsz, m_len, -1), bsz, s)
    wo = w_out[1].astype(BF16)
    x2 = out_projection(x1, y_tok.reshape(t, -1), y_mem.reshape(t, -1), wo[:SSD_INNER], wo[SSD_INNER:], tm)

    tmr = _pick(t, (512, 256))
    nt = t // tmr
    h, route, runs, counts = moe_router(x2, g_ffn[1], w_router[0], b_router[0], tmr)
    r = EXPERT_ROWS
    n_tiles = -(-(t * TOP_K + nt * N_EXPERTS * (SUBLANES - 1)) // r) + N_EXPERTS
    cnt = counts[:, 0]
    tiles_per = (cnt + r - 1) // r
    ends = jnp.cumsum(tiles_per)
    first_row = (ends - tiles_per) * r
    n_used = ends[-1].astype(jnp.int32)
    step = jnp.minimum(jnp.arange(n_tiles, dtype=jnp.int32), n_used - 1)
    tile_expert = jnp.sum(step[:, None] >= ends[None, :], axis=1).astype(jnp.int32)
    tile_rows = jnp.sum(jnp.where(tile_expert[:, None] == jnp.arange(N_EXPERTS)[None, :],
                                  (cnt + first_row)[None, :], 0), axis=1) - step * r
    rc = EXPERT_ROW_CHUNK
    tile_rows = jnp.where(jnp.arange(n_tiles) < n_used, jnp.clip((tile_rows + rc - 1) // rc * rc, 0, r), 0)
    tab = runs.reshape(nt, N_EXPERTS, LANES)[:, :, :3]
    tab = tab.at[:, :, 1].add(first_row[None, :]).reshape(-1).astype(jnp.int32)
    meta = jnp.concatenate([cnt, first_row, n_used.reshape(1)]).astype(jnp.int32)
    xb = moe_dispatch(tab, meta, route, h, tmr, n_tiles)
    yb = expert_ffn(tile_expert, tile_rows.astype(jnp.int32), xb, w_exp_gate[0], w_exp_up[0], w_exp_down[0], 512)
    out = combine_final(tab, route, x2, g_final, yb, tmr)
    return out.reshape(bsz, s, d).astype(x.dtype)
```

```python
import functools
import math

import jax
import jax.numpy as jnp
from jax import lax
from jax.experimental import pallas as pl
from jax.experimental.pallas import tpu as pltpu

F32 = jnp.float32
BF16 = jnp.bfloat16

D_MODEL = 1024
HD = 128
MOBA_HEADS = 12
MOBA_WIDTH = MOBA_HEADS * HD
MOBA_BLOCK = 256
MOBA_TOPK = 3
MOBA_HEADS_PER_STEP = 2
REL_BUCKETS = 32
REL_MAX_DIST = 128
MEM_HEADS = 4
MEM_WIDTH = MEM_HEADS * HD
SSD_HEADS = 24
SSD_HD = 64
SSD_INNER = SSD_HEADS * SSD_HD
SSD_GROUPS = 4
SSD_STATE = 128
SSD_CONV = 4
SSD_CHUNK = 128
SSD_GROUPS_PER_STEP = 2
SSD_GHEADS = SSD_HEADS // SSD_GROUPS
SSD_GW = SSD_GHEADS * SSD_HD
SSD_CONV_GW = SSD_GW + 2 * SSD_STATE
D_FF = 3584
N_EXPERTS = 8
TOP_K = 2
EPS = 1e-6

LOG2E = math.log2(math.e)
LANES = 128
SUBLANES = 8
VMEM_LIMIT = 48 * 1024 * 1024
EXPERT_ROWS = 1024
EXPERT_ROW_CHUNK = 256


def _dot(a, b):
    return jnp.dot(a, b, preferred_element_type=F32)


def _dot_nt(a, b):
    return lax.dot_general(a, b, (((1,), (1,)), ((), ())), preferred_element_type=F32)


def _split(x, pieces):
    out = []
    for _ in range(pieces):
        p = x.astype(BF16)
        out.append(p)
        x = x - p.astype(F32)
    return out


def _dot_split_l(x, w, pieces):
    return sum(_dot(p, w) for p in _split(x, pieces))


def _dot_split_r(w, x, pieces):
    return sum(_dot(w, p) for p in _split(x, pieces))


def _sigmoid(x):
    return 1.0 / (1.0 + jnp.exp(-x))


def _rms(x, g):
    ms = jnp.mean(x * x, axis=-1, keepdims=True)
    return x * lax.rsqrt(ms + EPS) * g


def _rms_rows_to(h_ref, x_ref, g_ref, rows):
    step = min(rows, 256)
    for r in range(0, rows, step):
        h_ref[r:r + step, :] = _rms(x_ref[r:r + step, :], g_ref[...]).astype(h_ref.dtype)


def _params(*sem):
    return pltpu.CompilerParams(dimension_semantics=sem, vmem_limit_bytes=VMEM_LIMIT)


def _normmm_kernel(x_ref, g_ref, *refs, rc, tns):
    w_refs, o_refs = refs[:len(tns)], refs[len(tns):]
    tm = x_ref.shape[0]

    def norm(r):
        return _rms(x_ref[r * rc:(r + 1) * rc, :], g_ref[...]).astype(BF16)

    def matmuls(r, h):
        for w_ref, o_ref, tn in zip(w_refs, o_refs, tns):
            for j in range(w_ref.shape[1] // tn):
                cols = slice(j * tn, (j + 1) * tn)
                o_ref[r * rc:(r + 1) * rc, cols] = _dot(h, w_ref[:, cols]).astype(o_ref.dtype)

    live = {}
    for step in range(tm // rc + 1):
        if step < tm // rc:
            live[step] = norm(step)
        if step >= 1:
            matmuls(step - 1, live.pop(step - 1))


def norm_matmul(x, g, ws, out_dtypes, tm, tns):
    t, d = x.shape
    return pl.pallas_call(
        functools.partial(_normmm_kernel, rc=min(tm, 256), tns=tuple(tns)),
        grid=(t // tm,),
        in_specs=[pl.BlockSpec((tm, d), lambda i: (i, 0)),
                  pl.BlockSpec((1, d), lambda i: (0, 0))]
        + [pl.BlockSpec(w.shape, lambda i: (0, 0)) for w in ws],
        out_specs=[pl.BlockSpec((tm, w.shape[1]), lambda i: (i, 0)) for w in ws],
        out_shape=[jax.ShapeDtypeStruct((t, w.shape[1]), dt) for w, dt in zip(ws, out_dtypes)],
        compiler_params=_params("parallel"),
        name="norm_matmul",
    )(x, g.reshape(1, d), *ws)


def _moba_kernel(q_ref, k_ref, v_ref, tab_ref, o_ref, vt_ref, km_ref, *, nb, n_sel, heads):
    for hh in range(heads):
        lanes = pl.ds(hh * HD, HD)
        _moba_head(q_ref.at[:, lanes], k_ref.at[:, lanes], v_ref.at[:, lanes], tab_ref.at[hh],
                   o_ref.at[:, lanes], vt_ref.at[hh], km_ref.at[hh], nb=nb, n_sel=n_sel)


def _moba_head(q_ref, k_ref, v_ref, tab_ref, o_ref, vt_ref, km_ref, *, nb, n_sel):
    blk = MOBA_BLOCK
    scale = HD ** -0.5 * LOG2E
    nbp = km_ref.shape[0]
    km_ref[...] = jnp.zeros_like(km_ref)
    for j in range(nb):
        rows = slice(j * blk, (j + 1) * blk)
        vt_ref[:, rows] = v_ref[rows, :].astype(F32).T.astype(BF16)
        km_ref[j:j + 1, :] = jnp.mean(k_ref[rows, :].astype(F32), axis=0, keepdims=True)
    km = km_ref[...]
    km_hi = km.astype(BF16)
    km_lo = (km - km_hi.astype(F32)).astype(BF16)
    q_all = q_ref[...]
    gate = _dot_nt(km_hi, q_all) + _dot_nt(km_lo, q_all)
    sub = lax.broadcasted_iota(jnp.int32, (nbp, blk), 0)
    bias_far = tab_ref[2, 0:1, :]

    def scores(i):
        return _dot_nt(k_ref[0:(i + 1) * blk, :], q_ref[i * blk:(i + 1) * blk, :])

    def softmax(i, s_all):
        rows = slice(i * blk, (i + 1) * blk)
        if i > 0:
            valid = sub < i
            gm = jnp.where(valid, gate[:, rows], -jnp.inf)
            rank = jnp.zeros((nbp, blk), F32)
            for jp in range(i):
                row = gm[jp:jp + 1, :]
                beats = (row > gm) | ((row == gm) & (sub > jp))
                rank = rank + jnp.where(beats, 1.0, 0.0)
            selm = jnp.where(valid & (rank < n_sel), 0.0, -jnp.inf)
        bands = []
        for j in range(i + 1):
            sj = s_all[j * blk:(j + 1) * blk, :] * scale
            if j == i:
                sj = sj + tab_ref[0]
            elif j == i - 1:
                sj = sj + (tab_ref[1] + selm[j:j + 1, :])
            else:
                sj = sj + (bias_far + selm[j:j + 1, :])
            bands.append(sj)
        s = jnp.concatenate(bands, axis=0) if i > 0 else bands[0]
        m = jnp.max(s, axis=0, keepdims=True)
        return jnp.exp2(s - m).astype(BF16)

    def output(i, p):
        nk = (i + 1) * blk
        l = _dot(jnp.ones((SUBLANES, nk), BF16), p)[0:1, :]
        acc = _dot(vt_ref[:, 0:nk], p)
        o_ref[i * blk:(i + 1) * blk, :] = (acc / l).T.astype(o_ref.dtype)

    s_live, p_live = {}, {}
    units = [sorted({u, nb - 1 - u}) for u in range((nb + 1) // 2)]
    for step in range(len(units) + 2):
        if step < len(units):
            for i in units[step]:
                s_live[i] = scores(i)
        if 1 <= step <= len(units):
            for i in units[step - 1]:
                p_live[i] = softmax(i, s_live.pop(i))
        if step >= 2:
            for i in units[step - 2]:
                output(i, p_live.pop(i))


def _t5_bucket_idx(dist):
    n = jnp.maximum(dist, 0)
    max_exact = REL_BUCKETS // 2
    large = max_exact + (jnp.log(jnp.maximum(n, 1).astype(F32) / max_exact)
                         / math.log(REL_MAX_DIST / max_exact)
                         * (REL_BUCKETS - max_exact)).astype(jnp.int32)
    large = jnp.minimum(large, REL_BUCKETS - 1)
    return jnp.where(n < max_exact, n, large)


def _bias_table_kernel(rb_ref, idx_ref, o_ref):
    h = pl.program_id(0)
    for t in range(3):
        idx = idx_ref[t]
        acc = jnp.full(idx.shape, -jnp.inf, F32)
        for b in range(REL_BUCKETS):
            acc = jnp.where(idx == b, rb_ref[h, b], acc)
        o_ref[t] = acc * LOG2E


def _moba_bias_tables(rel_bias):
    blk = MOBA_BLOCK
    loc = jnp.arange(blk)
    d0 = loc[None, :] - loc[:, None]
    idx = jnp.stack([jnp.where(d0 >= 0, _t5_bucket_idx(d0), -1),
                     _t5_bucket_idx(d0 + blk),
                     _t5_bucket_idx(d0 + 2 * blk)]).astype(jnp.int32)
    return pl.pallas_call(
        _bias_table_kernel,
        grid=(MOBA_HEADS,),
        in_specs=[pl.BlockSpec(memory_space=pltpu.SMEM),
                  pl.BlockSpec((3, blk, blk), lambda h: (0, 0, 0))],
        out_specs=pl.BlockSpec((None, 3, blk, blk), lambda h: (h, 0, 0, 0)),
        out_shape=jax.ShapeDtypeStruct((MOBA_HEADS, 3, blk, blk), F32),
        compiler_params=_params("parallel"),
        name="moba_bias_tables",
    )(rel_bias.T.astype(F32), idx)


def moba_attention(proj, tabs, bsz, s):
    assert s % MOBA_BLOCK == 0
    nb = s // MOBA_BLOCK
    nbp = -(-nb // 8) * 8
    n_sel = min(MOBA_TOPK, nb - 1)
    blk = MOBA_BLOCK

    hps = MOBA_HEADS_PER_STEP
    groups = MOBA_HEADS // hps

    def col(off):
        return pl.BlockSpec((None, s, hps * HD), lambda h, b: (b, 0, off + h))

    return pl.pallas_call(
        functools.partial(_moba_kernel, nb=nb, n_sel=n_sel, heads=hps),
        grid=(groups, bsz),
        in_specs=[col(0), col(groups), col(2 * groups),
                  pl.BlockSpec((hps, 3, blk, blk), lambda h, b: (h, 0, 0, 0))],
        out_specs=pl.BlockSpec((None, s, hps * HD), lambda h, b: (b, 0, h)),
        out_shape=jax.ShapeDtypeStruct((bsz, s, MOBA_WIDTH), BF16),
        scratch_shapes=[pltpu.VMEM((hps, HD, s), BF16),
                        pltpu.VMEM((hps, nbp, HD), F32)],
        compiler_params=_params("parallel", "parallel"),
        name="moba_attention",
    )(proj, proj, proj, tabs)


def _memattn_kernel(q_ref, kv_ref, o_ref, *, n_chunks, qc):
    scale = HD ** -0.5 * LOG2E
    ks = [kv_ref[:, h * HD:(h + 1) * HD] for h in range(MEM_HEADS)]
    vts = [kv_ref[:, MEM_WIDTH + h * HD:MEM_WIDTH + (h + 1) * HD].astype(F32).T.astype(BF16)
           for h in range(MEM_HEADS)]
    ones = jnp.ones((SUBLANES, kv_ref.shape[0]), BF16)

    def scores(h, c):
        return _dot_nt(ks[h], q_ref[c * qc:(c + 1) * qc, h * HD:(h + 1) * HD])

    def softmax(s):
        s = s * scale
        return jnp.exp2(s - jnp.max(s, axis=0, keepdims=True)).astype(BF16)

    def output(h, c, p):
        l = _dot(ones, p)[0:1, :]
        o_ref[c * qc:(c + 1) * qc, h * HD:(h + 1) * HD] = (_dot(vts[h], p) / l).T.astype(o_ref.dtype)

    work = [(h, c) for h in range(MEM_HEADS) for c in range(n_chunks)]
    s_live, p_live = {}, {}
    for step in range(len(work) + 2):
        if step < len(work):
            s_live[step] = scores(*work[step])
        if 1 <= step <= len(work):
            p_live[step - 1] = softmax(s_live.pop(step - 1))
        if step >= 2:
            output(*work[step - 2], p_live.pop(step - 2))


def memory_attention(proj, q_col, kv, bsz, s):
    m_len = kv.shape[1]
    qc = 256
    assert q_col % MEM_WIDTH == 0
    return pl.pallas_call(
        functools.partial(_memattn_kernel, n_chunks=s // qc, qc=qc),
        grid=(bsz,),
        in_specs=[pl.BlockSpec((None, s, MEM_WIDTH), lambda b: (b, 0, q_col // MEM_WIDTH)),
                  pl.BlockSpec((None, m_len, 2 * MEM_WIDTH), lambda b: (b, 0, 0))],
        out_specs=pl.BlockSpec((None, s, MEM_WIDTH), lambda b: (b, 0, 0)),
        out_shape=jax.ShapeDtypeStruct((bsz, s, MEM_WIDTH), BF16),
        compiler_params=_params("parallel"),
        name="memory_attention",
    )(proj, kv)


def _outproj_kernel(x_ref, ya_ref, yb_ref, wa_ref, wb_ref, o_ref):
    o_ref[...] = x_ref[...] + _dot(ya_ref[...], wa_ref[...]) + _dot(yb_ref[...], wb_ref[...])


def out_projection(x, ya, yb, wa, wb, tm):
    t, d = x.shape
    ka, kb = ya.shape[1], yb.shape[1]
    return pl.pallas_call(
        _outproj_kernel,
        grid=(t // tm,),
        in_specs=[pl.BlockSpec((tm, d), lambda i: (i, 0)),
                  pl.BlockSpec((tm, ka), lambda i: (i, 0)),
                  pl.BlockSpec((tm, kb), lambda i: (i, 0)),
                  pl.BlockSpec((ka, d), lambda i: (0, 0)),
                  pl.BlockSpec((kb, d), lambda i: (0, 0))],
        out_specs=pl.BlockSpec((tm, d), lambda i: (i, 0)),
        out_shape=jax.ShapeDtypeStruct((t, d), F32),
        compiler_params=_params("parallel"),
        name="out_projection",
    )(x, ya, yb, wa, wb)


def _swiglu_accumulate(h, wg, wu, wd, acc_ref):
    a = _dot(h, wg)
    u = _dot(h, wu)
    acc_ref[...] += _dot((a * _sigmoid(a) * u).astype(BF16), wd)


def _ffn_kernel(x_ref, g_ref, wg_ref, wu_ref, wd_ref, o_ref, h_ref, acc_ref, *, tm):
    f = pl.program_id(1)

    @pl.when(f == 0)
    def _():
        _rms_rows_to(h_ref, x_ref, g_ref, tm)
        acc_ref[...] = x_ref[...]

    _swiglu_accumulate(h_ref[...], wg_ref[...], wu_ref[...], wd_ref[...], acc_ref)

    @pl.when(f == pl.num_programs(1) - 1)
    def _():
        o_ref[...] = acc_ref[...]


def dense_ffn(x, g, wg, wu, wd, tm, tf):
    t, d = x.shape
    ff = wg.shape[1]
    return pl.pallas_call(
        functools.partial(_ffn_kernel, tm=tm),
        grid=(t // tm, ff // tf),
        in_specs=[pl.BlockSpec((tm, d), lambda i, f: (i, 0)),
                  pl.BlockSpec((1, d), lambda i, f: (0, 0)),
                  pl.BlockSpec((d, tf), lambda i, f: (0, f)),
                  pl.BlockSpec((d, tf), lambda i, f: (0, f)),
                  pl.BlockSpec((tf, d), lambda i, f: (f, 0))],
        out_specs=pl.BlockSpec((tm, d), lambda i, f: (i, 0)),
        out_shape=jax.ShapeDtypeStruct((t, d), F32),
        scratch_shapes=[pltpu.VMEM((tm, d), BF16), pltpu.VMEM((tm, d), F32)],
        compiler_params=_params("parallel", "arbitrary"),
        name="dense_ffn",
    )(x, g.reshape(1, d), wg, wu, wd)


def _ssd_kernel(z_ref, xs_ref, b_ref, c_ref, dt_ref, cw_ref, cb_ref, dtb_ref, alog_ref, dsk_ref,
                gout_ref, exp_ref, o_ref, h_ref, *, nc, unroll, groups):
    gw, n = SSD_GW, SSD_STATE
    chunks = []
    for sg in range(groups):
        wide, narrow, lanes = pl.ds(sg * gw, gw), pl.ds(sg * n, n), pl.ds(sg * LANES, LANES)
        chunks.append(_ssd_group_chunk(
            z_ref.at[:, wide], xs_ref.at[:, wide], b_ref.at[:, narrow], c_ref.at[:, narrow], dt_ref.at[:, lanes],
            cw_ref.at[sg], cb_ref.at[sg], dtb_ref.at[sg], alog_ref.at[sg], dsk_ref.at[sg], gout_ref.at[sg],
            exp_ref, o_ref.at[:, wide], h_ref.at[sg]))

    def body(it, carry):
        work = [(sg, it * unroll + u) for u in range(unroll) for sg in range(groups)]
        live = {}
        for step in range(len(work) + 1):
            if step < len(work):
                sg, c = work[step]
                live[step] = chunks[sg][0](c)
            if step >= 1:
                chunks[work[step - 1][0]][1](*live.pop(step - 1))
        return carry

    lax.fori_loop(0, nc // unroll, body, 0)


def _ssd_group_chunk(z_ref, xs_ref, b_ref, c_ref, dt_ref, cw_ref, cb_ref, dtb_ref, alog_ref, dsk_ref,
                     gout_ref, exp_ref, o_ref, h_ref):
    L = SSD_CHUNK
    gw = SSD_GW
    n = SSD_STATE
    h_ref[...] = jnp.zeros(h_ref.shape, h_ref.dtype)
    a_neg = -jnp.exp(alog_ref[...])
    causal = (lax.broadcasted_iota(jnp.int32, (L, L), 0) >= lax.broadcasted_iota(jnp.int32, (L, L), 1))
    tri = jnp.where(causal, 1.0, 0.0).astype(BF16)
    first_half = lax.broadcasted_iota(jnp.int32, (1, 2 * SSD_HD), 1) < SSD_HD
    expm = exp_ref[...]
    srcs = (xs_ref, b_ref, c_ref)
    sh_r = lax.broadcasted_iota(jnp.int32, (L, 2 * L), 0)
    sh_c = lax.broadcasted_iota(jnp.int32, (L, 2 * L), 1)
    shifts = [jnp.where(sh_c == sh_r + (L - s), 1.0, 0.0).astype(BF16) for s in range(1, SSD_CONV)]

    def local(c):
        r0 = pl.multiple_of(c * L, L)
        rq = pl.multiple_of(jnp.maximum(r0 - L, 0), L)
        cur = jnp.concatenate([ref[pl.ds(r0, L), :] for ref in srcs], axis=1)
        prev = jnp.concatenate([ref[pl.ds(rq, L), :] for ref in srcs], axis=1)
        prev = jnp.where(c > 0, prev, jnp.zeros_like(prev))
        ext = jnp.concatenate([prev, cur], axis=0)
        conv = cb_ref[...] + cw_ref[SSD_CONV - 1:SSD_CONV, :] * cur.astype(F32)
        for s in range(1, SSD_CONV):
            k = SSD_CONV - 1 - s
            conv = conv + cw_ref[k:k + 1, :] * _dot(shifts[s - 1], ext)
        act = conv * _sigmoid(conv)
        xs = act[:, 0:gw]
        bm = act[:, gw:gw + n]
        cm = act[:, gw + n:gw + 2 * n]

        dtr = dt_ref[pl.ds(r0, L), :] + dtb_ref[...]
        dt = jnp.maximum(dtr, 0.0) + jnp.log(1.0 + jnp.exp(-jnp.abs(dtr)))
        la = dt * a_neg
        cs = _dot_split_r(tri, la, 3)
        cs_t = cs.T
        dt_t = dt.T
        bm16 = bm.astype(BF16)
        cm16 = cm.astype(BF16)
        scores = _dot_nt(cm16, bm16)
        from_start = _dot_split_l(jnp.exp(cs), expm, 2)
        w_end = dt * jnp.exp(cs[L - 1:L, :] - cs)
        xs_t = xs.T
        return r0, xs, bm, cm16, cs, cs_t, dt_t, scores, from_start, w_end, xs_t

    def carried(r0, xs, bm, cm16, cs, cs_t, dt_t, scores, from_start, w_end, xs_t):
        y = _dot_nt(cm16, h_ref[...].astype(BF16)) * from_start
        y_in = []
        for e in range(SSD_GHEADS):
            diff = cs[:, e:e + 1] - cs_t[e:e + 1, :]
            dec = jnp.exp(jnp.where(causal, diff, -jnp.inf))
            mm = (scores * dec * dt_t[e:e + 1, :]).astype(BF16)
            if e % 2 == 0:
                mm_even = mm
            else:
                xp = xs[:, (e - 1) * SSD_HD:(e + 1) * SSD_HD]
                rhs = jnp.concatenate([jnp.where(first_half, xp, 0.0), jnp.where(first_half, 0.0, xp)], axis=0)
                y_in.append(_dot(jnp.concatenate([mm_even, mm], axis=1), rhs.astype(BF16)))
            bw = (bm * w_end[:, e:e + 1]).astype(BF16)
            st = _dot(xs_t[e * SSD_HD:(e + 1) * SSD_HD, :].astype(BF16), bw)
            cdec = jnp.exp(cs[L - 1:L, e:e + 1])
            hs = slice(e * SSD_HD, (e + 1) * SSD_HD)
            h_ref[hs, :] = h_ref[hs, :] * cdec + st
        y = y + jnp.concatenate(y_in, axis=1) + xs * dsk_ref[...]
        z = z_ref[pl.ds(r0, L), :].astype(F32)
        u = y * (z * _sigmoid(z))
        ms = jnp.mean(u * u, axis=-1, keepdims=True)
        o_ref[pl.ds(r0, L), :] = (u * lax.rsqrt(ms + EPS) * gout_ref[...]).astype(o_ref.dtype)

    return local, carried


def ssd_mixer(proj, dt_all, cw_g, cb_g, dtb_g, alog_g, dsk_g, gout_g, expand, bsz, s):
    assert s % SSD_CHUNK == 0
    nc = s // SSD_CHUNK
    gps = SSD_GROUPS_PER_STEP
    gw = SSD_GW * gps
    nw = SSD_STATE * gps
    z_blk = 0
    xs_blk = SSD_INNER // gw
    b_blk = 2 * SSD_INNER // nw
    c_blk = b_blk + SSD_GROUPS // gps

    def seq(width, off):
        return pl.BlockSpec((None, s, width), lambda b, g: (b, 0, off + g))

    def par(rows, width):
        return pl.BlockSpec((gps, rows, width), lambda b, g: (g, 0, 0))

    return pl.pallas_call(
        functools.partial(_ssd_kernel, nc=nc, unroll=_pick(nc, (4, 2, 1)), groups=gps),
        grid=(bsz, SSD_GROUPS // gps),
        in_specs=[seq(gw, z_blk), seq(gw, xs_blk), seq(nw, b_blk), seq(nw, c_blk),
                  seq(LANES * gps, 0),
                  par(SSD_CONV, SSD_CONV_GW), par(1, SSD_CONV_GW), par(1, LANES), par(1, LANES),
                  par(1, SSD_GW), par(1, SSD_GW),
                  pl.BlockSpec((LANES, SSD_GW), lambda b, g: (0, 0))],
        out_specs=seq(gw, 0),
        out_shape=jax.ShapeDtypeStruct((bsz, s, SSD_INNER), BF16),
        scratch_shapes=[pltpu.VMEM((gps, SSD_GW, SSD_STATE), F32)],
        compiler_params=_params("parallel", "parallel"),
        name="ssd_mixer",
    )(proj, proj, proj, proj, dt_all, cw_g, cb_g, dtb_g, alog_g, dsk_g, gout_g, expand)


def _ssd_group_params(conv_w, conv_b, dt_bias, a_log, d_skip, g_out):
    g, gh, gw, n = SSD_GROUPS, SSD_GHEADS, SSD_GW, SSD_STATE

    def conv_cols(a):
        xs = a[..., :SSD_INNER].reshape(a.shape[:-1] + (g, gw))
        bb = a[..., SSD_INNER:SSD_INNER + g * n].reshape(a.shape[:-1] + (g, n))
        cc = a[..., SSD_INNER + g * n:].reshape(a.shape[:-1] + (g, n))
        return jnp.moveaxis(jnp.concatenate([xs, bb, cc], axis=-1), -2, 0)

    cw_g = conv_cols(conv_w.astype(F32))
    cb_g = conv_cols(conv_b.astype(F32)[None, :])

    def per_head(a):
        return jnp.pad(a.astype(F32).reshape(g, 1, gh), ((0, 0), (0, 0), (0, LANES - gh)))

    dsk_g = jnp.repeat(d_skip.astype(F32), SSD_HD).reshape(g, 1, gw)
    gout_g = g_out.astype(F32).reshape(g, 1, gw)
    expand = (jnp.arange(LANES)[:, None] == (jnp.arange(gw)[None, :] // SSD_HD)).astype(BF16)
    return cw_g, cb_g, per_head(dt_bias), per_head(a_log), dsk_g, gout_g, expand


def _router_kernel(x_ref, g_ref, wrt_ref, br_ref, h_ref, route_ref, runs_ref, cnt_ref, carry_ref, *, tm):
    @pl.when(pl.program_id(0) == 0)
    def _():
        carry_ref[...] = jnp.zeros_like(carry_ref)

    ne = N_EXPERTS
    hn = _rms(x_ref[...], g_ref[...])
    h_hi = hn.astype(BF16)
    h_ref[...] = h_hi
    h_lo = (hn - h_hi.astype(F32)).astype(BF16)
    wr = wrt_ref[...]
    w_hi = wr.astype(BF16)
    w_lo = (wr - w_hi.astype(F32)).astype(BF16)
    logits = _dot_nt(w_hi, h_hi) + _dot_nt(w_hi, h_lo) + _dot_nt(w_lo, h_hi) + br_ref[:, 0:1]
    sub = lax.broadcasted_iota(jnp.int32, (ne, tm), 0)
    l1 = jnp.max(logits, axis=0, keepdims=True)
    i1 = jnp.min(jnp.where(logits == l1, sub, ne), axis=0, keepdims=True)
    rest = jnp.where(sub == i1, -jnp.inf, logits)
    l2 = jnp.max(rest, axis=0, keepdims=True)
    i2 = jnp.min(jnp.where(rest == l2, sub, ne), axis=0, keepdims=True)
    e2 = jnp.exp(l2 - l1)
    g1 = 1.0 / (1.0 + e2)
    g2 = e2 / (1.0 + e2)
    sel = jnp.where((sub == i1) | (sub == i2), 1.0, 0.0)
    before = (lax.broadcasted_iota(jnp.int32, (tm, tm), 0) < lax.broadcasted_iota(jnp.int32, (tm, tm), 1))
    prefix = _dot(sel.astype(BF16), jnp.where(before, 1.0, 0.0).astype(BF16))
    lane = lax.broadcasted_iota(jnp.int32, (ne, LANES), 1)
    subl = lax.broadcasted_iota(jnp.int32, (ne, LANES), 0)
    n_e = (jnp.sum(sel, axis=1, keepdims=True) + jnp.zeros((ne, LANES), F32)).astype(jnp.int32)
    n_pad = (n_e + (SUBLANES - 1)) & (-SUBLANES)
    off = jnp.zeros((ne, LANES), jnp.int32)
    for j in range(ne - 1):
        off = off + jnp.where(subl > j, n_pad[j:j + 1, :], 0)
    slot = prefix + off[:, 0:1].astype(F32)
    p1 = jnp.sum(jnp.where(sub == i1, slot, 0.0), axis=0, keepdims=True)
    p2 = jnp.sum(jnp.where(sub == i2, slot, 0.0), axis=0, keepdims=True)
    route_ref[...] = jnp.where(sub == 0, p1, jnp.where(sub == 1, p2, jnp.where(
        sub == 2, g1, jnp.where(sub == 3, g2, 0.0))))
    carry = carry_ref[...]
    runs_ref[...] = jnp.where(lane == 0, off, jnp.where(lane == 1, carry, jnp.where(lane == 2, n_pad, 0)))
    carry_ref[...] = carry + n_pad
    cnt_ref[...] = carry + n_pad


def moe_router(x, g, w_router, b_router, tm):
    t, d = x.shape
    ne = N_EXPERTS
    wrt = w_router.astype(F32).T
    br = jnp.broadcast_to(b_router.astype(F32)[:, None], (ne, LANES))
    return pl.pallas_call(
        functools.partial(_router_kernel, tm=tm),
        grid=(t // tm,),
        in_specs=[pl.BlockSpec((tm, d), lambda i: (i, 0)),
                  pl.BlockSpec((1, d), lambda i: (0, 0)),
                  pl.BlockSpec((ne, d), lambda i: (0, 0)),
                  pl.BlockSpec((ne, LANES), lambda i: (0, 0))],
        out_specs=[pl.BlockSpec((tm, d), lambda i: (i, 0)),
                   pl.BlockSpec((ne, tm), lambda i: (0, i)),
                   pl.BlockSpec((ne, LANES), lambda i: (i, 0)),
                   pl.BlockSpec((ne, LANES), lambda i: (0, 0))],
        out_shape=[jax.ShapeDtypeStruct((t, d), BF16),
                   jax.ShapeDtypeStruct((ne, t), F32),
                   jax.ShapeDtypeStruct((t // tm * ne, LANES), jnp.int32),
                   jax.ShapeDtypeStruct((ne, LANES), jnp.int32)],
        scratch_shapes=[pltpu.VMEM((ne, LANES), jnp.int32)],
        compiler_params=_params("arbitrary"),
        name="moe_router",
    )(x, g.reshape(1, d), wrt, br)


def _pieces(length, lo_bit, hi_bit):
    for b in range(lo_bit, hi_bit):
        n = 1 << b
        yield n, length & (n - 1), ((length >> b) & 1) == 1


def _run_copies(tab_ref, tile, tile_ref, buf_ref, sem, tm, to_buf, wait):
    for e in range(N_EXPERTS):
        base = (tile * N_EXPERTS + e) * 3
        off, dst, rows = tab_ref[base], tab_ref[base + 1], tab_ref[base + 2]
        for n, lo, present in _pieces(rows, SUBLANES.bit_length() - 1, tm.bit_length()):
            in_tile = tile_ref.at[pl.ds(pl.multiple_of(off + lo, SUBLANES), n), :]
            in_buf = buf_ref.at[pl.ds(pl.multiple_of(dst + lo, SUBLANES), n), :]
            cp = pltpu.make_async_copy(in_tile, in_buf, sem) if to_buf else pltpu.make_async_copy(in_buf, in_tile, sem)

            @pl.when(present)
            def _():
                if wait:
                    cp.wait()
                else:
                    cp.start()


def _dispatch_kernel(tab_ref, meta_ref, route_ref, h_ref, xb_ref, srt_ref, zero_ref, sem, zsem, *, tm, n_tiles):
    i = pl.program_id(0)
    nt = pl.num_programs(0)
    slot = i % 2
    rs = srt_ref.shape[1]

    def runs(tile, slot, wait):
        _run_copies(tab_ref, tile, srt_ref.at[slot], xb_ref, sem.at[slot], tm, True, wait)

    @pl.when(i >= 2)
    def _():
        runs(i - 2, slot, True)

    pos = route_ref[0:2, :].astype(jnp.int32)
    row = lax.broadcasted_iota(jnp.int32, (rs, tm), 0)
    onehot = jnp.where((row == pos[0:1, :]) | (row == pos[1:2, :]), 1.0, 0.0).astype(BF16)
    srt_ref[slot] = _dot(onehot, h_ref[...])
    runs(i, slot, False)

    @pl.when(i == nt - 1)
    def _():
        zero_ref[...] = jnp.zeros_like(zero_ref)
        zrows = zero_ref.shape[0]
        per = EXPERT_ROWS // zrows
        first_unused = meta_ref[2 * N_EXPERTS] * per
        for wait in (False, True):
            def go(cp, cond):
                @pl.when(cond)
                def _():
                    if wait:
                        cp.wait()
                    else:
                        cp.start()

            for e in range(N_EXPERTS):
                c = meta_ref[e]
                first = meta_ref[N_EXPERTS + e]
                tail = (-c) & (EXPERT_ROWS - 1)
                for n, lo, present in _pieces(tail, SUBLANES.bit_length() - 1, EXPERT_ROWS.bit_length() - 1):
                    row0 = pl.multiple_of(first + c + lo, SUBLANES)
                    go(pltpu.make_async_copy(zero_ref.at[pl.ds(0, n), :], xb_ref.at[pl.ds(row0, n), :], zsem), present)
            for j in range(N_EXPERTS * per):
                row0 = pl.multiple_of(jnp.minimum(first_unused + j, n_tiles * per - 1) * zrows, zrows)
                go(pltpu.make_async_copy(zero_ref, xb_ref.at[pl.ds(row0, zrows), :], zsem),
                   first_unused + j < n_tiles * per)

        @pl.when(i >= 1)
        def _():
            runs(i - 1, 1 - slot, True)

        runs(i, slot, True)


def moe_dispatch(tab, meta, route, h, tm, n_tiles):
    t, d = h.shape
    ne = route.shape[0]
    rs = TOP_K * tm + LANES
    return pl.pallas_call(
        functools.partial(_dispatch_kernel, tm=tm, n_tiles=n_tiles),
        grid=(t // tm,),
        in_specs=[pl.BlockSpec(memory_space=pltpu.SMEM),
                  pl.BlockSpec(memory_space=pltpu.SMEM),
                  pl.BlockSpec((ne, tm), lambda i: (0, i)),
                  pl.BlockSpec((tm, d), lambda i: (i, 0))],
        out_specs=pl.BlockSpec(memory_space=pl.ANY),
        out_shape=jax.ShapeDtypeStruct((n_tiles * EXPERT_ROWS, d), F32),
        scratch_shapes=[pltpu.VMEM((2, rs, d), F32),
                        pltpu.VMEM((EXPERT_ROWS // 2, d), F32),
                        pltpu.SemaphoreType.DMA((2,)), pltpu.SemaphoreType.DMA(())],
        compiler_params=_params("arbitrary"),
        name="moe_dispatch",
    )(tab, meta, route, h)


def _expert_kernel(te_ref, nr_ref, xb_ref, wg_ref, wu_ref, wd_ref, o_ref, acc_ref):
    c = pl.program_id(0)
    f = pl.program_id(1)
    rows = nr_ref[c]
    r, rc = EXPERT_ROWS, EXPERT_ROW_CHUNK

    @pl.when(f == 0)
    def _():
        acc_ref[...] = jnp.zeros_like(acc_ref)

    @pl.when(rows == r)
    def _():
        _swiglu_accumulate(xb_ref[...].astype(BF16), wg_ref[...].astype(BF16), wu_ref[...].astype(BF16),
                           wd_ref[...].astype(BF16), acc_ref)

    @pl.when(jnp.logical_and(rows > 0, rows < r))
    def _():
        wg, wu, wd = wg_ref[...].astype(BF16), wu_ref[...].astype(BF16), wd_ref[...].astype(BF16)
        for j in range(r // rc):
            @pl.when(j * rc < rows)
            def _():
                _swiglu_accumulate(xb_ref[j * rc:(j + 1) * rc, :].astype(BF16), wg, wu, wd,
                                   acc_ref.at[j * rc:(j + 1) * rc, :])

    @pl.when(f == pl.num_programs(1) - 1)
    def _():
        o_ref[...] = acc_ref[...]


def expert_ffn(tile_expert, tile_rows, xb, wg, wu, wd, tf):
    n_rows, d = xb.shape
    ff = wg.shape[2]
    r = EXPERT_ROWS
    nf = ff // tf

    def wcol(c, f, te, nr):
        return (te[c], 0, jnp.where(nr[c] > 0, f, nf - 1))

    def wrow(c, f, te, nr):
        return (te[c], jnp.where(nr[c] > 0, f, nf - 1), 0)

    grid_spec = pltpu.PrefetchScalarGridSpec(
        num_scalar_prefetch=2,
        grid=(n_rows // r, nf),
        in_specs=[pl.BlockSpec((r, d), lambda c, f, te, nr: (c, 0)),
                  pl.BlockSpec((None, d, tf), wcol),
                  pl.BlockSpec((None, d, tf), wcol),
                  pl.BlockSpec((None, tf, d), wrow)],
        out_specs=pl.BlockSpec((r, d), lambda c, f, te, nr: (c, 0)),
        scratch_shapes=[pltpu.VMEM((r, d), F32)],
    )
    return pl.pallas_call(
        _expert_kernel,
        grid_spec=grid_spec,
        out_shape=jax.ShapeDtypeStruct((n_rows, d), F32),
        compiler_params=_params("parallel", "arbitrary"),
        name="expert_ffn",
    )(tile_expert, tile_rows, xb, wg, wu, wd)


def _combine_kernel(tab_ref, route_ref, x_ref, g_ref, yb_ref, o_ref, srt_ref, sem, *, tm):
    i = pl.program_id(0)
    nt = pl.num_programs(0)
    slot = i % 2
    rs = srt_ref.shape[1]

    def runs(tile, slot, wait):
        _run_copies(tab_ref, tile, srt_ref.at[slot], yb_ref, sem.at[slot], tm, False, wait)

    @pl.when(i == 0)
    def _():
        srt_ref[...] = jnp.zeros_like(srt_ref)
        runs(0, 0, False)

    @pl.when(i + 1 < nt)
    def _():
        runs(i + 1, 1 - slot, False)

    runs(i, slot, True)
    aux = route_ref[...]
    aux_t = jnp.concatenate([aux, jnp.zeros((LANES - aux.shape[0], tm), F32)], axis=0).T
    srt = srt_ref[slot].astype(BF16)
    col = lax.broadcasted_iota(jnp.int32, (tm, rs), 1)
    ys = []
    for k in range(TOP_K):
        onehot = jnp.where(col == aux_t[:, k:k + 1].astype(jnp.int32), 1.0, 0.0).astype(BF16)
        ys.append(aux_t[:, TOP_K + k:TOP_K + k + 1] * _dot(onehot, srt))
    o_ref[...] = _rms(x_ref[...] + (ys[0] + ys[1]), g_ref[...])


def combine_final(tab, route, x, g, yb, tm):
    t, d = x.shape
    ne = route.shape[0]
    rs = TOP_K * tm + LANES
    return pl.pallas_call(
        functools.partial(_combine_kernel, tm=tm),
        grid=(t // tm,),
        in_specs=[pl.BlockSpec(memory_space=pltpu.SMEM),
                  pl.BlockSpec((ne, tm), lambda i: (0, i)),
                  pl.BlockSpec((tm, d), lambda i: (i, 0)),
                  pl.BlockSpec((1, d), lambda i: (0, 0)),
                  pl.BlockSpec(memory_space=pl.ANY)],
        out_specs=pl.BlockSpec((tm, d), lambda i: (i, 0)),
        out_shape=jax.ShapeDtypeStruct((t, d), F32),
        scratch_shapes=[pltpu.VMEM((2, rs, d), F32), pltpu.SemaphoreType.DMA((2,))],
        compiler_params=_params("arbitrary"),
        name="combine_final",
    )(tab, route, x, g.reshape(1, d), yb)


def _pick(n, prefs):
    for p in prefs:
        if n % p == 0:
            return p
    return n


def kernel(x, mem, g_mix, g_mem, w_in_moba, w_in_ssd, w_mem_kv, w_out, rel_bias, conv_w, conv_b, dt_bias, a_log, d_skip, g_ssd_out, g_ffn, w_ffn_gate, w_ffn_up, w_ffn_down, w_router, b_router, w_exp_gate, w_exp_up, w_exp_down, g_final):
    bsz, s, d = x.shape
    t = bsz * s
    m_len = mem.shape[1]
    tm = _pick(t, (1024, 512, 256))
    xf = x.reshape(t, d).astype(F32)
    memf = mem.reshape(bsz * m_len, d).astype(F32)
    tmm = _pick(bsz * m_len, (1024, 512, 256))

    tmn = _pick(t, (512, 256))
    proj0, = norm_matmul(xf, g_mix[0], [w_in_moba[0].astype(BF16)], [BF16], tmn, [1280])
    kv0, = norm_matmul(memf, g_mem[0], [w_mem_kv[0].astype(BF16)], [BF16], tmm, [1024])
    proj0 = proj0.reshape(bsz, s, -1)
    y_tok = moba_attention(proj0, _moba_bias_tables(rel_bias), bsz, s)
    y_mem = memory_attention(proj0, 3 * MOBA_WIDTH, kv0.reshape(bsz, m_len, -1), bsz, s)
    wo = w_out[0].astype(BF16)
    x1 = out_projection(xf, y_tok.reshape(t, -1), y_mem.reshape(t, -1), wo[:MOBA_WIDTH], wo[MOBA_WIDTH:], tm)
    x1 = dense_ffn(x1, g_ffn[0], w_ffn_gate[0].astype(BF16), w_ffn_up[0].astype(BF16),
                   w_ffn_down[0].astype(BF16), tmn, 1792)

    n_zx = SSD_INNER + SSD_INNER + 2 * SSD_GROUPS * SSD_STATE
    w1 = w_in_ssd[0]
    w_main = jnp.concatenate([w1[:, :n_zx], w1[:, n_zx + SSD_HEADS:]], axis=1).astype(BF16)
    w_dt = jnp.pad(w1[:, n_zx:n_zx + SSD_HEADS].reshape(d, SSD_GROUPS, SSD_GHEADS),
                   ((0, 0), (0, 0), (0, LANES - SSD_GHEADS))).reshape(d, SSD_GROUPS * LANES).astype(BF16)
    proj1, dt_all = norm_matmul(x1, g_mix[1], [w_main, w_dt], [BF16, F32], tmn, [1536, SSD_GROUPS * LANES])
    proj1, dt_all = proj1.reshape(bsz, s, -1), dt_all.reshape(bsz, s, -1)
    kv1, = norm_matmul(memf, g_mem[1], [w_mem_kv[1].astype(BF16)], [BF16], tmm, [1024])
    y_tok = ssd_mixer(proj1, dt_all, *_ssd_group_params(conv_w[0], conv_b[0], dt_bias[0], a_log[0],
                                                         d_skip[0], g_ssd_out[0]), bsz, s)
    y_mem = memory_attention(proj1, n_zx, kv1.reshape(bsz, m_len, -1), bsz, s)
    wo = w_out[1].astype(BF16)
    x2 = out_projection(x1, y_tok.reshape(t, -1), y_mem.reshape(t, -1), wo[:SSD_INNER], wo[SSD_INNER:], tm)

    tmr = _pick(t, (512, 256))
    nt = t // tmr
    h, route, runs, counts = moe_router(x2, g_ffn[1], w_router[0], b_router[0], tmr)
    r = EXPERT_ROWS
    n_tiles = -(-(t * TOP_K + nt * N_EXPERTS * (SUBLANES - 1)) // r) + N_EXPERTS
    cnt = counts[:, 0]
    tiles_per = (cnt + r - 1) // r
    ends = jnp.cumsum(tiles_per)
    first_row = (ends - tiles_per) * r
    n_used = ends[-1].astype(jnp.int32)
    step = jnp.minimum(jnp.arange(n_tiles, dtype=jnp.int32), n_used - 1)
    tile_expert = jnp.sum(step[:, None] >= ends[None, :], axis=1).astype(jnp.int32)
    tile_rows = jnp.sum(jnp.where(tile_expert[:, None] == jnp.arange(N_EXPERTS)[None, :],
                                  (cnt + first_row)[None, :], 0), axis=1) - step * r
    rc = EXPERT_ROW_CHUNK
    tile_rows = jnp.where(jnp.arange(n_tiles) < n_used, jnp.clip((tile_rows + rc - 1) // rc * rc, 0, r), 0)
    tab = runs.reshape(nt, N_EXPERTS, LANES)[:, :, :3]
    tab = tab.at[:, :, 1].add(first_row[None, :]).reshape(-1).astype(jnp.int32)
    meta = jnp.concatenate([cnt, first_row, n_used.reshape(1)]).astype(jnp.int32)
    xb = moe_dispatch(tab, meta, route, h, tmr, n_tiles)
    yb = expert_ffn(tile_expert, tile_rows.astype(jnp.int32), xb, w_exp_gate[0], w_exp_up[0], w_exp_down[0], 512)
    out = combine_final(tab, route, x2, g_final, yb, tmr)
    return out.reshape(bsz, s, d).astype(x.dtype)
```

```python
import functools
import math
from typing import Any, NamedTuple

import jax
import jax.numpy as jnp
from jax import lax
from jax.experimental import pallas as pl
from jax.experimental.pallas import tpu as pltpu

F32 = jnp.float32
BF16 = jnp.bfloat16

D_MODEL = 1024
HD = 128
MOBA_HEADS = 12
MOBA_WIDTH = MOBA_HEADS * HD
MOBA_BLOCK = 256
MOBA_TOPK = 3
MOBA_HEADS_PER_STEP = 2
REL_BUCKETS = 32
REL_MAX_DIST = 128
MEM_HEADS = 4
MEM_WIDTH = MEM_HEADS * HD
SSD_HEADS = 24
SSD_HD = 64
SSD_INNER = SSD_HEADS * SSD_HD
SSD_GROUPS = 4
SSD_STATE = 128
SSD_CONV = 4
SSD_CHUNK = 128
SSD_GROUPS_PER_STEP = 2
SSD_GHEADS = SSD_HEADS // SSD_GROUPS
SSD_GW = SSD_GHEADS * SSD_HD
SSD_CONV_GW = SSD_GW + 2 * SSD_STATE
D_FF = 3584
N_EXPERTS = 8
TOP_K = 2
EPS = 1e-6

LOG2E = math.log2(math.e)
LANES = 128
SUBLANES = 8
VMEM_LIMIT = 48 * 1024 * 1024
EXPERT_ROWS = 1024
EXPERT_ROW_CHUNK = 256


def _dot(a, b):
    return jnp.dot(a, b, preferred_element_type=F32)


def _dot_nt(a, b):
    return lax.dot_general(a, b, (((1,), (1,)), ((), ())), preferred_element_type=F32)


def _split(x, pieces):
    out = []
    for _ in range(pieces):
        p = x.astype(BF16)
        out.append(p)
        x = x - p.astype(F32)
    return out


def _dot_split_l(x, w, pieces):
    return sum(_dot(p, w) for p in _split(x, pieces))


def _dot_split_r(w, x, pieces):
    return sum(_dot(w, p) for p in _split(x, pieces))


def _sigmoid(x):
    return 1.0 / (1.0 + jnp.exp(-x))


def _rms(x, g):
    ms = jnp.mean(x * x, axis=-1, keepdims=True)
    return x * lax.rsqrt(ms + EPS) * g


def _rms_rows_to(h_ref, x_ref, g_ref, rows):
    step = min(rows, 256)
    for r in range(0, rows, step):
        h_ref[r:r + step, :] = _rms(x_ref[r:r + step, :], g_ref[...]).astype(h_ref.dtype)


def _params(*sem):
    return pltpu.CompilerParams(dimension_semantics=sem, vmem_limit_bytes=VMEM_LIMIT)


def _normmm_kernel(x_ref, g_ref, *refs, rc, tns):
    w_refs, o_refs = refs[:len(tns)], refs[len(tns):]
    tm = x_ref.shape[0]

    def norm(r):
        return _rms(x_ref[r * rc:(r + 1) * rc, :], g_ref[...]).astype(BF16)

    def matmuls(r, h):
        for w_ref, o_ref, tn in zip(w_refs, o_refs, tns):
            for j in range(w_ref.shape[1] // tn):
                cols = slice(j * tn, (j + 1) * tn)
                o_ref[r * rc:(r + 1) * rc, cols] = _dot(h, w_ref[:, cols]).astype(o_ref.dtype)

    live = {}
    for step in range(tm // rc + 1):
        if step < tm // rc:
            live[step] = norm(step)
        if step >= 1:
            matmuls(step - 1, live.pop(step - 1))


def norm_matmul(x, g, ws, out_dtypes, tm, tns):
    t, d = x.shape
    return pl.pallas_call(
        functools.partial(_normmm_kernel, rc=min(tm, 256), tns=tuple(tns)),
        grid=(t // tm,),
        in_specs=[pl.BlockSpec((tm, d), lambda i: (i, 0)),
                  pl.BlockSpec((1, d), lambda i: (0, 0))]
        + [pl.BlockSpec(w.shape, lambda i: (0, 0)) for w in ws],
        out_specs=[pl.BlockSpec((tm, w.shape[1]), lambda i: (i, 0)) for w in ws],
        out_shape=[jax.ShapeDtypeStruct((t, w.shape[1]), dt) for w, dt in zip(ws, out_dtypes)],
        compiler_params=_params("parallel"),
        name="norm_matmul",
    )(x, g.reshape(1, d), *ws)


def _moba_kernel(q_ref, k_ref, v_ref, tab_ref, o_ref, vt_ref, km_ref, *, nb, n_sel, heads):
    for hh in range(heads):
        lanes = pl.ds(hh * HD, HD)
        _moba_head(q_ref.at[:, lanes], k_ref.at[:, lanes], v_ref.at[:, lanes], tab_ref.at[hh],
                   o_ref.at[:, lanes], vt_ref.at[hh], km_ref.at[hh], nb=nb, n_sel=n_sel)


def _moba_head(q_ref, k_ref, v_ref, tab_ref, o_ref, vt_ref, km_ref, *, nb, n_sel):
    blk = MOBA_BLOCK
    scale = HD ** -0.5 * LOG2E
    nbp = km_ref.shape[0]
    km_ref[...] = jnp.zeros_like(km_ref)
    for j in range(nb):
        rows = slice(j * blk, (j + 1) * blk)
        vt_ref[:, rows] = v_ref[rows, :].astype(F32).T.astype(BF16)
        km_ref[j:j + 1, :] = jnp.mean(k_ref[rows, :].astype(F32), axis=0, keepdims=True)
    km = km_ref[...]
    km_hi = km.astype(BF16)
    km_lo = (km - km_hi.astype(F32)).astype(BF16)
    q_all = q_ref[...]
    gate = _dot_nt(km_hi, q_all) + _dot_nt(km_lo, q_all)
    sub = lax.broadcasted_iota(jnp.int32, (nbp, blk), 0)
    bias_far = tab_ref[2, 0:1, :]

    def scores(i):
        return _dot_nt(k_ref[0:(i + 1) * blk, :], q_ref[i * blk:(i + 1) * blk, :])

    def softmax(i, s_all):
        rows = slice(i * blk, (i + 1) * blk)
        if i > 0:
            valid = sub < i
            gm = jnp.where(valid, gate[:, rows], -jnp.inf)
            rank = jnp.zeros((nbp, blk), F32)
            for jp in range(i):
                row = gm[jp:jp + 1, :]
                beats = (row > gm) | ((row == gm) & (sub > jp))
                rank = rank + jnp.where(beats, 1.0, 0.0)
            selm = jnp.where(valid & (rank < n_sel), 0.0, -jnp.inf)
        bands = []
        for j in range(i + 1):
            sj = s_all[j * blk:(j + 1) * blk, :] * scale
            if j == i:
                sj = sj + tab_ref[0]
            elif j == i - 1:
                sj = sj + (tab_ref[1] + selm[j:j + 1, :])
            else:
                sj = sj + (bias_far + selm[j:j + 1, :])
            bands.append(sj)
        s = jnp.concatenate(bands, axis=0) if i > 0 else bands[0]
        m = jnp.max(s, axis=0, keepdims=True)
        return jnp.exp2(s - m).astype(BF16)

    def output(i, p):
        nk = (i + 1) * blk
        l = _dot(jnp.ones((SUBLANES, nk), BF16), p)[0:1, :]
        acc = _dot(vt_ref[:, 0:nk], p)
        o_ref[i * blk:(i + 1) * blk, :] = (acc / l).T.astype(o_ref.dtype)

    s_live, p_live = {}, {}
    units = [sorted({u, nb - 1 - u}) for u in range((nb + 1) // 2)]
    for step in range(len(units) + 2):
        if step < len(units):
            for i in units[step]:
                s_live[i] = scores(i)
        if 1 <= step <= len(units):
            for i in units[step - 1]:
                p_live[i] = softmax(i, s_live.pop(i))
        if step >= 2:
            for i in units[step - 2]:
                output(i, p_live.pop(i))


def _t5_bucket_idx(dist):
    n = jnp.maximum(dist, 0)
    max_exact = REL_BUCKETS // 2
    large = max_exact + (jnp.log(jnp.maximum(n, 1).astype(F32) / max_exact)
                         / math.log(REL_MAX_DIST / max_exact)
                         * (REL_BUCKETS - max_exact)).astype(jnp.int32)
    large = jnp.minimum(large, REL_BUCKETS - 1)
    return jnp.where(n < max_exact, n, large)


def _bias_table_kernel(rb_ref, idx_ref, o_ref):
    h = pl.program_id(0)
    for t in range(3):
        idx = idx_ref[t]
        acc = jnp.full(idx.shape, -jnp.inf, F32)
        for b in range(REL_BUCKETS):
            acc = jnp.where(idx == b, rb_ref[h, b], acc)
        o_ref[t] = acc * LOG2E


def _moba_bias_tables(rel_bias):
    blk = MOBA_BLOCK
    loc = jnp.arange(blk)
    d0 = loc[None, :] - loc[:, None]
    idx = jnp.stack([jnp.where(d0 >= 0, _t5_bucket_idx(d0), -1),
                     _t5_bucket_idx(d0 + blk),
                     _t5_bucket_idx(d0 + 2 * blk)]).astype(jnp.int32)
    return pl.pallas_call(
        _bias_table_kernel,
        grid=(MOBA_HEADS,),
        in_specs=[pl.BlockSpec(memory_space=pltpu.SMEM),
                  pl.BlockSpec((3, blk, blk), lambda h: (0, 0, 0))],
        out_specs=pl.BlockSpec((None, 3, blk, blk), lambda h: (h, 0, 0, 0)),
        out_shape=jax.ShapeDtypeStruct((MOBA_HEADS, 3, blk, blk), F32),
        compiler_params=_params("parallel"),
        name="moba_bias_tables",
    )(rel_bias.T.astype(F32), idx)


def moba_attention(proj, tabs, bsz, s):
    assert s % MOBA_BLOCK == 0
    nb = s // MOBA_BLOCK
    nbp = -(-nb // 8) * 8
    n_sel = min(MOBA_TOPK, nb - 1)
    blk = MOBA_BLOCK

    hps = MOBA_HEADS_PER_STEP
    groups = MOBA_HEADS // hps

    def col(off):
        return pl.BlockSpec((None, s, hps * HD), lambda h, b: (b, 0, off + h))

    return pl.pallas_call(
        functools.partial(_moba_kernel, nb=nb, n_sel=n_sel, heads=hps),
        grid=(groups, bsz),
        in_specs=[col(0), col(groups), col(2 * groups),
                  pl.BlockSpec((hps, 3, blk, blk), lambda h, b: (h, 0, 0, 0))],
        out_specs=pl.BlockSpec((None, s, hps * HD), lambda h, b: (b, 0, h)),
        out_shape=jax.ShapeDtypeStruct((bsz, s, MOBA_WIDTH), BF16),
        scratch_shapes=[pltpu.VMEM((hps, HD, s), BF16),
                        pltpu.VMEM((hps, nbp, HD), F32)],
        compiler_params=_params("parallel", "parallel"),
        name="moba_attention",
    )(proj, proj, proj, tabs)


def _memattn_kernel(q_ref, kv_ref, o_ref, *, n_chunks, qc):
    scale = HD ** -0.5 * LOG2E
    ks = [kv_ref[:, h * HD:(h + 1) * HD] for h in range(MEM_HEADS)]
    vts = [kv_ref[:, MEM_WIDTH + h * HD:MEM_WIDTH + (h + 1) * HD].astype(F32).T.astype(BF16)
           for h in range(MEM_HEADS)]
    ones = jnp.ones((SUBLANES, kv_ref.shape[0]), BF16)

    def scores(h, c):
        return _dot_nt(ks[h], q_ref[c * qc:(c + 1) * qc, h * HD:(h + 1) * HD])

    def softmax(s):
        s = s * scale
        return jnp.exp2(s - jnp.max(s, axis=0, keepdims=True)).astype(BF16)

    def output(h, c, p):
        l = _dot(ones, p)[0:1, :]
        o_ref[c * qc:(c + 1) * qc, h * HD:(h + 1) * HD] = (_dot(vts[h], p) / l).T.astype(o_ref.dtype)

    work = [(h, c) for h in range(MEM_HEADS) for c in range(n_chunks)]
    s_live, p_live = {}, {}
    for step in range(len(work) + 2):
        if step < len(work):
            s_live[step] = scores(*work[step])
        if 1 <= step <= len(work):
            p_live[step - 1] = softmax(s_live.pop(step - 1))
        if step >= 2:
            output(*work[step - 2], p_live.pop(step - 2))


def memory_attention(proj, q_col, kv, bsz, s):
    m_len = kv.shape[1]
    qc = 256
    assert q_col % MEM_WIDTH == 0
    return pl.pallas_call(
        functools.partial(_memattn_kernel, n_chunks=s // qc, qc=qc),
        grid=(bsz,),
        in_specs=[pl.BlockSpec((None, s, MEM_WIDTH), lambda b: (b, 0, q_col // MEM_WIDTH)),
                  pl.BlockSpec((None, m_len, 2 * MEM_WIDTH), lambda b: (b, 0, 0))],
        out_specs=pl.BlockSpec((None, s, MEM_WIDTH), lambda b: (b, 0, 0)),
        out_shape=jax.ShapeDtypeStruct((bsz, s, MEM_WIDTH), BF16),
        compiler_params=_params("parallel"),
        name="memory_attention",
    )(proj, kv)


def _outproj_kernel(x_ref, ya_ref, yb_ref, wa_ref, wb_ref, o_ref):
    o_ref[...] = x_ref[...] + _dot(ya_ref[...], wa_ref[...]) + _dot(yb_ref[...], wb_ref[...])


def out_projection(x, ya, yb, wa, wb, tm):
    t, d = x.shape
    ka, kb = ya.shape[1], yb.shape[1]
    return pl.pallas_call(
        _outproj_kernel,
        grid=(t // tm,),
        in_specs=[pl.BlockSpec((tm, d), lambda i: (i, 0)),
                  pl.BlockSpec((tm, ka), lambda i: (i, 0)),
                  pl.BlockSpec((tm, kb), lambda i: (i, 0)),
                  pl.BlockSpec((ka, d), lambda i: (0, 0)),
                  pl.BlockSpec((kb, d), lambda i: (0, 0))],
        out_specs=pl.BlockSpec((tm, d), lambda i: (i, 0)),
        out_shape=jax.ShapeDtypeStruct((t, d), F32),
        compiler_params=_params("parallel"),
        name="out_projection",
    )(x, ya, yb, wa, wb)


def _swiglu_accumulate(h, wg, wu, wd, acc_ref):
    a = _dot(h, wg)
    u = _dot(h, wu)
    acc_ref[...] += _dot((a * _sigmoid(a) * u).astype(BF16), wd)


def _ffn_kernel(x_ref, g_ref, wg_ref, wu_ref, wd_ref, o_ref, h_ref, acc_ref, *, tm):
    f = pl.program_id(1)

    @pl.when(f == 0)
    def _():
        _rms_rows_to(h_ref, x_ref, g_ref, tm)
        acc_ref[...] = x_ref[...]

    _swiglu_accumulate(h_ref[...], wg_ref[...], wu_ref[...], wd_ref[...], acc_ref)

    @pl.when(f == pl.num_programs(1) - 1)
    def _():
        o_ref[...] = acc_ref[...]


def dense_ffn(x, g, wg, wu, wd, tm, tf):
    t, d = x.shape
    ff = wg.shape[1]
    return pl.pallas_call(
        functools.partial(_ffn_kernel, tm=tm),
        grid=(t // tm, ff // tf),
        in_specs=[pl.BlockSpec((tm, d), lambda i, f: (i, 0)),
                  pl.BlockSpec((1, d), lambda i, f: (0, 0)),
                  pl.BlockSpec((d, tf), lambda i, f: (0, f)),
                  pl.BlockSpec((d, tf), lambda i, f: (0, f)),
                  pl.BlockSpec((tf, d), lambda i, f: (f, 0))],
        out_specs=pl.BlockSpec((tm, d), lambda i, f: (i, 0)),
        out_shape=jax.ShapeDtypeStruct((t, d), F32),
        scratch_shapes=[pltpu.VMEM((tm, d), BF16), pltpu.VMEM((tm, d), F32)],
        compiler_params=_params("parallel", "arbitrary"),
        name="dense_ffn",
    )(x, g.reshape(1, d), wg, wu, wd)


def _ssd_kernel(z_ref, xs_ref, b_ref, c_ref, dt_ref, cw_ref, cb_ref, dtb_ref, alog_ref, dsk_ref,
                gout_ref, exp_ref, o_ref, h_ref, *, nc, unroll, groups):
    gw, n = SSD_GW, SSD_STATE
    chunks = []
    for sg in range(groups):
        wide, narrow, lanes = pl.ds(sg * gw, gw), pl.ds(sg * n, n), pl.ds(sg * LANES, LANES)
        chunks.append(_ssd_group_chunk(
            z_ref.at[:, wide], xs_ref.at[:, wide], b_ref.at[:, narrow], c_ref.at[:, narrow], dt_ref.at[:, lanes],
            cw_ref.at[sg], cb_ref.at[sg], dtb_ref.at[sg], alog_ref.at[sg], dsk_ref.at[sg], gout_ref.at[sg],
            exp_ref, o_ref.at[:, wide], h_ref.at[sg]))

    def body(it, carry):
        work = [(sg, it * unroll + u) for u in range(unroll) for sg in range(groups)]
        live = {}
        for step in range(len(work) + 1):
            if step < len(work):
                sg, c = work[step]
                live[step] = chunks[sg][0](c)
            if step >= 1:
                chunks[work[step - 1][0]][1](*live.pop(step - 1))
        return carry

    lax.fori_loop(0, nc // unroll, body, 0)


def _ssd_group_chunk(z_ref, xs_ref, b_ref, c_ref, dt_ref, cw_ref, cb_ref, dtb_ref, alog_ref, dsk_ref,
                     gout_ref, exp_ref, o_ref, h_ref):
    L = SSD_CHUNK
    gw = SSD_GW
    n = SSD_STATE
    h_ref[...] = jnp.zeros(h_ref.shape, h_ref.dtype)
    a_neg = -jnp.exp(alog_ref[...])
    causal = (lax.broadcasted_iota(jnp.int32, (L, L), 0) >= lax.broadcasted_iota(jnp.int32, (L, L), 1))
    tri = jnp.where(causal, 1.0, 0.0).astype(BF16)
    first_half = lax.broadcasted_iota(jnp.int32, (1, 2 * SSD_HD), 1) < SSD_HD
    expm = exp_ref[...]
    srcs = (xs_ref, b_ref, c_ref)
    sh_r = lax.broadcasted_iota(jnp.int32, (L, 2 * L), 0)
    sh_c = lax.broadcasted_iota(jnp.int32, (L, 2 * L), 1)
    shifts = [jnp.where(sh_c == sh_r + (L - s), 1.0, 0.0).astype(BF16) for s in range(1, SSD_CONV)]

    def local(c):
        r0 = pl.multiple_of(c * L, L)
        rq = pl.multiple_of(jnp.maximum(r0 - L, 0), L)
        cur = jnp.concatenate([ref[pl.ds(r0, L), :] for ref in srcs], axis=1)
        prev = jnp.concatenate([ref[pl.ds(rq, L), :] for ref in srcs], axis=1)
        prev = jnp.where(c > 0, prev, jnp.zeros_like(prev))
        ext = jnp.concatenate([prev, cur], axis=0)
        conv = cb_ref[...] + cw_ref[SSD_CONV - 1:SSD_CONV, :] * cur.astype(F32)
        for s in range(1, SSD_CONV):
            k = SSD_CONV - 1 - s
            conv = conv + cw_ref[k:k + 1, :] * _dot(shifts[s - 1], ext)
        act = conv * _sigmoid(conv)
        xs = act[:, 0:gw]
        bm = act[:, gw:gw + n]
        cm = act[:, gw + n:gw + 2 * n]

        dtr = dt_ref[pl.ds(r0, L), :] + dtb_ref[...]
        dt = jnp.maximum(dtr, 0.0) + jnp.log(1.0 + jnp.exp(-jnp.abs(dtr)))
        la = dt * a_neg
        cs = _dot_split_r(tri, la, 3)
        cs_t = cs.T
        dt_t = dt.T
        bm16 = bm.astype(BF16)
        cm16 = cm.astype(BF16)
        scores = _dot_nt(cm16, bm16)
        from_start = _dot_split_l(jnp.exp(cs), expm, 2)
        w_end = dt * jnp.exp(cs[L - 1:L, :] - cs)
        xs_t = xs.T
        return r0, xs, bm, cm16, cs, cs_t, dt_t, scores, from_start, w_end, xs_t

    def carried(r0, xs, bm, cm16, cs, cs_t, dt_t, scores, from_start, w_end, xs_t):
        y = _dot_nt(cm16, h_ref[...].astype(BF16)) * from_start
        y_in = []
        for e in range(SSD_GHEADS):
            diff = cs[:, e:e + 1] - cs_t[e:e + 1, :]
            dec = jnp.exp(jnp.where(causal, diff, -jnp.inf))
            mm = (scores * dec * dt_t[e:e + 1, :]).astype(BF16)
            if e % 2 == 0:
                mm_even = mm
            else:
                xp = xs[:, (e - 1) * SSD_HD:(e + 1) * SSD_HD]
                rhs = jnp.concatenate([jnp.where(first_half, xp, 0.0), jnp.where(first_half, 0.0, xp)], axis=0)
                y_in.append(_dot(jnp.concatenate([mm_even, mm], axis=1), rhs.astype(BF16)))
            bw = (bm * w_end[:, e:e + 1]).astype(BF16)
            st = _dot(xs_t[e * SSD_HD:(e + 1) * SSD_HD, :].astype(BF16), bw)
            cdec = jnp.exp(cs[L - 1:L, e:e + 1])
            hs = slice(e * SSD_HD, (e + 1) * SSD_HD)
            h_ref[hs, :] = h_ref[hs, :] * cdec + st
        y = y + jnp.concatenate(y_in, axis=1) + xs * dsk_ref[...]
        z = z_ref[pl.ds(r0, L), :].astype(F32)
        u = y * (z * _sigmoid(z))
        ms = jnp.mean(u * u, axis=-1, keepdims=True)
        o_ref[pl.ds(r0, L), :] = (u * lax.rsqrt(ms + EPS) * gout_ref[...]).astype(o_ref.dtype)

    return local, carried


def ssd_mixer(proj, dt_all, cw_g, cb_g, dtb_g, alog_g, dsk_g, gout_g, expand, bsz, s):
    assert s % SSD_CHUNK == 0
    nc = s // SSD_CHUNK
    gps = SSD_GROUPS_PER_STEP
    gw = SSD_GW * gps
    nw = SSD_STATE * gps
    z_blk = 0
    xs_blk = SSD_INNER // gw
    b_blk = 2 * SSD_INNER // nw
    c_blk = b_blk + SSD_GROUPS // gps

    def seq(width, off):
        return pl.BlockSpec((None, s, width), lambda b, g: (b, 0, off + g))

    def par(rows, width):
        return pl.BlockSpec((gps, rows, width), lambda b, g: (g, 0, 0))

    return pl.pallas_call(
        functools.partial(_ssd_kernel, nc=nc, unroll=_pick(nc, (4, 2, 1)), groups=gps),
        grid=(bsz, SSD_GROUPS // gps),
        in_specs=[seq(gw, z_blk), seq(gw, xs_blk), seq(nw, b_blk), seq(nw, c_blk),
                  seq(LANES * gps, 0),
                  par(SSD_CONV, SSD_CONV_GW), par(1, SSD_CONV_GW), par(1, LANES), par(1, LANES),
                  par(1, SSD_GW), par(1, SSD_GW),
                  pl.BlockSpec((LANES, SSD_GW), lambda b, g: (0, 0))],
        out_specs=seq(gw, 0),
        out_shape=jax.ShapeDtypeStruct((bsz, s, SSD_INNER), BF16),
        scratch_shapes=[pltpu.VMEM((gps, SSD_GW, SSD_STATE), F32)],
        compiler_params=_params("parallel", "parallel"),
        name="ssd_mixer",
    )(proj, proj, proj, proj, dt_all, cw_g, cb_g, dtb_g, alog_g, dsk_g, gout_g, expand)


def _ssd_group_params(conv_w, conv_b, dt_bias, a_log, d_skip, g_out):
    g, gh, gw, n = SSD_GROUPS, SSD_GHEADS, SSD_GW, SSD_STATE

    def conv_cols(a):
        xs = a[..., :SSD_INNER].reshape(a.shape[:-1] + (g, gw))
        bb = a[..., SSD_INNER:SSD_INNER + g * n].reshape(a.shape[:-1] + (g, n))
        cc = a[..., SSD_INNER + g * n:].reshape(a.shape[:-1] + (g, n))
        return jnp.moveaxis(jnp.concatenate([xs, bb, cc], axis=-1), -2, 0)

    cw_g = conv_cols(conv_w.astype(F32))
    cb_g = conv_cols(conv_b.astype(F32)[None, :])

    def per_head(a):
        return jnp.pad(a.astype(F32).reshape(g, 1, gh), ((0, 0), (0, 0), (0, LANES - gh)))

    dsk_g = jnp.repeat(d_skip.astype(F32), SSD_HD).reshape(g, 1, gw)
    gout_g = g_out.astype(F32).reshape(g, 1, gw)
    expand = (jnp.arange(LANES)[:, None] == (jnp.arange(gw)[None, :] // SSD_HD)).astype(BF16)
    return cw_g, cb_g, per_head(dt_bias), per_head(a_log), dsk_g, gout_g, expand


def _router_kernel(x_ref, g_ref, wrt_ref, br_ref, h_ref, route_ref, runs_ref, cnt_ref, carry_ref, *, tm):
    @pl.when(pl.program_id(0) == 0)
    def _():
        carry_ref[...] = jnp.zeros_like(carry_ref)

    ne = N_EXPERTS
    hn = _rms(x_ref[...], g_ref[...])
    h_hi = hn.astype(BF16)
    h_ref[...] = h_hi
    h_lo = (hn - h_hi.astype(F32)).astype(BF16)
    wr = wrt_ref[...]
    w_hi = wr.astype(BF16)
    w_lo = (wr - w_hi.astype(F32)).astype(BF16)
    logits = _dot_nt(w_hi, h_hi) + _dot_nt(w_hi, h_lo) + _dot_nt(w_lo, h_hi) + br_ref[:, 0:1]
    sub = lax.broadcasted_iota(jnp.int32, (ne, tm), 0)
    l1 = jnp.max(logits, axis=0, keepdims=True)
    i1 = jnp.min(jnp.where(logits == l1, sub, ne), axis=0, keepdims=True)
    rest = jnp.where(sub == i1, -jnp.inf, logits)
    l2 = jnp.max(rest, axis=0, keepdims=True)
    i2 = jnp.min(jnp.where(rest == l2, sub, ne), axis=0, keepdims=True)
    e2 = jnp.exp(l2 - l1)
    g1 = 1.0 / (1.0 + e2)
    g2 = e2 / (1.0 + e2)
    sel = jnp.where((sub == i1) | (sub == i2), 1.0, 0.0)
    before = (lax.broadcasted_iota(jnp.int32, (tm, tm), 0) < lax.broadcasted_iota(jnp.int32, (tm, tm), 1))
    prefix = _dot(sel.astype(BF16), jnp.where(before, 1.0, 0.0).astype(BF16))
    lane = lax.broadcasted_iota(jnp.int32, (ne, LANES), 1)
    subl = lax.broadcasted_iota(jnp.int32, (ne, LANES), 0)
    n_e = (jnp.sum(sel, axis=1, keepdims=True) + jnp.zeros((ne, LANES), F32)).astype(jnp.int32)
    n_pad = (n_e + (SUBLANES - 1)) & (-SUBLANES)
    off = jnp.zeros((ne, LANES), jnp.int32)
    for j in range(ne - 1):
        off = off + jnp.where(subl > j, n_pad[j:j + 1, :], 0)
    slot = prefix + off[:, 0:1].astype(F32)
    p1 = jnp.sum(jnp.where(sub == i1, slot, 0.0), axis=0, keepdims=True)
    p2 = jnp.sum(jnp.where(sub == i2, slot, 0.0), axis=0, keepdims=True)
    route_ref[...] = jnp.where(sub == 0, p1, jnp.where(sub == 1, p2, jnp.where(
        sub == 2, g1, jnp.where(sub == 3, g2, 0.0))))
    carry = carry_ref[...]
    runs_ref[...] = jnp.where(lane == 0, off, jnp.where(lane == 1, carry, jnp.where(lane == 2, n_pad, 0)))
    carry_ref[...] = carry + n_pad
    cnt_ref[...] = carry + n_pad


def moe_router(x, g, w_router, b_router, tm):
    t, d = x.shape
    ne = N_EXPERTS
    wrt = w_router.astype(F32).T
    br = jnp.broadcast_to(b_router.astype(F32)[:, None], (ne, LANES))
    return pl.pallas_call(
        functools.partial(_router_kernel, tm=tm),
        grid=(t // tm,),
        in_specs=[pl.BlockSpec((tm, d), lambda i: (i, 0)),
                  pl.BlockSpec((1, d), lambda i: (0, 0)),
                  pl.BlockSpec((ne, d), lambda i: (0, 0)),
                  pl.BlockSpec((ne, LANES), lambda i: (0, 0))],
        out_specs=[pl.BlockSpec((tm, d), lambda i: (i, 0)),
                   pl.BlockSpec((ne, tm), lambda i: (0, i)),
                   pl.BlockSpec((ne, LANES), lambda i: (i, 0)),
                   pl.BlockSpec((ne, LANES), lambda i: (0, 0))],
        out_shape=[jax.ShapeDtypeStruct((t, d), BF16),
                   jax.ShapeDtypeStruct((ne, t), F32),
                   jax.ShapeDtypeStruct((t // tm * ne, LANES), jnp.int32),
                   jax.ShapeDtypeStruct((ne, LANES), jnp.int32)],
        scratch_shapes=[pltpu.VMEM((ne, LANES), jnp.int32)],
        compiler_params=_params("arbitrary"),
        name="moe_router",
    )(x, g.reshape(1, d), wrt, br)


def _pieces(length, lo_bit, hi_bit):
    for b in range(lo_bit, hi_bit):
        n = 1 << b
        yield n, length & (n - 1), ((length >> b) & 1) == 1


def _run_copies(tab_ref, tile, tile_ref, buf_ref, sem, tm, to_buf, wait):
    for e in range(N_EXPERTS):
        base = (tile * N_EXPERTS + e) * 3
        off, dst, rows = tab_ref[base], tab_ref[base + 1], tab_ref[base + 2]
        for n, lo, present in _pieces(rows, SUBLANES.bit_length() - 1, tm.bit_length()):
            in_tile = tile_ref.at[pl.ds(pl.multiple_of(off + lo, SUBLANES), n), :]
            in_buf = buf_ref.at[pl.ds(pl.multiple_of(dst + lo, SUBLANES), n), :]
            cp = pltpu.make_async_copy(in_tile, in_buf, sem) if to_buf else pltpu.make_async_copy(in_buf, in_tile, sem)

            @pl.when(present)
            def _():
                if wait:
                    cp.wait()
                else:
                    cp.start()


def _dispatch_kernel(tab_ref, meta_ref, route_ref, h_ref, xb_ref, srt_ref, zero_ref, sem, zsem, *, tm, n_tiles):
    i = pl.program_id(0)
    nt = pl.num_programs(0)
    slot = i % 2
    rs = srt_ref.shape[1]

    def runs(tile, slot, wait):
        _run_copies(tab_ref, tile, srt_ref.at[slot], xb_ref, sem.at[slot], tm, True, wait)

    @pl.when(i >= 2)
    def _():
        runs(i - 2, slot, True)

    pos = route_ref[0:2, :].astype(jnp.int32)
    row = lax.broadcasted_iota(jnp.int32, (rs, tm), 0)
    onehot = jnp.where((row == pos[0:1, :]) | (row == pos[1:2, :]), 1.0, 0.0).astype(BF16)
    srt_ref[slot] = _dot(onehot, h_ref[...])
    runs(i, slot, False)

    @pl.when(i == nt - 1)
    def _():
        zero_ref[...] = jnp.zeros_like(zero_ref)
        zrows = zero_ref.shape[0]
        per = EXPERT_ROWS // zrows
        first_unused = meta_ref[2 * N_EXPERTS] * per
        for wait in (False, True):
            def go(cp, cond):
                @pl.when(cond)
                def _():
                    if wait:
                        cp.wait()
                    else:
                        cp.start()

            for e in range(N_EXPERTS):
                c = meta_ref[e]
                first = meta_ref[N_EXPERTS + e]
                tail = (-c) & (EXPERT_ROWS - 1)
                for n, lo, present in _pieces(tail, SUBLANES.bit_length() - 1, EXPERT_ROWS.bit_length() - 1):
                    row0 = pl.multiple_of(first + c + lo, SUBLANES)
                    go(pltpu.make_async_copy(zero_ref.at[pl.ds(0, n), :], xb_ref.at[pl.ds(row0, n), :], zsem), present)
            for j in range(N_EXPERTS * per):
                row0 = pl.multiple_of(jnp.minimum(first_unused + j, n_tiles * per - 1) * zrows, zrows)
                go(pltpu.make_async_copy(zero_ref, xb_ref.at[pl.ds(row0, zrows), :], zsem),
                   first_unused + j < n_tiles * per)

        @pl.when(i >= 1)
        def _():
            runs(i - 1, 1 - slot, True)

        runs(i, slot, True)


def moe_dispatch(tab, meta, route, h, tm, n_tiles):
    t, d = h.shape
    ne = route.shape[0]
    rs = TOP_K * tm + LANES
    return pl.pallas_call(
        functools.partial(_dispatch_kernel, tm=tm, n_tiles=n_tiles),
        grid=(t // tm,),
        in_specs=[pl.BlockSpec(memory_space=pltpu.SMEM),
                  pl.BlockSpec(memory_space=pltpu.SMEM),
                  pl.BlockSpec((ne, tm), lambda i: (0, i)),
                  pl.BlockSpec((tm, d), lambda i: (i, 0))],
        out_specs=pl.BlockSpec(memory_space=pl.ANY),
        out_shape=jax.ShapeDtypeStruct((n_tiles * EXPERT_ROWS, d), F32),
        scratch_shapes=[pltpu.VMEM((2, rs, d), F32),
                        pltpu.VMEM((EXPERT_ROWS // 2, d), F32),
                        pltpu.SemaphoreType.DMA((2,)), pltpu.SemaphoreType.DMA(())],
        compiler_params=_params("arbitrary"),
        name="moe_dispatch",
    )(tab, meta, route, h)


def _expert_kernel(te_ref, nr_ref, xb_ref, wg_ref, wu_ref, wd_ref, o_ref, acc_ref):
    c = pl.program_id(0)
    f = pl.program_id(1)
    rows = nr_ref[c]
    r, rc = EXPERT_ROWS, EXPERT_ROW_CHUNK

    @pl.when(f == 0)
    def _():
        acc_ref[...] = jnp.zeros_like(acc_ref)

    @pl.when(rows == r)
    def _():
        _swiglu_accumulate(xb_ref[...].astype(BF16), wg_ref[...].astype(BF16), wu_ref[...].astype(BF16),
                           wd_ref[...].astype(BF16), acc_ref)

    @pl.when(jnp.logical_and(rows > 0, rows < r))
    def _():
        wg, wu, wd = wg_ref[...].astype(BF16), wu_ref[...].astype(BF16), wd_ref[...].astype(BF16)
        for j in range(r // rc):
            @pl.when(j * rc < rows)
            def _():
                _swiglu_accumulate(xb_ref[j * rc:(j + 1) * rc, :].astype(BF16), wg, wu, wd,
                                   acc_ref.at[j * rc:(j + 1) * rc, :])

    @pl.when(f == pl.num_programs(1) - 1)
    def _():
        o_ref[...] = acc_ref[...]


def expert_ffn(tile_expert, tile_rows, xb, wg, wu, wd, tf):
    n_rows, d = xb.shape
    ff = wg.shape[2]
    r = EXPERT_ROWS
    nf = ff // tf

    def wcol(c, f, te, nr):
        return (te[c], 0, jnp.where(nr[c] > 0, f, nf - 1))

    def wrow(c, f, te, nr):
        return (te[c], jnp.where(nr[c] > 0, f, nf - 1), 0)

    grid_spec = pltpu.PrefetchScalarGridSpec(
        num_scalar_prefetch=2,
        grid=(n_rows // r, nf),
        in_specs=[pl.BlockSpec((r, d), lambda c, f, te, nr: (c, 0)),
                  pl.BlockSpec((None, d, tf), wcol),
                  pl.BlockSpec((None, d, tf), wcol),
                  pl.BlockSpec((None, tf, d), wrow)],
        out_specs=pl.BlockSpec((r, d), lambda c, f, te, nr: (c, 0)),
        scratch_shapes=[pltpu.VMEM((r, d), F32)],
    )
    return pl.pallas_call(
        _expert_kernel,
        grid_spec=grid_spec,
        out_shape=jax.ShapeDtypeStruct((n_rows, d), F32),
        compiler_params=_params("parallel", "arbitrary"),
        name="expert_ffn",
    )(tile_expert, tile_rows, xb, wg, wu, wd)


def _combine_kernel(tab_ref, route_ref, x_ref, g_ref, yb_ref, o_ref, srt_ref, sem, *, tm):
    i = pl.program_id(0)
    nt = pl.num_programs(0)
    slot = i % 2
    rs = srt_ref.shape[1]

    def runs(tile, slot, wait):
        _run_copies(tab_ref, tile, srt_ref.at[slot], yb_ref, sem.at[slot], tm, False, wait)

    @pl.when(i == 0)
    def _():
        srt_ref[...] = jnp.zeros_like(srt_ref)
        runs(0, 0, False)

    @pl.when(i + 1 < nt)
    def _():
        runs(i + 1, 1 - slot, False)

    runs(i, slot, True)
    aux = route_ref[...]
    aux_t = jnp.concatenate([aux, jnp.zeros((LANES - aux.shape[0], tm), F32)], axis=0).T
    srt = srt_ref[slot].astype(BF16)
    col = lax.broadcasted_iota(jnp.int32, (tm, rs), 1)
    ys = []
    for k in range(TOP_K):
        onehot = jnp.where(col == aux_t[:, k:k + 1].astype(jnp.int32), 1.0, 0.0).astype(BF16)
        ys.append(aux_t[:, TOP_K + k:TOP_K + k + 1] * _dot(onehot, srt))
    o_ref[...] = _rms(x_ref[...] + (ys[0] + ys[1]), g_ref[...])


def combine_final(tab, route, x, g, yb, tm):
    t, d = x.shape
    ne = route.shape[0]
    rs = TOP_K * tm + LANES
    return pl.pallas_call(
        functools.partial(_combine_kernel, tm=tm),
        grid=(t // tm,),
        in_specs=[pl.BlockSpec(memory_space=pltpu.SMEM),
                  pl.BlockSpec((ne, tm), lambda i: (0, i)),
                  pl.BlockSpec((tm, d), lambda i: (i, 0)),
                  pl.BlockSpec((1, d), lambda i: (0, 0)),
                  pl.BlockSpec(memory_space=pl.ANY)],
        out_specs=pl.BlockSpec((tm, d), lambda i: (i, 0)),
        out_shape=jax.ShapeDtypeStruct((t, d), F32),
        scratch_shapes=[pltpu.VMEM((2, rs, d), F32), pltpu.SemaphoreType.DMA((2,))],
        compiler_params=_params("arbitrary"),
        name="combine_final",
    )(tab, route, x, g.reshape(1, d), yb)


def _pick(n, prefs):
    for p in prefs:
        if n % p == 0:
            return p
    return n


class _Tiles(NamedTuple):
    proj_rows: int
    out_rows: int
    mem_rows: int
    moe_rows: int
    moba_cols: int
    ssd_cols: int
    ffn_cols: int
    expert_cols: int


def _choose_tiles(t, mem_rows):
    return _Tiles(proj_rows=_pick(t, (512, 256)), out_rows=_pick(t, (1024, 512, 256)),
                  mem_rows=_pick(mem_rows, (1024, 512, 256)), moe_rows=_pick(t, (512, 256)),
                  moba_cols=1280, ssd_cols=1536, ffn_cols=1792, expert_cols=512)


class _ExpertLayout(NamedTuple):
    n_tiles: int
    tile_expert: Any
    tile_rows: Any
    run_table: Any
    meta: Any


def _expert_layout(runs, cnt, t, tm):
    r, rc = EXPERT_ROWS, EXPERT_ROW_CHUNK
    nt = t // tm
    n_tiles = -(-(t * TOP_K + nt * N_EXPERTS * (SUBLANES - 1)) // r) + N_EXPERTS
    tiles_per = (cnt + r - 1) // r
    ends = jnp.cumsum(tiles_per)
    first_row = (ends - tiles_per) * r
    n_used = ends[-1].astype(jnp.int32)
    step = jnp.minimum(jnp.arange(n_tiles, dtype=jnp.int32), n_used - 1)
    tile_expert = jnp.sum(step[:, None] >= ends[None, :], axis=1).astype(jnp.int32)
    rows_left = jnp.sum(jnp.where(tile_expert[:, None] == jnp.arange(N_EXPERTS)[None, :],
                                  (cnt + first_row)[None, :], 0), axis=1) - step * r
    tile_rows = jnp.where(jnp.arange(n_tiles) < n_used, jnp.clip((rows_left + rc - 1) // rc * rc, 0, r), 0)
    tab = runs.reshape(nt, N_EXPERTS, LANES)[:, :, :3]
    tab = tab.at[:, :, 1].add(first_row[None, :]).reshape(-1).astype(jnp.int32)
    meta = jnp.concatenate([cnt, first_row, n_used.reshape(1)]).astype(jnp.int32)
    return _ExpertLayout(n_tiles, tile_expert, tile_rows.astype(jnp.int32), tab, meta)


def kernel(x, mem, g_mix, g_mem, w_in_moba, w_in_ssd, w_mem_kv, w_out, rel_bias, conv_w, conv_b, dt_bias, a_log, d_skip, g_ssd_out, g_ffn, w_ffn_gate, w_ffn_up, w_ffn_down, w_router, b_router, w_exp_gate, w_exp_up, w_exp_down, g_final):
    bsz, s, d = x.shape
    t = bsz * s
    m_len = mem.shape[1]
    tiles = _choose_tiles(t, bsz * m_len)
    xf = x.reshape(t, d).astype(F32)
    memf = mem.reshape(bsz * m_len, d).astype(F32)

    proj0, = norm_matmul(xf, g_mix[0], [w_in_moba[0].astype(BF16)], [BF16], tiles.proj_rows, [tiles.moba_cols])
    kv0, = norm_matmul(memf, g_mem[0], [w_mem_kv[0].astype(BF16)], [BF16], tiles.mem_rows, [2 * MEM_WIDTH])
    proj0 = proj0.reshape(bsz, s, -1)
    y_tok = moba_attention(proj0, _moba_bias_tables(rel_bias), bsz, s)
    y_mem = memory_attention(proj0, 3 * MOBA_WIDTH, kv0.reshape(bsz, m_len, -1), bsz, s)
    wo = w_out[0].astype(BF16)
    x1 = out_projection(xf, y_tok.reshape(t, -1), y_mem.reshape(t, -1), wo[:MOBA_WIDTH], wo[MOBA_WIDTH:],
                        tiles.out_rows)
    x1 = dense_ffn(x1, g_ffn[0], w_ffn_gate[0].astype(BF16), w_ffn_up[0].astype(BF16),
                   w_ffn_down[0].astype(BF16), tiles.proj_rows, tiles.ffn_cols)

    n_zx = SSD_INNER + SSD_INNER + 2 * SSD_GROUPS * SSD_STATE
    w1 = w_in_ssd[0]
    w_main = jnp.concatenate([w1[:, :n_zx], w1[:, n_zx + SSD_HEADS:]], axis=1).astype(BF16)
    w_dt = jnp.pad(w1[:, n_zx:n_zx + SSD_HEADS].reshape(d, SSD_GROUPS, SSD_GHEADS),
                   ((0, 0), (0, 0), (0, LANES - SSD_GHEADS))).reshape(d, SSD_GROUPS * LANES).astype(BF16)
    proj1, dt_all = norm_matmul(x1, g_mix[1], [w_main, w_dt], [BF16, F32], tiles.proj_rows,
                                [tiles.ssd_cols, SSD_GROUPS * LANES])
    proj1, dt_all = proj1.reshape(bsz, s, -1), dt_all.reshape(bsz, s, -1)
    kv1, = norm_matmul(memf, g_mem[1], [w_mem_kv[1].astype(BF16)], [BF16], tiles.mem_rows, [2 * MEM_WIDTH])
    y_tok = ssd_mixer(proj1, dt_all, *_ssd_group_params(conv_w[0], conv_b[0], dt_bias[0], a_log[0],
                                                         d_skip[0], g_ssd_out[0]), bsz, s)
    y_mem = memory_attention(proj1, n_zx, kv1.reshape(bsz, m_len, -1), bsz, s)
    wo = w_out[1].astype(BF16)
    x2 = out_projection(x1, y_tok.reshape(t, -1), y_mem.reshape(t, -1), wo[:SSD_INNER], wo[SSD_INNER:],
                        tiles.out_rows)

    h, route, runs, counts = moe_router(x2, g_ffn[1], w_router[0], b_router[0], tiles.moe_rows)
    lay = _expert_layout(runs, counts[:, 0], t, tiles.moe_rows)
    xb = moe_dispatch(lay.run_table, lay.meta, route, h, tiles.moe_rows, lay.n_tiles)
    yb = expert_ffn(lay.tile_expert, lay.tile_rows, xb, w_exp_gate[0], w_exp_up[0], w_exp_down[0], tiles.expert_cols)
    out = combine_final(lay.run_table, route, x2, g_final, yb, tiles.moe_rows)
    return out.reshape(bsz, s, d).astype(x.dtype)
```

```python
import functools
import math
from typing import Any, NamedTuple

import jax
import jax.numpy as jnp
from jax import lax
from jax.experimental import pallas as pl
from jax.experimental.pallas import tpu as pltpu

F32 = jnp.float32
BF16 = jnp.bfloat16

D_MODEL = 1024
HD = 128
MOBA_HEADS = 12
MOBA_WIDTH = MOBA_HEADS * HD
MOBA_BLOCK = 256
MOBA_TOPK = 3
MOBA_HEADS_PER_STEP = 2
REL_BUCKETS = 32
REL_MAX_DIST = 128
MEM_HEADS = 4
MEM_WIDTH = MEM_HEADS * HD
SSD_HEADS = 24
SSD_HD = 64
SSD_INNER = SSD_HEADS * SSD_HD
SSD_GROUPS = 4
SSD_STATE = 128
SSD_CONV = 4
SSD_CHUNK = 128
SSD_GROUPS_PER_STEP = 2
SSD_GHEADS = SSD_HEADS // SSD_GROUPS
SSD_GW = SSD_GHEADS * SSD_HD
SSD_CONV_GW = SSD_GW + 2 * SSD_STATE
D_FF = 3584
N_EXPERTS = 8
TOP_K = 2
EPS = 1e-6

LOG2E = math.log2(math.e)
LANES = 128
SUBLANES = 8
VMEM_LIMIT = 48 * 1024 * 1024
EXPERT_ROWS = 1024
EXPERT_ROW_CHUNK = 256


def _dot(a, b):
    return jnp.dot(a, b, preferred_element_type=F32)


def _dot_nt(a, b):
    return lax.dot_general(a, b, (((1,), (1,)), ((), ())), preferred_element_type=F32)


def _split(x, pieces):
    out = []
    for _ in range(pieces):
        p = x.astype(BF16)
        out.append(p)
        x = x - p.astype(F32)
    return out


def _dot_split_l(x, w, pieces):
    return sum(_dot(p, w) for p in _split(x, pieces))


def _dot_split_r(w, x, pieces):
    return sum(_dot(w, p) for p in _split(x, pieces))


def _sigmoid(x):
    return 1.0 / (1.0 + jnp.exp(-x))


def _rms(x, g):
    ms = jnp.mean(x * x, axis=-1, keepdims=True)
    return x * lax.rsqrt(ms + EPS) * g


def _rms_rows_to(h_ref, x_ref, g_ref, rows):
    step = min(rows, 256)
    for r in range(0, rows, step):
        h_ref[r:r + step, :] = _rms(x_ref[r:r + step, :], g_ref[...]).astype(h_ref.dtype)


def _params(*sem):
    return pltpu.CompilerParams(dimension_semantics=sem, vmem_limit_bytes=VMEM_LIMIT)


def _normmm_kernel(x_ref, g_ref, *refs, rc, tns):
    w_refs, o_refs = refs[:len(tns)], refs[len(tns):]
    tm = x_ref.shape[0]

    def norm(r):
        return _rms(x_ref[r * rc:(r + 1) * rc, :], g_ref[...]).astype(BF16)

    def matmuls(r, h):
        for w_ref, o_ref, tn in zip(w_refs, o_refs, tns):
            for j in range(w_ref.shape[1] // tn):
                cols = slice(j * tn, (j + 1) * tn)
                o_ref[r * rc:(r + 1) * rc, cols] = _dot(h, w_ref[:, cols]).astype(o_ref.dtype)

    live = {}
    for step in range(tm // rc + 1):
        if step < tm // rc:
            live[step] = norm(step)
        if step >= 1:
            matmuls(step - 1, live.pop(step - 1))


def norm_matmul(x, g, ws, out_dtypes, tm, tns):
    t, d = x.shape
    return pl.pallas_call(
        functools.partial(_normmm_kernel, rc=min(tm, 256), tns=tuple(tns)),
        grid=(t // tm,),
        in_specs=[pl.BlockSpec((tm, d), lambda i: (i, 0)),
                  pl.BlockSpec((1, d), lambda i: (0, 0))]
        + [pl.BlockSpec(w.shape, lambda i: (0, 0)) for w in ws],
        out_specs=[pl.BlockSpec((tm, w.shape[1]), lambda i: (i, 0)) for w in ws],
        out_shape=[jax.ShapeDtypeStruct((t, w.shape[1]), dt) for w, dt in zip(ws, out_dtypes)],
        compiler_params=_params("parallel"),
        name="norm_matmul",
    )(x, g.reshape(1, d), *ws)


def _moba_kernel(q_ref, k_ref, v_ref, tab_ref, o_ref, vt_ref, km_ref, *, nb, n_sel, heads):
    for hh in range(heads):
        lanes = pl.ds(hh * HD, HD)
        _moba_head(q_ref.at[:, lanes], k_ref.at[:, lanes], v_ref.at[:, lanes], tab_ref.at[hh],
                   o_ref.at[:, lanes], vt_ref.at[hh], km_ref.at[hh], nb=nb, n_sel=n_sel)


def _moba_head(q_ref, k_ref, v_ref, tab_ref, o_ref, vt_ref, km_ref, *, nb, n_sel):
    blk = MOBA_BLOCK
    scale = HD ** -0.5 * LOG2E
    nbp = km_ref.shape[0]
    km_ref[...] = jnp.zeros_like(km_ref)
    for j in range(nb):
        rows = slice(j * blk, (j + 1) * blk)
        vt_ref[:, rows] = v_ref[rows, :].astype(F32).T.astype(BF16)
        km_ref[j:j + 1, :] = jnp.mean(k_ref[rows, :].astype(F32), axis=0, keepdims=True)
    km = km_ref[...]
    km_hi = km.astype(BF16)
    km_lo = (km - km_hi.astype(F32)).astype(BF16)
    q_all = q_ref[...]
    gate = _dot_nt(km_hi, q_all) + _dot_nt(km_lo, q_all)
    sub = lax.broadcasted_iota(jnp.int32, (nbp, blk), 0)
    bias_far = tab_ref[2, 0:1, :]

    def scores(i):
        return _dot_nt(k_ref[0:(i + 1) * blk, :], q_ref[i * blk:(i + 1) * blk, :])

    def softmax(i, s_all):
        rows = slice(i * blk, (i + 1) * blk)
        if i > 0:
            valid = sub < i
            gm = jnp.where(valid, gate[:, rows], -jnp.inf)
            rank = jnp.zeros((nbp, blk), F32)
            for jp in range(i):
                row = gm[jp:jp + 1, :]
                beats = (row > gm) | ((row == gm) & (sub > jp))
                rank = rank + jnp.where(beats, 1.0, 0.0)
            selm = jnp.where(valid & (rank < n_sel), 0.0, -jnp.inf)
        bands = []
        for j in range(i + 1):
            sj = s_all[j * blk:(j + 1) * blk, :] * scale
            if j == i:
                sj = sj + tab_ref[0]
            elif j == i - 1:
                sj = sj + (tab_ref[1] + selm[j:j + 1, :])
            else:
                sj = sj + (bias_far + selm[j:j + 1, :])
            bands.append(sj)
        s = jnp.concatenate(bands, axis=0) if i > 0 else bands[0]
        m = jnp.max(s, axis=0, keepdims=True)
        return jnp.exp2(s - m).astype(BF16)

    def output(i, p):
        nk = (i + 1) * blk
        l = _dot(jnp.ones((SUBLANES, nk), BF16), p)[0:1, :]
        acc = _dot(vt_ref[:, 0:nk], p)
        o_ref[i * blk:(i + 1) * blk, :] = (acc / l).T.astype(o_ref.dtype)

    s_live, p_live = {}, {}
    units = [sorted({u, nb - 1 - u}) for u in range((nb + 1) // 2)]
    for step in range(len(units) + 2):
        if step < len(units):
            for i in units[step]:
                s_live[i] = scores(i)
        if 1 <= step <= len(units):
            for i in units[step - 1]:
                p_live[i] = softmax(i, s_live.pop(i))
        if step >= 2:
            for i in units[step - 2]:
                output(i, p_live.pop(i))


def _t5_bucket_idx(dist):
    n = jnp.maximum(dist, 0)
    max_exact = REL_BUCKETS // 2
    large = max_exact + (jnp.log(jnp.maximum(n, 1).astype(F32) / max_exact)
                         / math.log(REL_MAX_DIST / max_exact)
                         * (REL_BUCKETS - max_exact)).astype(jnp.int32)
    large = jnp.minimum(large, REL_BUCKETS - 1)
    return jnp.where(n < max_exact, n, large)


def _bias_table_kernel(rb_ref, idx_ref, o_ref):
    h = pl.program_id(0)
    for t in range(3):
        idx = idx_ref[t]
        acc = jnp.full(idx.shape, -jnp.inf, F32)
        for b in range(REL_BUCKETS):
            acc = jnp.where(idx == b, rb_ref[h, b], acc)
        o_ref[t] = acc * LOG2E


def _moba_bias_tables(rel_bias):
    blk = MOBA_BLOCK
    loc = jnp.arange(blk)
    d0 = loc[None, :] - loc[:, None]
    idx = jnp.stack([jnp.where(d0 >= 0, _t5_bucket_idx(d0), -1),
                     _t5_bucket_idx(d0 + blk),
                     _t5_bucket_idx(d0 + 2 * blk)]).astype(jnp.int32)
    return pl.pallas_call(
        _bias_table_kernel,
        grid=(MOBA_HEADS,),
        in_specs=[pl.BlockSpec(memory_space=pltpu.SMEM),
                  pl.BlockSpec((3, blk, blk), lambda h: (0, 0, 0))],
        out_specs=pl.BlockSpec((None, 3, blk, blk), lambda h: (h, 0, 0, 0)),
        out_shape=jax.ShapeDtypeStruct((MOBA_HEADS, 3, blk, blk), F32),
        compiler_params=_params("parallel"),
        name="moba_bias_tables",
    )(rel_bias.T.astype(F32), idx)


def moba_attention(proj, tabs, bsz, s):
    assert s % MOBA_BLOCK == 0
    nb = s // MOBA_BLOCK
    nbp = -(-nb // 8) * 8
    n_sel = min(MOBA_TOPK, nb - 1)
    blk = MOBA_BLOCK

    hps = MOBA_HEADS_PER_STEP
    groups = MOBA_HEADS // hps

    def col(off):
        return pl.BlockSpec((None, s, hps * HD), lambda h, b: (b, 0, off + h))

    return pl.pallas_call(
        functools.partial(_moba_kernel, nb=nb, n_sel=n_sel, heads=hps),
        grid=(groups, bsz),
        in_specs=[col(0), col(groups), col(2 * groups),
                  pl.BlockSpec((hps, 3, blk, blk), lambda h, b: (h, 0, 0, 0))],
        out_specs=pl.BlockSpec((None, s, hps * HD), lambda h, b: (b, 0, h)),
        out_shape=jax.ShapeDtypeStruct((bsz, s, MOBA_WIDTH), BF16),
        scratch_shapes=[pltpu.VMEM((hps, HD, s), BF16),
                        pltpu.VMEM((hps, nbp, HD), F32)],
        compiler_params=_params("parallel", "parallel"),
        name="moba_attention",
    )(proj, proj, proj, tabs)


def _memattn_kernel(q_ref, kv_ref, o_ref, *, n_chunks, qc):
    scale = HD ** -0.5 * LOG2E
    ks =[kv_ref[:, h * HD:(h + 1) * HD] for h in range(MEM_HEADS)]
    vts = [kv_ref[:, MEM_WIDTH + h * HD:MEM_WIDTH + (h + 1) * HD].astype(F32).T.astype(BF16)
           for h in range(MEM_HEADS)]
    ones = jnp.ones((SUBLANES, kv_ref.shape[0]), BF16)

    def scores(h, c):
        return _dot_nt(ks[h], q_ref[c * qc:(c + 1) * qc, h * HD:(h + 1) * HD])

    def softmax(s):
        s = s * scale
        return jnp.exp2(s - jnp.max(s, axis=0, keepdims=True)).astype(BF16)

    def output(h, c, p):
        l = _dot(ones, p)[0:1, :]
        o_ref[c * qc:(c + 1) * qc, h * HD:(h + 1) * HD] = (_dot(vts[h], p) / l).T.astype(o_ref.dtype)

    work = [(h, c) for h in range(MEM_HEADS) for c in range(n_chunks)]
    s_live, p_live = {}, {}
    for step in range(len(work) + 2):
        if step < len(work):
            s_live[step] = scores(*work[step])
        if 1 <= step <= len(work):
            p_live[step - 1] = softmax(s_live.pop(step - 1))
        if step >= 2:
            output(*work[step - 2], p_live.pop(step - 2))


def memory_attention(proj, q_col, kv, bsz, s):
    m_len = kv.shape[1]
    qc = 256
    assert q_col % MEM_WIDTH == 0
    return pl.pallas_call(
        functools.partial(_memattn_kernel, n_chunks=s // qc, qc=qc),
        grid=(bsz,),
        in_specs=[pl.BlockSpec((None, s, MEM_WIDTH), lambda b: (b, 0, q_col // MEM_WIDTH)),
                  pl.BlockSpec((None, m_len, 2 * MEM_WIDTH), lambda b: (b, 0, 0))],
        out_specs=pl.BlockSpec((None, s, MEM_WIDTH), lambda b: (b, 0, 0)),
        out_shape=jax.ShapeDtypeStruct((bsz, s, MEM_WIDTH), BF16),
        compiler_params=_params("parallel"),
        name="memory_attention",
    )(proj, kv)


def _outproj_kernel(x_ref, ya_ref, yb_ref, w_ref, o_ref):
    ka = ya_ref.shape[1]
    o_ref[...] = x_ref[...] + _dot(ya_ref[...], w_ref[0:ka, :]) + _dot(yb_ref[...], w_ref[ka:, :])


def out_projection(x, ya, yb, w, tm):
    t, d = x.shape
    ka, kb = ya.shape[1], yb.shape[1]
    return pl.pallas_call(
        _outproj_kernel,
        grid=(t // tm,),
        in_specs=[pl.BlockSpec((tm, d), lambda i: (i, 0)),
                  pl.BlockSpec((tm, ka), lambda i: (i, 0)),
                  pl.BlockSpec((tm, kb), lambda i: (i, 0)),
                  pl.BlockSpec((ka + kb, d), lambda i: (0, 0))],
        out_specs=pl.BlockSpec((tm, d), lambda i: (i, 0)),
        out_shape=jax.ShapeDtypeStruct((t, d), F32),
        compiler_params=_params("parallel"),
        name="out_projection",
    )(x, ya, yb, w)


def _swiglu_accumulate(h, wg, wu, wd, acc_ref):
    a = _dot(h, wg)
    u = _dot(h, wu)
    acc_ref[...] += _dot((a * _sigmoid(a) * u).astype(BF16), wd)


def _ffn_kernel(x_ref, g_ref, wg_ref, wu_ref, wd_ref, o_ref, h_ref, *, tm):
    @pl.when(pl.program_id(1) == 0)
    def _():
        _rms_rows_to(h_ref, x_ref, g_ref, tm)
        o_ref[...] = x_ref[...]

    _swiglu_accumulate(h_ref[...], wg_ref[...], wu_ref[...], wd_ref[...], o_ref)


def dense_ffn(x, g, wg, wu, wd, tm, tf):
    t, d = x.shape
    ff = wg.shape[1]
    return pl.pallas_call(
        functools.partial(_ffn_kernel, tm=tm),
        grid=(t // tm, ff // tf),
        in_specs=[pl.BlockSpec((tm, d), lambda i, f: (i, 0)),
                  pl.BlockSpec((1, d), lambda i, f: (0, 0)),
                  pl.BlockSpec((d, tf), lambda i, f: (0, f)),
                  pl.BlockSpec((d, tf), lambda i, f: (0, f)),
                  pl.BlockSpec((tf, d), lambda i, f: (f, 0))],
        out_specs=pl.BlockSpec((tm, d), lambda i, f: (i, 0)),
        out_shape=jax.ShapeDtypeStruct((t, d), F32),
        scratch_shapes=[pltpu.VMEM((tm, d), BF16)],
        compiler_params=_params("parallel", "arbitrary"),
        name="dense_ffn",
    )(x, g.reshape(1, d), wg, wu, wd)


def _ssd_kernel(z_ref, xs_ref, b_ref, c_ref, dt_ref, cw_ref, cb_ref, dtb_ref, alog_ref, dsk_ref,
                gout_ref, exp_ref, o_ref, h_ref, *, nc, unroll, groups):
    gw, n = SSD_GW, SSD_STATE
    chunks = []
    for sg in range(groups):
        wide, narrow, lanes = pl.ds(sg * gw, gw), pl.ds(sg * n, n), pl.ds(sg * LANES, LANES)
        chunks.append(_ssd_group_chunk(
            z_ref.at[:, wide], xs_ref.at[:, wide], b_ref.at[:, narrow], c_ref.at[:, narrow], dt_ref.at[:, lanes],
            cw_ref.at[sg], cb_ref.at[sg], dtb_ref.at[sg], alog_ref.at[sg], dsk_ref.at[sg], gout_ref.at[sg],
            exp_ref, o_ref.at[:, wide], h_ref.at[sg]))

    def body(it, carry):
        work = [(sg, it * unroll + u) for u in range(unroll) for sg in range(groups)]
        live = {}
        for step in range(len(work) + 1):
            if step < len(work):
                sg, c = work[step]
                live[step] = chunks[sg][0](c)
            if step >= 1:
                chunks[work[step - 1][0]][1](*live.pop(step - 1))
        return carry

    lax.fori_loop(0, nc // unroll, body, 0)


def _ssd_group_chunk(z_ref, xs_ref, b_ref, c_ref, dt_ref, cw_ref, cb_ref, dtb_ref, alog_ref, dsk_ref,
                     gout_ref, exp_ref, o_ref, h_ref):
    L = SSD_CHUNK
    gw = SSD_GW
    n = SSD_STATE
    h_ref[...] = jnp.zeros(h_ref.shape, h_ref.dtype)
    a_neg = -jnp.exp(alog_ref[...])
    causal = (lax.broadcasted_iota(jnp.int32, (L, L), 0) >= lax.broadcasted_iota(jnp.int32, (L, L), 1))
    tri = jnp.where(causal, 1.0, 0.0).astype(BF16)
    first_half = lax.broadcasted_iota(jnp.int32, (1, 2 * SSD_HD), 1) < SSD_HD
    expm = exp_ref[...]
    srcs = (xs_ref, b_ref, c_ref)
    sh_r = lax.broadcasted_iota(jnp.int32, (L, 2 * L), 0)
    sh_c = lax.broadcasted_iota(jnp.int32, (L, 2 * L), 1)
    shifts = [jnp.where(sh_c == sh_r + (L - s), 1.0, 0.0).astype(BF16) for s in range(1, SSD_CONV)]

    def local(c):
        r0 = pl.multiple_of(c * L, L)
        rq = pl.multiple_of(jnp.maximum(r0 - L, 0), L)
        cur = jnp.concatenate([ref[pl.ds(r0, L), :] for ref in srcs], axis=1)
        prev = jnp.concatenate([ref[pl.ds(rq, L), :] for ref in srcs], axis=1)
        prev = jnp.where(c > 0, prev, jnp.zeros_like(prev))
        ext = jnp.concatenate([prev, cur], axis=0)
        conv = cb_ref[...] + cw_ref[SSD_CONV - 1:SSD_CONV, :] * cur.astype(F32)
        for s in range(1, SSD_CONV):
            k = SSD_CONV - 1 - s
            conv = conv + cw_ref[k:k + 1, :] * _dot(shifts[s - 1], ext)
        act = conv * _sigmoid(conv)
        xs = act[:, 0:gw]
        bm = act[:, gw:gw + n]
        cm = act[:, gw + n:gw + 2 * n]

        dtr = dt_ref[pl.ds(r0, L), :] + dtb_ref[...]
        dt = jnp.maximum(dtr, 0.0) + jnp.log(1.0 + jnp.exp(-jnp.abs(dtr)))
        la = dt * a_neg
        cs = _dot_split_r(tri, la, 3)
        cs_t = cs.T
        dt_t = dt.T
        bm16 = bm.astype(BF16)
        cm16 = cm.astype(BF16)
        scores = _dot_nt(cm16, bm16)
        from_start = _dot_split_l(jnp.exp(cs), expm, 2)
        w_end = dt * jnp.exp(cs[L - 1:L, :] - cs)
        xs_t = xs.T
        return r0, xs, bm, cm16, cs, cs_t, dt_t, scores, from_start, w_end, xs_t

    def carried(r0, xs, bm, cm16, cs, cs_t, dt_t, scores, from_start, w_end, xs_t):
        y = _dot_nt(cm16, h_ref[...].astype(BF16)) * from_start
        y_in = []
        for e in range(SSD_GHEADS):
            diff = cs[:, e:e + 1] - cs_t[e:e + 1, :]
            dec = jnp.exp(jnp.where(causal, diff, -jnp.inf))
            mm = (scores * dec * dt_t[e:e + 1, :]).astype(BF16)
            if e % 2 == 0:
                mm_even = mm
            else:
                xp = xs[:, (e - 1) * SSD_HD:(e + 1) * SSD_HD]
                rhs = jnp.concatenate([jnp.where(first_half, xp, 0.0), jnp.where(first_half, 0.0, xp)], axis=0)
                y_in.append(_dot(jnp.concatenate([mm_even, mm], axis=1), rhs.astype(BF16)))
            bw = (bm * w_end[:, e:e + 1]).astype(BF16)
            st = _dot(xs_t[e * SSD_HD:(e + 1) * SSD_HD, :].astype(BF16), bw)
            cdec = jnp.exp(cs[L - 1:L, e:e + 1])
            hs = slice(e * SSD_HD, (e + 1) * SSD_HD)
            h_ref[hs, :] = h_ref[hs, :] * cdec + st
        y = y + jnp.concatenate(y_in, axis=1) + xs * dsk_ref[...]
        z = z_ref[pl.ds(r0, L), :].astype(F32)
        u = y * (z * _sigmoid(z))
        ms = jnp.mean(u * u, axis=-1, keepdims=True)
        o_ref[pl.ds(r0, L), :] = (u * lax.rsqrt(ms + EPS) * gout_ref[...]).astype(o_ref.dtype)

    return local, carried


def ssd_mixer(proj, dt_all, cw_g, cb_g, dtb_g, alog_g, dsk_g, gout_g, expand, bsz, s):
    assert s % SSD_CHUNK == 0
    nc = s // SSD_CHUNK
    gps = SSD_GROUPS_PER_STEP
    gw = SSD_GW * gps
    nw = SSD_STATE * gps
    z_blk = 0
    xs_blk = SSD_INNER // gw
    b_blk = 2 * SSD_INNER // nw
    c_blk = b_blk + SSD_GROUPS // gps

    def seq(width, off):
        return pl.BlockSpec((None, s, width), lambda b, g: (b, 0, off + g))

    def par(rows, width):
        return pl.BlockSpec((gps, rows, width), lambda b, g: (g, 0, 0))

    return pl.pallas_call(
        functools.partial(_ssd_kernel, nc=nc, unroll=_pick(nc, (4, 2, 1)), groups=gps),
        grid=(bsz, SSD_GROUPS // gps),
        in_specs=[seq(gw, z_blk), seq(gw, xs_blk), seq(nw, b_blk), seq(nw, c_blk),
                  seq(LANES * gps, 0),
                  par(SSD_CONV, SSD_CONV_GW), par(1, SSD_CONV_GW), par(1, LANES), par(1, LANES),
                  par(1, SSD_GW), par(1, SSD_GW),
                  pl.BlockSpec((LANES, SSD_GW), lambda b, g: (0, 0))],
        out_specs=seq(gw, 0),
        out_shape=jax.ShapeDtypeStruct((bsz, s, SSD_INNER), BF16),
        scratch_shapes=[pltpu.VMEM((gps, SSD_GW, SSD_STATE), F32)],
        compiler_params=_params("parallel", "parallel"),
        name="ssd_mixer",
    )(proj, proj, proj, proj, dt_all, cw_g, cb_g, dtb_g, alog_g, dsk_g, gout_g, expand)


def _ssd_group_params(conv_w, conv_b, dt_bias, a_log, d_skip, g_out):
    g, gh, gw, n = SSD_GROUPS, SSD_GHEADS, SSD_GW, SSD_STATE

    def conv_cols(a):
        xs = a[..., :SSD_INNER].reshape(a.shape[:-1] + (g, gw))
        bb = a[..., SSD_INNER:SSD_INNER + g * n].reshape(a.shape[:-1] + (g, n))
        cc = a[..., SSD_INNER + g * n:].reshape(a.shape[:-1] + (g, n))
        return jnp.moveaxis(jnp.concatenate([xs, bb, cc], axis=-1), -2, 0)

    cw_g = conv_cols(conv_w.astype(F32))
    cb_g = conv_cols(conv_b.astype(F32)[None, :])

    def per_head(a):
        return jnp.pad(a.astype(F32).reshape(g, 1, gh), ((0, 0), (0, 0), (0, LANES - gh)))

    dsk_g = jnp.repeat(d_skip.astype(F32), SSD_HD).reshape(g, 1, gw)
    gout_g = g_out.astype(F32).reshape(g, 1, gw)
    expand = (jnp.arange(LANES)[:, None] == (jnp.arange(gw)[None, :] // SSD_HD)).astype(BF16)
    return cw_g, cb_g, per_head(dt_bias), per_head(a_log), dsk_g, gout_g, expand


def _router_kernel(x_ref, g_ref, wrt_ref, br_ref, h_ref, route_ref, runs_ref, cnt_ref, carry_ref, *, tm):
    @pl.when(pl.program_id(0) == 0)
    def _():
        carry_ref[...] = jnp.zeros_like(carry_ref)

    ne = N_EXPERTS
    hn = _rms(x_ref[...], g_ref[...])
    h_hi = hn.astype(BF16)
    h_ref[...] = h_hi
    h_lo = (hn - h_hi.astype(F32)).astype(BF16)
    wr = wrt_ref[...]
    w_hi = wr.astype(BF16)
    w_lo = (wr - w_hi.astype(F32)).astype(BF16)
    logits = _dot_nt(w_hi, h_hi) + _dot_nt(w_hi, h_lo) + _dot_nt(w_lo, h_hi) + br_ref[:, 0:1]
    sub = lax.broadcasted_iota(jnp.int32, (ne, tm), 0)
    l1 = jnp.max(logits, axis=0, keepdims=True)
    i1 = jnp.min(jnp.where(logits == l1, sub, ne), axis=0, keepdims=True)
    rest = jnp.where(sub == i1, -jnp.inf, logits)
    l2 = jnp.max(rest, axis=0, keepdims=True)
    i2 = jnp.min(jnp.where(rest == l2, sub, ne), axis=0, keepdims=True)
    e2 = jnp.exp(l2 - l1)
    g1 = 1.0 / (1.0 + e2)
    g2 = e2 / (1.0 + e2)
    sel = jnp.where((sub == i1) | (sub == i2), 1.0, 0.0)
    before = (lax.broadcasted_iota(jnp.int32, (tm, tm), 0) < lax.broadcasted_iota(jnp.int32, (tm, tm), 1))
    prefix = _dot(sel.astype(BF16), jnp.where(before, 1.0, 0.0).astype(BF16))
    lane = lax.broadcasted_iota(jnp.int32, (ne, LANES), 1)
    subl = lax.broadcasted_iota(jnp.int32, (ne, LANES), 0)
    n_e = (jnp.sum(sel, axis=1, keepdims=True) + jnp.zeros((ne, LANES), F32)).astype(jnp.int32)
    n_pad = (n_e + (SUBLANES - 1)) & (-SUBLANES)
    off = jnp.zeros((ne, LANES), jnp.int32)
    for j in range(ne - 1):
        off = off + jnp.where(subl > j, n_pad[j:j + 1, :], 0)
    slot = prefix + off[:, 0:1].astype(F32)
    p1 = jnp.sum(jnp.where(sub == i1, slot, 0.0), axis=0, keepdims=True)
    p2 = jnp.sum(jnp.where(sub == i2, slot, 0.0), axis=0, keepdims=True)
    route_ref[...] = jnp.where(sub == 0, p1, jnp.where(sub == 1, p2, jnp.where(
        sub == 2, g1, jnp.where(sub == 3, g2, 0.0))))
    carry = carry_ref[...]
    runs_ref[...] = jnp.where(lane == 0, off, jnp.where(lane == 1, carry, jnp.where(lane == 2, n_pad, 0)))
    carry_ref[...] = carry + n_pad
    cnt_ref[...] = carry + n_pad


def moe_router(x, g, w_router, b_router, tm):
    t, d = x.shape
    ne = N_EXPERTS
    wrt = w_router.astype(F32).T
    br = jnp.broadcast_to(b_router.astype(F32)[:, None], (ne, LANES))
    return pl.pallas_call(
        functools.partial(_router_kernel, tm=tm),
        grid=(t // tm,),
        in_specs=[pl.BlockSpec((tm, d), lambda i: (i, 0)),
                  pl.BlockSpec((1, d), lambda i: (0, 0)),
                  pl.BlockSpec((ne, d), lambda i: (0, 0)),
                  pl.BlockSpec((ne, LANES), lambda i: (0, 0))],
        out_specs=[pl.BlockSpec((tm, d), lambda i: (i, 0)),
                   pl.BlockSpec((ne, tm), lambda i: (0, i)),
                   pl.BlockSpec((ne, LANES), lambda i: (i, 0)),
                   pl.BlockSpec((ne, LANES), lambda i: (0, 0))],
        out_shape=[jax.ShapeDtypeStruct((t, d), BF16),
                   jax.ShapeDtypeStruct((ne, t), F32),
                   jax.ShapeDtypeStruct((t // tm * ne, LANES), jnp.int32),
                   jax.ShapeDtypeStruct((ne, LANES), jnp.int32)],
        scratch_shapes=[pltpu.VMEM((ne, LANES), jnp.int32)],
        compiler_params=_params("arbitrary"),
        name="moe_router",
    )(x, g.reshape(1, d), wrt, br)


def _pieces(length, lo_bit, hi_bit):
    for b in range(lo_bit, hi_bit):
        n = 1 << b
        yield n, length & (n - 1), ((length >> b) & 1) == 1


def _run_copies(tab_ref, tile, tile_ref, buf_ref, sem, tm, to_buf, wait):
    for e in range(N_EXPERTS):
        base = (tile * N_EXPERTS + e) * 3
        off, dst, rows = tab_ref[base], tab_ref[base + 1], tab_ref[base + 2]
        for n, lo, present in _pieces(rows, SUBLANES.bit_length() - 1, tm.bit_length()):
            in_tile = tile_ref.at[pl.ds(pl.multiple_of(off + lo, SUBLANES), n), :]
            in_buf = buf_ref.at[pl.ds(pl.multiple_of(dst + lo, SUBLANES), n), :]
            cp = pltpu.make_async_copy(in_tile, in_buf, sem) if to_buf else pltpu.make_async_copy(in_buf, in_tile, sem)

            @pl.when(present)
            def _():
                if wait:
                    cp.wait()
                else:
                    cp.start()


def _dispatch_kernel(tab_ref, meta_ref, route_ref, h_ref, xb_ref, srt_ref, zero_ref, sem, zsem, *, tm, n_tiles):
    i = pl.program_id(0)
    nt = pl.num_programs(0)
    slot = i % 2
    rs = srt_ref.shape[1]

    def runs(tile, slot, wait):
        _run_copies(tab_ref, tile, srt_ref.at[slot], xb_ref, sem.at[slot], tm, True, wait)

    @pl.when(i >= 2)
    def _():
        runs(i - 2, slot, True)

    pos = route_ref[0:2, :].astype(jnp.int32)
    row = lax.broadcasted_iota(jnp.int32, (rs, tm), 0)
    onehot = jnp.where((row == pos[0:1, :]) | (row == pos[1:2, :]), 1.0, 0.0).astype(BF16)
    srt_ref[slot] = _dot(onehot, h_ref[...])
    runs(i, slot, False)

    @pl.when(i == nt - 1)
    def _():
        zero_ref[...] = jnp.zeros_like(zero_ref)
        zrows = zero_ref.shape[0]
        per = EXPERT_ROWS // zrows
        first_unused = meta_ref[2 * N_EXPERTS] * per
        for wait in (False, True):
            def go(cp, cond):
                @pl.when(cond)
                def _():
                    if wait:
                        cp.wait()
                    else:
                        cp.start()

            for e in range(N_EXPERTS):
                c = meta_ref[e]
                first = meta_ref[N_EXPERTS + e]
                tail = (-c) & (EXPERT_ROWS - 1)
                for n, lo, present in _pieces(tail, SUBLANES.bit_length() - 1, EXPERT_ROWS.bit_length() - 1):
                    row0 = pl.multiple_of(first + c + lo, SUBLANES)
                    go(pltpu.make_async_copy(zero_ref.at[pl.ds(0, n), :], xb_ref.at[pl.ds(row0, n), :], zsem), present)
            for j in range(N_EXPERTS * per):
                row0 = pl.multiple_of(jnp.minimum(first_unused + j, n_tiles * per - 1) * zrows, zrows)
                go(pltpu.make_async_copy(zero_ref, xb_ref.at[pl.ds(row0, zrows), :], zsem),
                   first_unused + j < n_tiles * per)

        @pl.when(i >= 1)
        def _():
            runs(i - 1, 1 - slot, True)

        runs(i, slot, True)


def moe_dispatch(tab, meta, route, h, tm, n_tiles):
    t, d = h.shape
    ne = route.shape[0]
    rs = TOP_K * tm + LANES
    return pl.pallas_call(
        functools.partial(_dispatch_kernel, tm=tm, n_tiles=n_tiles),
        grid=(t // tm,),
        in_specs=[pl.BlockSpec(memory_space=pltpu.SMEM),
                  pl.BlockSpec(memory_space=pltpu.SMEM),
                  pl.BlockSpec((ne, tm), lambda i: (0, i)),
                  pl.BlockSpec((tm, d), lambda i: (i, 0))],
        out_specs=pl.BlockSpec(memory_space=pl.ANY),
        out_shape=jax.ShapeDtypeStruct((n_tiles * EXPERT_ROWS, d), F32),
        scratch_shapes=[pltpu.VMEM((2, rs, d), F32),
                        pltpu.VMEM((EXPERT_ROWS // 2, d), F32),
                        pltpu.SemaphoreType.DMA((2,)), pltpu.SemaphoreType.DMA(())],
        compiler_params=_params("arbitrary"),
        name="moe_dispatch",
    )(tab, meta, route, h)


def _expert_kernel(te_ref, nr_ref, xb_ref, wg_ref, wu_ref, wd_ref, o_ref):
    rows = nr_ref[pl.program_id(0)]
    r, rc = EXPERT_ROWS, EXPERT_ROW_CHUNK

    @pl.when(pl.program_id(1) == 0)
    def _():
        o_ref[...] = jnp.zeros_like(o_ref)

    @pl.when(rows == r)
    def _():
        _swiglu_accumulate(xb_ref[...].astype(BF16), wg_ref[...].astype(BF16), wu_ref[...].astype(BF16),
                           wd_ref[...].astype(BF16), o_ref)

    @pl.when(jnp.logical_and(rows > 0, rows < r))
    def _():
        wg, wu, wd = wg_ref[...].astype(BF16), wu_ref[...].astype(BF16), wd_ref[...].astype(BF16)
        for j in range(r // rc):
            @pl.when(j * rc < rows)
            def _():
                _swiglu_accumulate(xb_ref[j * rc:(j + 1) * rc, :].astype(BF16), wg, wu, wd,
                                   o_ref.at[j * rc:(j + 1) * rc, :])


def expert_ffn(tile_expert, tile_rows, xb, wg, wu, wd, tf):
    n_rows, d = xb.shape
    ff = wg.shape[2]
    r = EXPERT_ROWS
    nf = ff // tf

    def wcol(c, f, te, nr):
        return (te[c], 0, jnp.where(nr[c] > 0, f, nf - 1))

    def wrow(c, f, te, nr):
        return (te[c], jnp.where(nr[c] > 0, f, nf - 1), 0)

    grid_spec = pltpu.PrefetchScalarGridSpec(
        num_scalar_prefetch=2,
        grid=(n_rows // r, nf),
        in_specs=[pl.BlockSpec((r, d), lambda c, f, te, nr: (c, 0)),
                  pl.BlockSpec((None, d, tf), wcol),
                  pl.BlockSpec((None, d, tf), wcol),
                  pl.BlockSpec((None, tf, d), wrow)],
        out_specs=pl.BlockSpec((r, d), lambda c, f, te, nr: (c, 0)),
    )
    return pl.pallas_call(
        _expert_kernel,
        grid_spec=grid_spec,
        out_shape=jax.ShapeDtypeStruct((n_rows, d), F32),
        compiler_params=_params("parallel", "arbitrary"),
        name="expert_ffn",
    )(tile_expert, tile_rows, xb, wg, wu, wd)


def _combine_kernel(tab_ref, route_ref, x_ref, g_ref, yb_ref, o_ref, srt_ref, sem, *, tm):
    i = pl.program_id(0)
    nt = pl.num_programs(0)
    slot = i % 2
    rs = srt_ref.shape[1]

    def runs(tile, slot, wait):
        _run_copies(tab_ref, tile, srt_ref.at[slot], yb_ref, sem.at[slot], tm, False, wait)

    @pl.when(i == 0)
    def _():
        srt_ref[...] = jnp.zeros_like(srt_ref)
        runs(0, 0, False)

    @pl.when(i + 1 < nt)
    def _():
        runs(i + 1, 1 - slot, False)

    runs(i, slot, True)
    aux = route_ref[...]
    aux_t = jnp.concatenate([aux, jnp.zeros((LANES - aux.shape[0], tm), F32)], axis=0).T
    srt = srt_ref[slot].astype(BF16)
    col = lax.broadcasted_iota(jnp.int32, (tm, rs), 1)
    ys = []
    for k in range(TOP_K):
        onehot = jnp.where(col == aux_t[:, k:k + 1].astype(jnp.int32), 1.0, 0.0).astype(BF16)
        ys.append(aux_t[:, TOP_K + k:TOP_K + k + 1] * _dot(onehot, srt))
    o_ref[...] = _rms(x_ref[...] + (ys[0] + ys[1]), g_ref[...])


def combine_final(tab, route, x, g, yb, tm):
    t, d = x.shape
    ne = route.shape[0]
    rs = TOP_K * tm + LANES
    return pl.pallas_call(
        functools.partial(_combine_kernel, tm=tm),
        grid=(t // tm,),
        in_specs=[pl.BlockSpec(memory_space=pltpu.SMEM),
                  pl.BlockSpec((ne, tm), lambda i: (0, i)),
                  pl.BlockSpec((tm, d), lambda i: (i, 0)),
                  pl.BlockSpec((1, d), lambda i: (0, 0)),
                  pl.BlockSpec(memory_space=pl.ANY)],
        out_specs=pl.BlockSpec((tm, d), lambda i: (i, 0)),
        out_shape=jax.ShapeDtypeStruct((t, d), F32),
        scratch_shapes=[pltpu.VMEM((2, rs, d), F32), pltpu.SemaphoreType.DMA((2,))],
        compiler_params=_params("arbitrary"),
        name="combine_final",
    )(tab, route, x, g.reshape(1, d), yb)


def _pick(n, prefs):
    for p in prefs:
        if n % p == 0:
            return p
    return n


class _Tiles(NamedTuple):
    proj_rows: int
    out_rows: int
    mem_rows: int
    moe_rows: int
    moba_cols: int
    ssd_cols: int
    ffn_cols: int
    expert_cols: int


def _choose_tiles(t, mem_rows):
    return _Tiles(proj_rows=_pick(t, (512, 256)), out_rows=_pick(t, (1024, 512, 256)),
                  mem_rows=_pick(mem_rows, (1024, 512, 256)), moe_rows=_pick(t, (512, 256)),
                  moba_cols=1280, ssd_cols=2048, ffn_cols=1792, expert_cols=512)


class _ExpertLayout(NamedTuple):
    n_tiles: int
    tile_expert: Any
    tile_rows: Any
    run_table: Any
    meta: Any


def _expert_layout(runs, cnt, t, tm):
    r, rc = EXPERT_ROWS, EXPERT_ROW_CHUNK
    nt = t // tm
    n_tiles = -(-(t * TOP_K + nt * N_EXPERTS * (SUBLANES - 1)) // r) + N_EXPERTS
    tiles_per = (cnt + r - 1) // r
    ends = jnp.cumsum(tiles_per)
    first_row = (ends - tiles_per) * r
    n_used = ends[-1].astype(jnp.int32)
    step = jnp.minimum(jnp.arange(n_tiles, dtype=jnp.int32), n_used - 1)
    tile_expert = jnp.sum(step[:, None] >= ends[None, :], axis=1).astype(jnp.int32)
    rows_left = jnp.sum(jnp.where(tile_expert[:, None] == jnp.arange(N_EXPERTS)[None, :],
                                  (cnt + first_row)[None, :], 0), axis=1) - step * r
    tile_rows = jnp.where(jnp.arange(n_tiles) < n_used, jnp.clip((rows_left + rc - 1) // rc * rc, 0, r), 0)
    tab = runs.reshape(nt, N_EXPERTS, LANES)[:, :, :3]
    tab = tab.at[:, :, 1].add(first_row[None, :]).reshape(-1).astype(jnp.int32)
    meta = jnp.concatenate([cnt, first_row, n_used.reshape(1)]).astype(jnp.int32)
    return _ExpertLayout(n_tiles, tile_expert, tile_rows.astype(jnp.int32), tab, meta)


def kernel(x, mem, g_mix, g_mem, w_in_moba, w_in_ssd, w_mem_kv, w_out, rel_bias, conv_w, conv_b, dt_bias, a_log, d_skip, g_ssd_out, g_ffn, w_ffn_gate, w_ffn_up, w_ffn_down, w_router, b_router, w_exp_gate, w_exp_up, w_exp_down, g_final):
    bsz, s, d = x.shape
    t = bsz * s
    m_len = mem.shape[1]
    tiles = _choose_tiles(t, bsz * m_len)
    xf = x.reshape(t, d).astype(F32)
    memf = mem.reshape(bsz * m_len, d).astype(F32)

    proj0, = norm_matmul(xf, g_mix[0], [w_in_moba[0].astype(BF16)], [BF16], tiles.proj_rows, [tiles.moba_cols])
    kv0, = norm_matmul(memf, g_mem[0], [w_mem_kv[0].astype(BF16)], [BF16], tiles.mem_rows, [2 * MEM_WIDTH])
    proj0 = proj0.reshape(bsz, s, -1)
    y_tok = moba_attention(proj0, _moba_bias_tables(rel_bias), bsz, s)
    y_mem = memory_attention(proj0, 3 * MOBA_WIDTH, kv0.reshape(bsz, m_len, -1), bsz, s)
    x1 = out_projection(xf, y_tok.reshape(t, -1), y_mem.reshape(t, -1), w_out[0].astype(BF16), tiles.out_rows)
    x1 = dense_ffn(x1, g_ffn[0], w_ffn_gate[0].astype(BF16), w_ffn_up[0].astype(BF16),
                   w_ffn_down[0].astype(BF16), tiles.proj_rows, tiles.ffn_cols)

    n_zx = SSD_INNER + SSD_INNER + 2 * SSD_GROUPS * SSD_STATE
    w1 = w_in_ssd[0]
    w_zx = w1[:, :n_zx].astype(BF16)
    w_qm = w1[:, n_zx + SSD_HEADS:].astype(BF16)
    w_dt = jnp.pad(w1[:, n_zx:n_zx + SSD_HEADS].reshape(d, SSD_GROUPS, SSD_GHEADS),
                   ((0, 0), (0, 0), (0, LANES - SSD_GHEADS))).reshape(d, SSD_GROUPS * LANES).astype(BF16)
    proj1, q_mem, dt_all = norm_matmul(x1, g_mix[1], [w_zx, w_qm, w_dt], [BF16, BF16, F32], tiles.proj_rows,
                                       [tiles.ssd_cols, MEM_WIDTH, SSD_GROUPS * LANES])
    proj1, q_mem, dt_all = (a.reshape(bsz, s, -1) for a in (proj1, q_mem, dt_all))
    kv1, = norm_matmul(memf, g_mem[1], [w_mem_kv[1].astype(BF16)], [BF16], tiles.mem_rows, [2 * MEM_WIDTH])
    y_tok = ssd_mixer(proj1, dt_all, *_ssd_group_params(conv_w[0], conv_b[0], dt_bias[0], a_log[0],
                                                         d_skip[0], g_ssd_out[0]), bsz, s)
    y_mem = memory_attention(q_mem, 0, kv1.reshape(bsz, m_len, -1), bsz, s)
    x2 = out_projection(x1, y_tok.reshape(t, -1), y_mem.reshape(t, -1), w_out[1].astype(BF16), tiles.out_rows)

    h, route, runs, counts = moe_router(x2, g_ffn[1], w_router[0], b_router[0], tiles.moe_rows)
    lay = _expert_layout(runs, counts[:, 0], t, tiles.moe_rows)
    xb = moe_dispatch(lay.run_table, lay.meta, route, h, tiles.moe_rows, lay.n_tiles)
    yb = expert_ffn(lay.tile_expert, lay.tile_rows, xb, w_exp_gate[0], w_exp_up[0], w_exp_down[0], tiles.expert_cols)
    out = combine_final(lay.run_table, route, x2, g_final, yb, tiles.moe_rows)
    return out.reshape(bsz, s, d).astype(x.dtype)
```

```python
import functools
import math
from typing import Any, NamedTuple

import jax
import jax.numpy as jnp
from jax import lax
from jax.experimental import pallas as pl
from jax.experimental.pallas import tpu as pltpu

F32 = jnp.float32
BF16 = jnp.bfloat16

D_MODEL = 1024
HD = 128
MOBA_HEADS = 12
MOBA_WIDTH = MOBA_HEADS * HD
MOBA_BLOCK = 256
MOBA_TOPK = 3
MOBA_HEADS_PER_STEP = 2
REL_BUCKETS = 32
REL_MAX_DIST = 128
MEM_HEADS = 4
MEM_WIDTH = MEM_HEADS * HD
SSD_HEADS = 24
SSD_HD = 64
SSD_INNER = SSD_HEADS * SSD_HD
SSD_GROUPS = 4
SSD_STATE = 128
SSD_CONV = 4
SSD_CHUNK = 128
SSD_GROUPS_PER_STEP = 2
SSD_GHEADS = SSD_HEADS // SSD_GROUPS
SSD_GW = SSD_GHEADS * SSD_HD
SSD_CONV_GW = SSD_GW + 2 * SSD_STATE
D_FF = 3584
N_EXPERTS = 8
TOP_K = 2
EPS = 1e-6

LOG2E = math.log2(math.e)
LANES = 128
SUBLANES = 8
VMEM_LIMIT = 48 * 1024 * 1024
EXPERT_ROWS = 1024
EXPERT_ROW_CHUNK = 256
EXPERT_HIDDEN_PER_STEP = 2


def _dot(a, b):
    return jnp.dot(a, b, preferred_element_type=F32)


def _dot_nt(a, b):
    return lax.dot_general(a, b, (((1,), (1,)), ((), ())), preferred_element_type=F32)


def _split(x, pieces):
    out = []
    for _ in range(pieces):
        p = x.astype(BF16)
        out.append(p)
        x = x - p.astype(F32)
    return out


def _dot_split_l(x, w, pieces):
    return sum(_dot(p, w) for p in _split(x, pieces))


def _dot_split_r(w, x, pieces):
    return sum(_dot(w, p) for p in _split(x, pieces))


def _sigmoid(x):
    return 1.0 / (1.0 + jnp.exp(-x))


def _rms(x, g):
    ms = jnp.mean(x * x, axis=-1, keepdims=True)
    return x * lax.rsqrt(ms + EPS) * g


def _rms_rows_to(h_ref, x_ref, g_ref, rows):
    step = min(rows, 256)
    for r in range(0, rows, step):
        h_ref[r:r + step, :] = _rms(x_ref[r:r + step, :], g_ref[...]).astype(h_ref.dtype)


def _params(*sem):
    return pltpu.CompilerParams(dimension_semantics=sem, vmem_limit_bytes=VMEM_LIMIT)


def _normmm_kernel(x_ref, g_ref, *refs, rc, tns):
    w_refs, o_refs = refs[:len(tns)], refs[len(tns):]
    tm = x_ref.shape[0]

    def norm(r):
        return _rms(x_ref[r * rc:(r + 1) * rc, :], g_ref[...]).astype(BF16)

    def matmuls(r, h):
        for w_ref, o_ref, tn in zip(w_refs, o_refs, tns):
            for j in range(w_ref.shape[1] // tn):
                cols = slice(j * tn, (j + 1) * tn)
                o_ref[r * rc:(r + 1) * rc, cols] = _dot(h, w_ref[:, cols]).astype(o_ref.dtype)

    live = {}
    for step in range(tm // rc + 1):
        if step < tm // rc:
            live[step] = norm(step)
        if step >= 1:
            matmuls(step - 1, live.pop(step - 1))


def norm_matmul(x, g, ws, out_dtypes, tm, tns):
    t, d = x.shape
    return pl.pallas_call(
        functools.partial(_normmm_kernel, rc=min(tm, 256), tns=tuple(tns)),
        grid=(t // tm,),
        in_specs=[pl.BlockSpec((tm, d), lambda i: (i, 0)),
                  pl.BlockSpec((1, d), lambda i: (0, 0))]
        + [pl.BlockSpec(w.shape, lambda i: (0, 0), pipeline_mode=pl.Buffered(1)) for w in ws],
        out_specs=[pl.BlockSpec((tm, w.shape[1]), lambda i: (i, 0)) for w in ws],
        out_shape=[jax.ShapeDtypeStruct((t, w.shape[1]), dt) for w, dt in zip(ws, out_dtypes)],
        compiler_params=_params("parallel"),
        name="norm_matmul",
    )(x, g.reshape(1, d), *ws)


def _moba_kernel(q_ref, k_ref, v_ref, tab_ref, o_ref, vt_ref, km_ref, *, nb, n_sel, heads):
    for hh in range(heads):
        lanes = pl.ds(hh * HD, HD)
        _moba_head(q_ref.at[:, lanes], k_ref.at[:, lanes], v_ref.at[:, lanes], tab_ref.at[hh],
                   o_ref.at[:, lanes], vt_ref.at[hh], km_ref.at[hh], nb=nb, n_sel=n_sel)


def _moba_head(q_ref, k_ref, v_ref, tab_ref, o_ref, vt_ref, km_ref, *, nb, n_sel):
    blk = MOBA_BLOCK
    scale = HD ** -0.5 * LOG2E
    nbp = km_ref.shape[0]
    km_ref[...] = jnp.zeros_like(km_ref)
    for j in range(nb):
        rows = slice(j * blk, (j + 1) * blk)
        vt_ref[:, rows] = v_ref[rows, :].astype(F32).T.astype(BF16)
        km_ref[j:j + 1, :] = jnp.mean(k_ref[rows, :].astype(F32), axis=0, keepdims=True)
    km = km_ref[...]
    km_hi = km.astype(BF16)
    km_lo = (km - km_hi.astype(F32)).astype(BF16)
    q_all = q_ref[...]
    gate = _dot_nt(km_hi, q_all) + _dot_nt(km_lo, q_all)
    sub = lax.broadcasted_iota(jnp.int32, (nbp, blk), 0)
    bias_far = tab_ref[2, 0:1, :]

    def scores(i):
        return _dot_nt(k_ref[0:(i + 1) * blk, :], q_ref[i * blk:(i + 1) * blk, :])

    def softmax(i, s_all):
        rows = slice(i * blk, (i + 1) * blk)
        if i > 0:
            valid = sub < i
            gm = jnp.where(valid, gate[:, rows], -jnp.inf)
            rank = jnp.zeros((nbp, blk), F32)
            for jp in range(i):
                row = gm[jp:jp + 1, :]
                beats = (row > gm) | ((row == gm) & (sub > jp))
                rank = rank + jnp.where(beats, 1.0, 0.0)
            selm = jnp.where(valid & (rank < n_sel), 0.0, -jnp.inf)
        bands = []
        for j in range(i + 1):
            sj = s_all[j * blk:(j + 1) * blk, :] * scale
            if j == i:
                sj = sj + tab_ref[0]
            elif j == i - 1:
                sj = sj + (tab_ref[1] + selm[j:j + 1, :])
            else:
                sj = sj + (bias_far + selm[j:j + 1, :])
            bands.append(sj)
        s = jnp.concatenate(bands, axis=0) if i > 0 else bands[0]
        m = jnp.max(s, axis=0, keepdims=True)
        return jnp.exp2(s - m).astype(BF16)

    def output(i, p):
        nk = (i + 1) * blk
        l = _dot(jnp.ones((SUBLANES, nk), BF16), p)[0:1, :]
        acc = _dot(vt_ref[:, 0:nk], p)
        o_ref[i * blk:(i + 1) * blk, :] = (acc / l).T.astype(o_ref.dtype)

    s_live, p_live = {}, {}
    units = [sorted({u, nb - 1 - u}) for u in range((nb + 1) // 2)]
    for step in range(len(units) + 2):
        if step < len(units):
            for i in units[step]:
                s_live[i] = scores(i)
        if 1 <= step <= len(units):
            for i in units[step - 1]:
                p_live[i] = softmax(i, s_live.pop(i))
        if step >= 2:
            for i in units[step - 2]:
                output(i, p_live.pop(i))


def _t5_bucket_idx(dist):
    n = jnp.maximum(dist, 0)
    max_exact = REL_BUCKETS // 2
    large = max_exact + (jnp.log(jnp.maximum(n, 1).astype(F32) / max_exact)
                         / math.log(REL_MAX_DIST / max_exact)
                         * (REL_BUCKETS - max_exact)).astype(jnp.int32)
    large = jnp.minimum(large, REL_BUCKETS - 1)
    return jnp.where(n < max_exact, n, large)


def _bias_table_kernel(rb_ref, idx_ref, o_ref):
    h = pl.program_id(0)
    for t in range(3):
        idx = idx_ref[t]
        acc = jnp.full(idx.shape, -jnp.inf, F32)
        for b in range(REL_BUCKETS):
            acc = jnp.where(idx == b, rb_ref[h, b], acc)
        o_ref[t] = acc * LOG2E


def _moba_bias_tables(rel_bias):
    blk = MOBA_BLOCK
    loc = jnp.arange(blk)
    d0 = loc[None, :] - loc[:, None]
    idx = jnp.stack([jnp.where(d0 >= 0, _t5_bucket_idx(d0), -1),
                     _t5_bucket_idx(d0 + blk),
                     _t5_bucket_idx(d0 + 2 * blk)]).astype(jnp.int32)
    return pl.pallas_call(
        _bias_table_kernel,
        grid=(MOBA_HEADS,),
        in_specs=[pl.BlockSpec(memory_space=pltpu.SMEM),
                  pl.BlockSpec((3, blk, blk), lambda h: (0, 0, 0))],
        out_specs=pl.BlockSpec((None, 3, blk, blk), lambda h: (h, 0, 0, 0)),
        out_shape=jax.ShapeDtypeStruct((MOBA_HEADS, 3, blk, blk), F32),
        compiler_params=_params("parallel"),
        name="moba_bias_tables",
    )(rel_bias.T.astype(F32), idx)


def moba_attention(proj, tabs, bsz, s):
    assert s % MOBA_BLOCK == 0
    nb = s // MOBA_BLOCK
    nbp = -(-nb // 8) * 8
    n_sel = min(MOBA_TOPK, nb - 1)
    blk = MOBA_BLOCK

    hps = MOBA_HEADS_PER_STEP
    groups = MOBA_HEADS // hps

    def col(off):
        return pl.BlockSpec((None, s, hps * HD), lambda h, b: (b, 0, off + h))

    return pl.pallas_call(
        functools.partial(_moba_kernel, nb=nb, n_sel=n_sel, heads=hps),
        grid=(groups, bsz),
        in_specs=[col(0), col(groups), col(2 * groups),
                  pl.BlockSpec((hps, 3, blk, blk), lambda h, b: (h, 0, 0, 0))],
        out_specs=pl.BlockSpec((None, s, hps * HD), lambda h, b: (b, 0, h)),
        out_shape=jax.ShapeDtypeStruct((bsz, s, MOBA_WIDTH), BF16),
        scratch_shapes=[pltpu.VMEM((hps, HD, s), BF16),
                        pltpu.VMEM((hps, nbp, HD), F32)],
        compiler_params=_params("parallel", "parallel"),
        name="moba_attention",
    )(proj, proj, proj, tabs)


def _memattn_kernel(q_ref, kv_ref, o_ref, *, n_chunks, qc):
    scale = HD ** -0.5 * LOG2E
    ks =[kv_ref[:, h * HD:(h + 1) * HD] for h in range(MEM_HEADS)]
    vts = [kv_ref[:, MEM_WIDTH + h * HD:MEM_WIDTH + (h + 1) * HD].astype(F32).T.astype(BF16)
           for h in range(MEM_HEADS)]
    ones = jnp.ones((SUBLANES, kv_ref.shape[0]), BF16)

    def scores(h, c):
        return _dot_nt(ks[h], q_ref[c * qc:(c + 1) * qc, h * HD:(h + 1) * HD])

    def softmax(s):
        s = s * scale
        return jnp.exp2(s - jnp.max(s, axis=0, keepdims=True)).astype(BF16)

    def output(h, c, p):
        l = _dot(ones, p)[0:1, :]
        o_ref[c * qc:(c + 1) * qc, h * HD:(h + 1) * HD] = (_dot(vts[h], p) / l).T.astype(o_ref.dtype)

    work = [(h, c) for h in range(MEM_HEADS) for c in range(n_chunks)]
    s_live, p_live = {}, {}
    for step in range(len(work) + 2):
        if step < len(work):
            s_live[step] = scores(*work[step])
        if 1 <= step <= len(work):
            p_live[step - 1] = softmax(s_live.pop(step - 1))
        if step >= 2:
            output(*work[step - 2], p_live.pop(step - 2))


def memory_attention(proj, q_col, kv, bsz, s):
    m_len = kv.shape[1]
    qc = 256
    assert q_col % MEM_WIDTH == 0
    return pl.pallas_call(
        functools.partial(_memattn_kernel, n_chunks=s // qc, qc=qc),
        grid=(bsz,),
        in_specs=[pl.BlockSpec((None, s, MEM_WIDTH), lambda b: (b, 0, q_col // MEM_WIDTH)),
                  pl.BlockSpec((None, m_len, 2 * MEM_WIDTH), lambda b: (b, 0, 0))],
        out_specs=pl.BlockSpec((None, s, MEM_WIDTH), lambda b: (b, 0, 0)),
        out_shape=jax.ShapeDtypeStruct((bsz, s, MEM_WIDTH), BF16),
        compiler_params=_params("parallel"),
        name="memory_attention",
    )(proj, kv)


def _outproj_kernel(x_ref, ya_ref, yb_ref, w_ref, o_ref):
    ka = ya_ref.shape[1]
    o_ref[...] = x_ref[...] + _dot(ya_ref[...], w_ref[0:ka, :]) + _dot(yb_ref[...], w_ref[ka:, :])


def out_projection(x, ya, yb, w, tm):
    t, d = x.shape
    ka, kb = ya.shape[1], yb.shape[1]
    return pl.pallas_call(
        _outproj_kernel,
        grid=(t // tm,),
        in_specs=[pl.BlockSpec((tm, d), lambda i: (i, 0)),
                  pl.BlockSpec((tm, ka), lambda i: (i, 0)),
                  pl.BlockSpec((tm, kb), lambda i: (i, 0)),
                  pl.BlockSpec((ka + kb, d), lambda i: (0, 0))],
        out_specs=pl.BlockSpec((tm, d), lambda i: (i, 0)),
        out_shape=jax.ShapeDtypeStruct((t, d), F32),
        compiler_params=_params("parallel"),
        name="out_projection",
    )(x, ya, yb, w)


def _swiglu_accumulate(h, wg, wu, wd, acc_ref):
    a = _dot(h, wg)
    u = _dot(h, wu)
    acc_ref[...] += _dot((a * _sigmoid(a) * u).astype(BF16), wd)


def _ffn_kernel(x_ref, g_ref, wg_ref, wu_ref, wd_ref, o_ref, h_ref, *, tm):
    @pl.when(pl.program_id(1) == 0)
    def _():
        _rms_rows_to(h_ref, x_ref, g_ref, tm)
        o_ref[...] = x_ref[...]

    _swiglu_accumulate(h_ref[...], wg_ref[...], wu_ref[...], wd_ref[...], o_ref)


def dense_ffn(x, g, wg, wu, wd, tm, tf):
    t, d = x.shape
    ff = wg.shape[1]
    return pl.pallas_call(
        functools.partial(_ffn_kernel, tm=tm),
        grid=(t // tm, ff // tf),
        in_specs=[pl.BlockSpec((tm, d), lambda i, f: (i, 0)),
                  pl.BlockSpec((1, d), lambda i, f: (0, 0)),
                  pl.BlockSpec((d, tf), lambda i, f: (0, f)),
                  pl.BlockSpec((d, tf), lambda i, f: (0, f)),
                  pl.BlockSpec((tf, d), lambda i, f: (f, 0))],
        out_specs=pl.BlockSpec((tm, d), lambda i, f: (i, 0)),
        out_shape=jax.ShapeDtypeStruct((t, d), F32),
        scratch_shapes=[pltpu.VMEM((tm, d), BF16)],
        compiler_params=_params("parallel", "arbitrary"),
        name="dense_ffn",
    )(x, g.reshape(1, d), wg, wu, wd)


def _ssd_kernel(z_ref, xs_ref, b_ref, c_ref, dt_ref, cw_ref, cb_ref, dtb_ref, alog_ref, dsk_ref,
                gout_ref, exp_ref, o_ref, h_ref, *, nc, unroll, groups):
    gw, n = SSD_GW, SSD_STATE
    chunks = []
    for sg in range(groups):
        wide, narrow, lanes = pl.ds(sg * gw, gw), pl.ds(sg * n, n), pl.ds(sg * LANES, LANES)
        chunks.append(_ssd_group_chunk(
            z_ref.at[:, wide], xs_ref.at[:, wide], b_ref.at[:, narrow], c_ref.at[:, narrow], dt_ref.at[:, lanes],
            cw_ref.at[sg], cb_ref.at[sg], dtb_ref.at[sg], alog_ref.at[sg], dsk_ref.at[sg], gout_ref.at[sg],
            exp_ref, o_ref.at[:, wide], h_ref.at[sg]))

    def body(it, carry):
        work = [(sg, it * unroll + u) for u in range(unroll) for sg in range(groups)]
        live = {}
        for step in range(len(work) + 1):
            if step < len(work):
                sg, c = work[step]
                live[step] = chunks[sg][0](c)
            if step >= 1:
                chunks[work[step - 1][0]][1](*live.pop(step - 1))
        return carry

    lax.fori_loop(0, nc // unroll, body, 0)


def _ssd_group_chunk(z_ref, xs_ref, b_ref, c_ref, dt_ref, cw_ref, cb_ref, dtb_ref, alog_ref, dsk_ref,
                     gout_ref, exp_ref, o_ref, h_ref):
    L = SSD_CHUNK
    gw = SSD_GW
    n = SSD_STATE
    h_ref[...] = jnp.zeros(h_ref.shape, h_ref.dtype)
    a_neg = -jnp.exp(alog_ref[...])
    causal = (lax.broadcasted_iota(jnp.int32, (L, L), 0) >= lax.broadcasted_iota(jnp.int32, (L, L), 1))
    tri = jnp.where(causal, 1.0, 0.0).astype(BF16)
    first_half = lax.broadcasted_iota(jnp.int32, (1, 2 * SSD_HD), 1) < SSD_HD
    expm = exp_ref[...]
    srcs = (xs_ref, b_ref, c_ref)
    sh_r = lax.broadcasted_iota(jnp.int32, (L, 2 * L), 0)
    sh_c = lax.broadcasted_iota(jnp.int32, (L, 2 * L), 1)
    shifts = [jnp.where(sh_c == sh_r + (L - s), 1.0, 0.0).astype(BF16) for s in range(1, SSD_CONV)]

    def local(c):
        r0 = pl.multiple_of(c * L, L)
        rq = pl.multiple_of(jnp.maximum(r0 - L, 0), L)
        cur = jnp.concatenate([ref[pl.ds(r0, L), :] for ref in srcs], axis=1)
        prev = jnp.concatenate([ref[pl.ds(rq, L), :] for ref in srcs], axis=1)
        prev = jnp.where(c > 0, prev, jnp.zeros_like(prev))
        ext = jnp.concatenate([prev, cur], axis=0)
        conv = cb_ref[...] + cw_ref[SSD_CONV - 1:SSD_CONV, :] * cur.astype(F32)
        for s in range(1, SSD_CONV):
            k = SSD_CONV - 1 - s
            conv = conv + cw_ref[k:k + 1, :] * _dot(shifts[s - 1], ext)
        act = conv * _sigmoid(conv)
        xs = act[:, 0:gw]
        bm = act[:, gw:gw + n]
        cm = act[:, gw + n:gw + 2 * n]

        dtr = dt_ref[pl.ds(r0, L), :] + dtb_ref[...]
        dt = jnp.maximum(dtr, 0.0) + jnp.log(1.0 + jnp.exp(-jnp.abs(dtr)))
        la = dt * a_neg
        cs = _dot_split_r(tri, la, 3)
        cs_t = cs.T
        dt_t = dt.T
        bm16 = bm.astype(BF16)
        cm16 = cm.astype(BF16)
        scores = _dot_nt(cm16, bm16)
        from_start = _dot_split_l(jnp.exp(cs), expm, 2)
        w_end = dt * jnp.exp(cs[L - 1:L, :] - cs)
        xs_t = xs.T
        return r0, xs, bm, cm16, cs, cs_t, dt_t, scores, from_start, w_end, xs_t

    def carried(r0, xs, bm, cm16, cs, cs_t, dt_t, scores, from_start, w_end, xs_t):
        y = _dot_nt(cm16, h_ref[...].astype(BF16)) * from_start
        y_in = []
        for e in range(SSD_GHEADS):
            diff = cs[:, e:e + 1] - cs_t[e:e + 1, :]
            dec = jnp.exp(jnp.where(causal, diff, -jnp.inf))
            mm = (scores * dec * dt_t[e:e + 1, :]).astype(BF16)
            if e % 2 == 0:
                mm_even = mm
            else:
                xp = xs[:, (e - 1) * SSD_HD:(e + 1) * SSD_HD]
                rhs = jnp.concatenate([jnp.where(first_half, xp, 0.0), jnp.where(first_half, 0.0, xp)], axis=0)
                y_in.append(_dot(jnp.concatenate([mm_even, mm], axis=1), rhs.astype(BF16)))
            bw = (bm * w_end[:, e:e + 1]).astype(BF16)
            st = _dot(xs_t[e * SSD_HD:(e + 1) * SSD_HD, :].astype(BF16), bw)
            cdec = jnp.exp(cs[L - 1:L, e:e + 1])
            hs = slice(e * SSD_HD, (e + 1) * SSD_HD)
            h_ref[hs, :] = h_ref[hs, :] * cdec + st
        y = y + jnp.concatenate(y_in, axis=1) + xs * dsk_ref[...]
        z = z_ref[pl.ds(r0, L), :].astype(F32)
        u = y * (z * _sigmoid(z))
        ms = jnp.mean(u * u, axis=-1, keepdims=True)
        o_ref[pl.ds(r0, L), :] = (u * lax.rsqrt(ms + EPS) * gout_ref[...]).astype(o_ref.dtype)

    return local, carried


def ssd_mixer(proj, dt_all, cw_g, cb_g, dtb_g, alog_g, dsk_g, gout_g, expand, bsz, s):
    assert s % SSD_CHUNK == 0
    nc = s // SSD_CHUNK
    gps = SSD_GROUPS_PER_STEP
    gw = SSD_GW * gps
    nw = SSD_STATE * gps
    z_blk = 0
    xs_blk = SSD_INNER // gw
    b_blk = 2 * SSD_INNER // nw
    c_blk = b_blk + SSD_GROUPS // gps

    def seq(width, off):
        return pl.BlockSpec((None, s, width), lambda b, g: (b, 0, off + g))

    def par(rows, width):
        return pl.BlockSpec((gps, rows, width), lambda b, g: (g, 0, 0))

    return pl.pallas_call(
        functools.partial(_ssd_kernel, nc=nc, unroll=_pick(nc, (4, 2, 1)), groups=gps),
        grid=(bsz, SSD_GROUPS // gps),
        in_specs=[seq(gw, z_blk), seq(gw, xs_blk), seq(nw, b_blk), seq(nw, c_blk),
                  seq(LANES * gps, 0),
                  par(SSD_CONV, SSD_CONV_GW), par(1, SSD_CONV_GW), par(1, LANES), par(1, LANES),
                  par(1, SSD_GW), par(1, SSD_GW),
                  pl.BlockSpec((LANES, SSD_GW), lambda b, g: (0, 0))],
        out_specs=seq(gw, 0),
        out_shape=jax.ShapeDtypeStruct((bsz, s, SSD_INNER), BF16),
        scratch_shapes=[pltpu.VMEM((gps, SSD_GW, SSD_STATE), F32)],
        compiler_params=_params("parallel", "parallel"),
        name="ssd_mixer",
    )(proj, proj, proj, proj, dt_all, cw_g, cb_g, dtb_g, alog_g, dsk_g, gout_g, expand)


def _ssd_group_params(conv_w, conv_b, dt_bias, a_log, d_skip, g_out):
    g, gh, gw, n = SSD_GROUPS, SSD_GHEADS, SSD_GW, SSD_STATE

    def conv_cols(a):
        xs = a[..., :SSD_INNER].reshape(a.shape[:-1] + (g, gw))
        bb = a[..., SSD_INNER:SSD_INNER + g * n].reshape(a.shape[:-1] + (g, n))
        cc = a[..., SSD_INNER + g * n:].reshape(a.shape[:-1] + (g, n))
        return jnp.moveaxis(jnp.concatenate([xs, bb, cc], axis=-1), -2, 0)

    cw_g = conv_cols(conv_w.astype(F32))
    cb_g = conv_cols(conv_b.astype(F32)[None, :])

    def per_head(a):
        return jnp.pad(a.astype(F32).reshape(g, 1, gh), ((0, 0), (0, 0), (0, LANES - gh)))

    dsk_g = jnp.repeat(d_skip.astype(F32), SSD_HD).reshape(g, 1, gw)
    gout_g = g_out.astype(F32).reshape(g, 1, gw)
    expand = (jnp.arange(LANES)[:, None] == (jnp.arange(gw)[None, :] // SSD_HD)).astype(BF16)
    return cw_g, cb_g, per_head(dt_bias), per_head(a_log), dsk_g, gout_g, expand


def _router_kernel(x_ref, g_ref, wrt_ref, br_ref, h_ref, route_ref, runs_ref, cnt_ref, carry_ref, *, tm):
    @pl.when(pl.program_id(0) == 0)
    def _():
        carry_ref[...] = jnp.zeros_like(carry_ref)

    ne = N_EXPERTS
    hn = _rms(x_ref[...], g_ref[...])
    h_hi = hn.astype(BF16)
    h_ref[...] = h_hi
    h_lo = (hn - h_hi.astype(F32)).astype(BF16)
    wr = wrt_ref[...]
    w_hi = wr.astype(BF16)
    w_lo = (wr - w_hi.astype(F32)).astype(BF16)
    logits = _dot_nt(w_hi, h_hi) + _dot_nt(w_hi, h_lo) + _dot_nt(w_lo, h_hi) + br_ref[:, 0:1]
    sub = lax.broadcasted_iota(jnp.int32, (ne, tm), 0)
    l1 = jnp.max(logits, axis=0, keepdims=True)
    i1 = jnp.min(jnp.where(logits == l1, sub, ne), axis=0, keepdims=True)
    rest = jnp.where(sub == i1, -jnp.inf, logits)
    l2 = jnp.max(rest, axis=0, keepdims=True)
    i2 = jnp.min(jnp.where(rest == l2, sub, ne), axis=0, keepdims=True)
    e2 = jnp.exp(l2 - l1)
    g1 = 1.0 / (1.0 + e2)
    g2 = e2 / (1.0 + e2)
    sel = jnp.where((sub == i1) | (sub == i2), 1.0, 0.0)
    before = (lax.broadcasted_iota(jnp.int32, (tm, tm), 0) < lax.broadcasted_iota(jnp.int32, (tm, tm), 1))
    prefix = _dot(sel.astype(BF16), jnp.where(before, 1.0, 0.0).astype(BF16))
    lane = lax.broadcasted_iota(jnp.int32, (ne, LANES), 1)
    subl = lax.broadcasted_iota(jnp.int32, (ne, LANES), 0)
    n_e = (jnp.sum(sel, axis=1, keepdims=True) + jnp.zeros((ne, LANES), F32)).astype(jnp.int32)
    n_pad = (n_e + (SUBLANES - 1)) & (-SUBLANES)
    off = jnp.zeros((ne, LANES), jnp.int32)
    for j in range(ne - 1):
        off = off + jnp.where(subl > j, n_pad[j:j + 1, :], 0)
    slot = prefix + off[:, 0:1].astype(F32)
    p1 = jnp.sum(jnp.where(sub == i1, slot, 0.0), axis=0, keepdims=True)
    p2 = jnp.sum(jnp.where(sub == i2, slot, 0.0), axis=0, keepdims=True)
    route_ref[...] = jnp.where(sub == 0, p1, jnp.where(sub == 1, p2, jnp.where(
        sub == 2, g1, jnp.where(sub == 3, g2, 0.0))))
    carry = carry_ref[...]
    runs_ref[...] = jnp.where(lane == 0, off, jnp.where(lane == 1, carry, jnp.where(lane == 2, n_pad, 0)))
    carry_ref[...] = carry + n_pad
    cnt_ref[...] = carry + n_pad


def moe_router(x, g, w_router, b_router, tm):
    t, d = x.shape
    ne = N_EXPERTS
    wrt = w_router.astype(F32).T
    br = jnp.broadcast_to(b_router.astype(F32)[:, None], (ne, LANES))
    return pl.pallas_call(
        functools.partial(_router_kernel, tm=tm),
        grid=(t // tm,),
        in_specs=[pl.BlockSpec((tm, d), lambda i: (i, 0)),
                  pl.BlockSpec((1, d), lambda i: (0, 0)),
                  pl.BlockSpec((ne, d), lambda i: (0, 0)),
                  pl.BlockSpec((ne, LANES), lambda i: (0, 0))],
        out_specs=[pl.BlockSpec((tm, d), lambda i: (i, 0)),
                   pl.BlockSpec((ne, tm), lambda i: (0, i)),
                   pl.BlockSpec((ne, LANES), lambda i: (i, 0)),
                   pl.BlockSpec((ne, LANES), lambda i: (0, 0))],
        out_shape=[jax.ShapeDtypeStruct((t, d), BF16),
                   jax.ShapeDtypeStruct((ne, t), F32),
                   jax.ShapeDtypeStruct((t // tm * ne, LANES), jnp.int32),
                   jax.ShapeDtypeStruct((ne, LANES), jnp.int32)],
        scratch_shapes=[pltpu.VMEM((ne, LANES), jnp.int32)],
        compiler_params=_params("arbitrary"),
        name="moe_router",
    )(x, g.reshape(1, d), wrt, br)


def _pieces(length, lo_bit, hi_bit):
    for b in range(lo_bit, hi_bit):
        n = 1 << b
        yield n, length & (n - 1), ((length >> b) & 1) == 1


def _run_copies(tab_ref, tile, tile_ref, buf_ref, sem, tm, to_buf, wait):
    for e in range(N_EXPERTS):
        base = (tile * N_EXPERTS + e) * 3
        off, dst, rows = tab_ref[base], tab_ref[base + 1], tab_ref[base + 2]
        for n, lo, present in _pieces(rows, SUBLANES.bit_length() - 1, tm.bit_length()):
            in_tile = tile_ref.at[pl.ds(pl.multiple_of(off + lo, SUBLANES), n), :]
            in_buf = buf_ref.at[pl.ds(pl.multiple_of(dst + lo, SUBLANES), n), :]
            cp = pltpu.make_async_copy(in_tile, in_buf, sem) if to_buf else pltpu.make_async_copy(in_buf, in_tile, sem)

            @pl.when(present)
            def _():
                if wait:
                    cp.wait()
                else:
                    cp.start()


def _dispatch_kernel(tab_ref, meta_ref, route_ref, h_ref, xb_ref, srt_ref, zero_ref, sem, zsem, *, tm, n_tiles):
    i = pl.program_id(0)
    nt = pl.num_programs(0)
    slot = i % 2
    rs = srt_ref.shape[1]

    def runs(tile, slot, wait):
        _run_copies(tab_ref, tile, srt_ref.at[slot], xb_ref, sem.at[slot], tm, True, wait)

    @pl.when(i >= 2)
    def _():
        runs(i - 2, slot, True)

    pos = route_ref[0:2, :].astype(jnp.int32)
    row = lax.broadcasted_iota(jnp.int32, (rs, tm), 0)
    onehot = jnp.where((row == pos[0:1, :]) | (row == pos[1:2, :]), 1.0, 0.0).astype(BF16)
    srt_ref[slot] = _dot(onehot, h_ref[...])
    runs(i, slot, False)

    @pl.when(i == nt - 1)
    def _():
        zero_ref[...] = jnp.zeros_like(zero_ref)
        zrows = zero_ref.shape[0]
        per = EXPERT_ROWS // zrows
        first_unused = meta_ref[2 * N_EXPERTS] * per
        for wait in (False, True):
            def go(cp, cond):
                @pl.when(cond)
                def _():
                    if wait:
                        cp.wait()
                    else:
                        cp.start()

            for e in range(N_EXPERTS):
                c = meta_ref[e]
                first = meta_ref[N_EXPERTS + e]
                tail = (-c) & (EXPERT_ROWS - 1)
                for n, lo, present in _pieces(tail, SUBLANES.bit_length() - 1, EXPERT_ROWS.bit_length() - 1):
                    row0 = pl.multiple_of(first + c + lo, SUBLANES)
                    go(pltpu.make_async_copy(zero_ref.at[pl.ds(0, n), :], xb_ref.at[pl.ds(row0, n), :], zsem), present)
            for j in range(N_EXPERTS * per):
                row0 = pl.multiple_of(jnp.minimum(first_unused + j, n_tiles * per - 1) * zrows, zrows)
                go(pltpu.make_async_copy(zero_ref, xb_ref.at[pl.ds(row0, zrows), :], zsem),
                   first_unused + j < n_tiles * per)

        @pl.when(i >= 1)
        def _():
            runs(i - 1, 1 - slot, True)

        runs(i, slot, True)


def moe_dispatch(tab, meta, route, h, tm, n_tiles):
    t, d = h.shape
    ne = route.shape[0]
    rs = TOP_K * tm + LANES
    return pl.pallas_call(
        functools.partial(_dispatch_kernel, tm=tm, n_tiles=n_tiles),
        grid=(t // tm,),
        in_specs=[pl.BlockSpec(memory_space=pltpu.SMEM),
                  pl.BlockSpec(memory_space=pltpu.SMEM),
                  pl.BlockSpec((ne, tm), lambda i: (0, i)),
                  pl.BlockSpec((tm, d), lambda i: (i, 0))],
        out_specs=pl.BlockSpec(memory_space=pl.ANY),
        out_shape=jax.ShapeDtypeStruct((n_tiles * EXPERT_ROWS, d), F32),
        scratch_shapes=[pltpu.VMEM((2, rs, d), F32),
                        pltpu.VMEM((EXPERT_ROWS // 2, d), F32),
                        pltpu.SemaphoreType.DMA((2,)), pltpu.SemaphoreType.DMA(())],
        compiler_params=_params("arbitrary"),
        name="moe_dispatch",
    )(tab, meta, route, h)


def _expert_kernel(te_ref, nr_ref, xb_ref, *refs, nf):
    w_refs, o_ref = refs[:-1], refs[-1]
    per = len(w_refs) // 3
    rows = nr_ref[pl.program_id(0)]
    step = pl.program_id(1)
    r, rc = EXPERT_ROWS, EXPERT_ROW_CHUNK

    @pl.when(step == 0)
    def _():
        o_ref[...] = jnp.zeros_like(o_ref)

    def accumulate(h, out_ref):
        for k in range(per):
            @pl.when(step * per + k < nf)
            def _():
                wg, wu, wd = (w[...].astype(BF16) for w in w_refs[3 * k:3 * k + 3])
                _swiglu_accumulate(h, wg, wu, wd, out_ref)

    @pl.when(rows == r)
    def _():
        accumulate(xb_ref[...].astype(BF16), o_ref)

    @pl.when(jnp.logical_and(rows > 0, rows < r))
    def _():
        for j in range(r // rc):
            @pl.when(j * rc < rows)
            def _():
                accumulate(xb_ref[j * rc:(j + 1) * rc, :].astype(BF16), o_ref.at[j * rc:(j + 1) * rc, :])


def expert_ffn(tile_expert, tile_rows, xb, wg, wu, wd, tf):
    n_rows, d = xb.shape
    ff = wg.shape[2]
    r = EXPERT_ROWS
    nf = ff // tf
    per = EXPERT_HIDDEN_PER_STEP
    steps = -(-nf // per)

    def hidden(k):
        return lambda c, f, te, nr: jnp.where(nr[c] > 0, jnp.minimum(f * per + k, nf - 1), nf - 1)

    w_specs, w_args = [], []
    for k in range(per):
        fk = hidden(k)
        w_specs += [pl.BlockSpec((None, d, tf), lambda c, f, te, nr, fk=fk: (te[c], 0, fk(c, f, te, nr))),
                    pl.BlockSpec((None, d, tf), lambda c, f, te, nr, fk=fk: (te[c], 0, fk(c, f, te, nr))),
                    pl.BlockSpec((None, tf, d), lambda c, f, te, nr, fk=fk: (te[c], fk(c, f, te, nr), 0))]
        w_args += [wg, wu, wd]
    grid_spec = pltpu.PrefetchScalarGridSpec(
        num_scalar_prefetch=2,
        grid=(n_rows // r, steps),
        in_specs=[pl.BlockSpec((r, d), lambda c, f, te, nr: (c, 0))] + w_specs,
        out_specs=pl.BlockSpec((r, d), lambda c, f, te, nr: (c, 0)),
    )
    return pl.pallas_call(
        functools.partial(_expert_kernel, nf=nf),
        grid_spec=grid_spec,
        out_shape=jax.ShapeDtypeStruct((n_rows, d), F32),
        compiler_params=_params("parallel", "arbitrary"),
        name="expert_ffn",
    )(tile_expert, tile_rows, xb, *w_args)


def _combine_kernel(tab_ref, route_ref, x_ref, g_ref, yb_ref, o_ref, srt_ref, sem, *, tm):
    i = pl.program_id(0)
    nt = pl.num_programs(0)
    slot = i % 2
    rs = srt_ref.shape[1]

    def runs(tile, slot, wait):
        _run_copies(tab_ref, tile, srt_ref.at[slot], yb_ref, sem.at[slot], tm, False, wait)

    @pl.when(i == 0)
    def _():
        srt_ref[...] = jnp.zeros_like(srt_ref)
        runs(0, 0, False)

    @pl.when(i + 1 < nt)
    def _():
        runs(i + 1, 1 - slot, False)

    runs(i, slot, True)
    aux = route_ref[...]
    aux_t = jnp.concatenate([aux, jnp.zeros((LANES - aux.shape[0], tm), F32)], axis=0).T
    srt = srt_ref[slot].astype(BF16)
    col = lax.broadcasted_iota(jnp.int32, (tm, rs), 1)
    ys = []
    for k in range(TOP_K):
        onehot = jnp.where(col == aux_t[:, k:k + 1].astype(jnp.int32), 1.0, 0.0).astype(BF16)
        ys.append(aux_t[:, TOP_K + k:TOP_K + k + 1] * _dot(onehot, srt))
    o_ref[...] = _rms(x_ref[...] + (ys[0] + ys[1]), g_ref[...])


def combine_final(tab, route, x, g, yb, tm):
    t, d = x.shape
    ne = route.shape[0]
    rs = TOP_K * tm + LANES
    return pl.pallas_call(
        functools.partial(_combine_kernel, tm=tm),
        grid=(t // tm,),
        in_specs=[pl.BlockSpec(memory_space=pltpu.SMEM),
                  pl.BlockSpec((ne, tm), lambda i: (0, i)),
                  pl.BlockSpec((tm, d), lambda i: (i, 0)),
                  pl.BlockSpec((1, d), lambda i: (0, 0)),
                  pl.BlockSpec(memory_space=pl.ANY)],
        out_specs=pl.BlockSpec((tm, d), lambda i: (i, 0)),
        out_shape=jax.ShapeDtypeStruct((t, d), F32),
        scratch_shapes=[pltpu.VMEM((2, rs, d), F32), pltpu.SemaphoreType.DMA((2,))],
        compiler_params=_params("arbitrary"),
        name="combine_final",
    )(tab, route, x, g.reshape(1, d), yb)


def _pick(n, prefs):
    for p in prefs:
        if n % p == 0:
            return p
    return n


class _Tiles(NamedTuple):
    in_rows: int
    proj_rows: int
    out_rows: int
    mem_rows: int
    moe_rows: int
    moba_cols: int
    ssd_cols: int
    ffn_cols: int
    expert_cols: int


def _choose_tiles(t, mem_rows):
    return _Tiles(in_rows=_pick(t, (1024, 512, 256)), proj_rows=_pick(t, (512, 256)),
                  out_rows=_pick(t, (1024, 512, 256)),
                  mem_rows=_pick(mem_rows, (1024, 512, 256)), moe_rows=_pick(t, (512, 256)),
                  moba_cols=1280, ssd_cols=2048, ffn_cols=1792, expert_cols=512)


class _ExpertLayout(NamedTuple):
    n_tiles: int
    tile_expert: Any
    tile_rows: Any
    run_table: Any
    meta: Any


def _expert_layout(runs, cnt, t, tm):
    r, rc = EXPERT_ROWS, EXPERT_ROW_CHUNK
    nt = t // tm
    n_tiles = -(-(t * TOP_K + nt * N_EXPERTS * (SUBLANES - 1)) // r) + N_EXPERTS
    tiles_per = (cnt + r - 1) // r
    ends = jnp.cumsum(tiles_per)
    first_row = (ends - tiles_per) * r
    n_used = ends[-1].astype(jnp.int32)
    step = jnp.minimum(jnp.arange(n_tiles, dtype=jnp.int32), n_used - 1)
    tile_expert = jnp.sum(step[:, None] >= ends[None, :], axis=1).astype(jnp.int32)
    rows_left = jnp.sum(jnp.where(tile_expert[:, None] == jnp.arange(N_EXPERTS)[None, :],
                                  (cnt + first_row)[None, :], 0), axis=1) - step * r
    tile_rows = jnp.where(jnp.arange(n_tiles) < n_used, jnp.clip((rows_left + rc - 1) // rc * rc, 0, r), 0)
    tab = runs.reshape(nt, N_EXPERTS, LANES)[:, :, :3]
    tab = tab.at[:, :, 1].add(first_row[None, :]).reshape(-1).astype(jnp.int32)
    meta = jnp.concatenate([cnt, first_row, n_used.reshape(1)]).astype(jnp.int32)
    return _ExpertLayout(n_tiles, tile_expert, tile_rows.astype(jnp.int32), tab, meta)


def kernel(x, mem, g_mix, g_mem, w_in_moba, w_in_ssd, w_mem_kv, w_out, rel_bias, conv_w, conv_b, dt_bias, a_log, d_skip, g_ssd_out, g_ffn, w_ffn_gate, w_ffn_up, w_ffn_down, w_router, b_router, w_exp_gate, w_exp_up, w_exp_down, g_final):
    bsz, s, d = x.shape
    t = bsz * s
    m_len = mem.shape[1]
    tiles = _choose_tiles(t, bsz * m_len)
    xf = x.reshape(t, d).astype(F32)
    memf = mem.reshape(bsz * m_len, d).astype(F32)

    proj0, = norm_matmul(xf, g_mix[0], [w_in_moba[0].astype(BF16)], [BF16], tiles.in_rows, [tiles.moba_cols])
    kv0, = norm_matmul(memf, g_mem[0], [w_mem_kv[0].astype(BF16)], [BF16], tiles.mem_rows, [2 * MEM_WIDTH])
    proj0 = proj0.reshape(bsz, s, -1)
    y_tok = moba_attention(proj0, _moba_bias_tables(rel_bias), bsz, s)
    y_mem = memory_attention(proj0, 3 * MOBA_WIDTH, kv0.reshape(bsz, m_len, -1), bsz, s)
    x1 = out_projection(xf, y_tok.reshape(t, -1), y_mem.reshape(t, -1), w_out[0].astype(BF16), tiles.out_rows)
    x1 = dense_ffn(x1, g_ffn[0], w_ffn_gate[0].astype(BF16), w_ffn_up[0].astype(BF16),
                   w_ffn_down[0].astype(BF16), tiles.proj_rows, tiles.ffn_cols)

    n_zx = SSD_INNER + SSD_INNER + 2 * SSD_GROUPS * SSD_STATE
    w1 = w_in_ssd[0]
    w_zx = w1[:, :n_zx].astype(BF16)
    w_qm = w1[:, n_zx + SSD_HEADS:].astype(BF16)
    w_dt = jnp.pad(w1[:, n_zx:n_zx + SSD_HEADS].reshape(d, SSD_GROUPS, SSD_GHEADS),
                   ((0, 0), (0, 0), (0, LANES - SSD_GHEADS))).reshape(d, SSD_GROUPS * LANES).astype(BF16)
    proj1, q_mem, dt_all = norm_matmul(x1, g_mix[1], [w_zx, w_qm, w_dt], [BF16, BF16, F32], tiles.in_rows,
                                       [tiles.ssd_cols, MEM_WIDTH, SSD_GROUPS * LANES])
    proj1, q_mem, dt_all = (a.reshape(bsz, s, -1) for a in (proj1, q_mem, dt_all))
    kv1, = norm_matmul(memf, g_mem[1], [w_mem_kv[1].astype(BF16)], [BF16], tiles.mem_rows, [2 * MEM_WIDTH])
    y_tok = ssd_mixer(proj1, dt_all, *_ssd_group_params(conv_w[0], conv_b[0], dt_bias[0], a_log[0],
                                                         d_skip[0], g_ssd_out[0]), bsz, s)
    y_mem = memory_attention(q_mem, 0, kv1.reshape(bsz, m_len, -1), bsz, s)
    x2 = out_projection(x1, y_tok.reshape(t, -1), y_mem.reshape(t, -1), w_out[1].astype(BF16), tiles.out_rows)

    h, route, runs, counts = moe_router(x2, g_ffn[1], w_router[0], b_router[0], tiles.moe_rows)
    lay = _expert_layout(runs, counts[:, 0], t, tiles.moe_rows)
    xb = moe_dispatch(lay.run_table, lay.meta, route, h, tiles.moe_rows, lay.n_tiles)
    yb = expert_ffn(lay.tile_expert, lay.tile_rows, xb, w_exp_gate[0], w_exp_up[0], w_exp_down[0], tiles.expert_cols)
    out = combine_final(lay.run_table, route, x2, g_final, yb, tiles.moe_rows)
    return out.reshape(bsz, s, d).astype(x.dtype)
```

```python
import functools
import math
from typing import Any, NamedTuple

import jax
import jax.numpy as jnp
from jax import lax
from jax.experimental import pallas as pl
from jax.experimental.pallas import tpu as pltpu

F32 = jnp.float32
BF16 = jnp.bfloat16

D_MODEL = 1024
HD = 128
MOBA_HEADS = 12
MOBA_WIDTH = MOBA_HEADS * HD
MOBA_BLOCK = 256
MOBA_TOPK = 3
MOBA_HEADS_PER_STEP = 2
REL_BUCKETS = 32
REL_MAX_DIST = 128
MEM_HEADS = 4
MEM_WIDTH = MEM_HEADS * HD
SSD_HEADS = 24
SSD_HD = 64
SSD_INNER = SSD_HEADS * SSD_HD
SSD_GROUPS = 4
SSD_STATE = 128
SSD_CONV = 4
SSD_CHUNK = 128
SSD_GROUPS_PER_STEP = 2
SSD_GHEADS = SSD_HEADS // SSD_GROUPS
SSD_GW = SSD_GHEADS * SSD_HD
SSD_CONV_GW = SSD_GW + 2 * SSD_STATE
D_FF = 3584
N_EXPERTS = 8
TOP_K = 2
EPS = 1e-6

LOG2E = math.log2(math.e)
LANES = 128
SUBLANES = 8
VMEM_LIMIT = 48 * 1024 * 1024
EXPERT_ROWS = 1024
EXPERT_ROW_CHUNK = 256


def _dot(a, b):
    return jnp.dot(a, b, preferred_element_type=F32)


def _dot_nt(a, b):
    return lax.dot_general(a, b, (((1,), (1,)), ((), ())), preferred_element_type=F32)


def _split(x, pieces):
    out = []
    for _ in range(pieces):
        p = x.astype(BF16)
        out.append(p)
        x = x - p.astype(F32)
    return out


def _dot_split_l(x, w, pieces):
    return sum(_dot(p, w) for p in _split(x, pieces))


def _dot_split_r(w, x, pieces):
    return sum(_dot(w, p) for p in _split(x, pieces))


def _sigmoid(x):
    return 1.0 / (1.0 + jnp.exp(-x))


def _rms(x, g):
    ms = jnp.mean(x * x, axis=-1, keepdims=True)
    return x * lax.rsqrt(ms + EPS) * g


def _rms_rows_to(h_ref, x_ref, g_ref, rows):
    step = min(rows, 256)
    for r in range(0, rows, step):
        h_ref[r:r + step, :] = _rms(x_ref[r:r + step, :], g_ref[...]).astype(h_ref.dtype)


def _params(*sem):
    return pltpu.CompilerParams(dimension_semantics=sem, vmem_limit_bytes=VMEM_LIMIT)


def _normmm_kernel(x_ref, g_ref, *refs, rc, tns):
    w_refs, o_refs = refs[:len(tns)], refs[len(tns):]
    tm = x_ref.shape[0]

    def norm(r):
        return _rms(x_ref[r * rc:(r + 1) * rc, :], g_ref[...]).astype(BF16)

    def matmuls(r, h):
        for w_ref, o_ref, tn in zip(w_refs, o_refs, tns):
            for j in range(w_ref.shape[1] // tn):
                cols = slice(j * tn, (j + 1) * tn)
                o_ref[r * rc:(r + 1) * rc, cols] = _dot(h, w_ref[:, cols]).astype(o_ref.dtype)

    live = {}
    for step in range(tm // rc + 1):
        if step < tm // rc:
            live[step] = norm(step)
        if step >= 1:
            matmuls(step - 1, live.pop(step - 1))


def norm_matmul(x, g, ws, out_dtypes, tm, tns):
    t, d = x.shape
    return pl.pallas_call(
        functools.partial(_normmm_kernel, rc=min(tm, 256), tns=tuple(tns)),
        grid=(t // tm,),
        in_specs=[pl.BlockSpec((tm, d), lambda i: (i, 0)),
                  pl.BlockSpec((1, d), lambda i: (0, 0))]
        + [pl.BlockSpec(w.shape, lambda i: (0, 0), pipeline_mode=pl.Buffered(1)) for w in ws],
        out_specs=[pl.BlockSpec((tm, w.shape[1]), lambda i: (i, 0)) for w in ws],
        out_shape=[jax.ShapeDtypeStruct((t, w.shape[1]), dt) for w, dt in zip(ws, out_dtypes)],
        compiler_params=_params("parallel"),
        name="norm_matmul",
    )(x, g.reshape(1, d), *ws)


def _moba_kernel(q_ref, k_ref, v_ref, tab_ref, o_ref, vt_ref, km_ref, *, nb, n_sel, heads):
    for hh in range(heads):
        lanes = pl.ds(hh * HD, HD)
        _moba_head(q_ref.at[:, lanes], k_ref.at[:, lanes], v_ref.at[:, lanes], tab_ref.at[hh],
                   o_ref.at[:, lanes], vt_ref.at[hh], km_ref.at[hh], nb=nb, n_sel=n_sel)


def _moba_head(q_ref, k_ref, v_ref, tab_ref, o_ref, vt_ref, km_ref, *, nb, n_sel):
    blk = MOBA_BLOCK
    scale = HD ** -0.5 * LOG2E
    nbp = km_ref.shape[0]
    km_ref[...] = jnp.zeros_like(km_ref)
    for j in range(nb):
        rows = slice(j * blk, (j + 1) * blk)
        vt_ref[:, rows] = v_ref[rows, :].astype(F32).T.astype(BF16)
        km_ref[j:j + 1, :] = jnp.mean(k_ref[rows, :].astype(F32), axis=0, keepdims=True)
    km = km_ref[...]
    km_hi = km.astype(BF16)
    km_lo = (km - km_hi.astype(F32)).astype(BF16)
    q_all = q_ref[...]
    gate = _dot_nt(km_hi, q_all) + _dot_nt(km_lo, q_all)
    sub = lax.broadcasted_iota(jnp.int32, (nbp, blk), 0)
    bias_far = tab_ref[2, 0:1, :]

    def scores(i):
        return _dot_nt(k_ref[0:(i + 1) * blk, :], q_ref[i * blk:(i + 1) * blk, :])

    def softmax(i, s_all):
        rows = slice(i * blk, (i + 1) * blk)
        if i > 0:
            valid = sub < i
            gm = jnp.where(valid, gate[:, rows], -jnp.inf)
            rank = jnp.zeros((nbp, blk), F32)
            for jp in range(i):
                row = gm[jp:jp + 1, :]
                beats = (row > gm) | ((row == gm) & (sub > jp))
                rank = rank + jnp.where(beats, 1.0, 0.0)
            selm = jnp.where(valid & (rank < n_sel), 0.0, -jnp.inf)
        bands = []
        for j in range(i + 1):
            sj = s_all[j * blk:(j + 1) * blk, :] * scale
            if j == i:
                sj = sj + tab_ref[0]
            elif j == i - 1:
                sj = sj + (tab_ref[1] + selm[j:j + 1, :])
            else:
                sj = sj + (bias_far + selm[j:j + 1, :])
            bands.append(sj)
        s = jnp.concatenate(bands, axis=0) if i > 0 else bands[0]
        m = jnp.max(s, axis=0, keepdims=True)
        return jnp.exp2(s - m).astype(BF16)

    def output(i, p):
        nk = (i + 1) * blk
        l = _dot(jnp.ones((SUBLANES, nk), BF16), p)[0:1, :]
        acc = _dot(vt_ref[:, 0:nk], p)
        o_ref[i * blk:(i + 1) * blk, :] = (acc / l).T.astype(o_ref.dtype)

    s_live, p_live = {}, {}
    units = [sorted({u, nb - 1 - u}) for u in range((nb + 1) // 2)]
    for step in range(len(units) + 2):
        if step < len(units):
            for i in units[step]:
                s_live[i] = scores(i)
        if 1 <= step <= len(units):
            for i in units[step - 1]:
                p_live[i] = softmax(i, s_live.pop(i))
        if step >= 2:
            for i in units[step - 2]:
                output(i, p_live.pop(i))


def _t5_bucket_idx(dist):
    n = jnp.maximum(dist, 0)
    max_exact = REL_BUCKETS // 2
    large = max_exact + (jnp.log(jnp.maximum(n, 1).astype(F32) / max_exact)
                         / math.log(REL_MAX_DIST / max_exact)
                         * (REL_BUCKETS - max_exact)).astype(jnp.int32)
    large = jnp.minimum(large, REL_BUCKETS - 1)
    return jnp.where(n < max_exact, n, large)


def _bias_table_kernel(rb_ref, idx_ref, o_ref):
    h = pl.program_id(0)
    for t in range(3):
        idx = idx_ref[t]
        acc = jnp.full(idx.shape, -jnp.inf, F32)
        for b in range(REL_BUCKETS):
            acc = jnp.where(idx == b, rb_ref[h, b], acc)
        o_ref[t] = acc * LOG2E


def _moba_bias_tables(rel_bias):
    blk = MOBA_BLOCK
    loc = jnp.arange(blk)
    d0 = loc[None, :] - loc[:, None]
    idx = jnp.stack([jnp.where(d0 >= 0, _t5_bucket_idx(d0), -1),
                     _t5_bucket_idx(d0 + blk),
                     _t5_bucket_idx(d0 + 2 * blk)]).astype(jnp.int32)
    return pl.pallas_call(
        _bias_table_kernel,
        grid=(MOBA_HEADS,),
        in_specs=[pl.BlockSpec(memory_space=pltpu.SMEM),
                  pl.BlockSpec((3, blk, blk), lambda h: (0, 0, 0))],
        out_specs=pl.BlockSpec((None, 3, blk, blk), lambda h: (h, 0, 0, 0)),
        out_shape=jax.ShapeDtypeStruct((MOBA_HEADS, 3, blk, blk), F32),
        compiler_params=_params("parallel"),
        name="moba_bias_tables",
    )(rel_bias.T.astype(F32), idx)


def moba_attention(proj, tabs, bsz, s):
    assert s % MOBA_BLOCK == 0
    nb = s // MOBA_BLOCK
    nbp = -(-nb // 8) * 8
    n_sel = min(MOBA_TOPK, nb - 1)
    blk = MOBA_BLOCK

    hps = MOBA_HEADS_PER_STEP
    groups = MOBA_HEADS // hps

    def col(off):
        return pl.BlockSpec((None, s, hps * HD), lambda h, b: (b, 0, off + h))

    return pl.pallas_call(
        functools.partial(_moba_kernel, nb=nb, n_sel=n_sel, heads=hps),
        grid=(groups, bsz),
        in_specs=[col(0), col(groups), col(2 * groups),
                  pl.BlockSpec((hps, 3, blk, blk), lambda h, b: (h, 0, 0, 0))],
        out_specs=pl.BlockSpec((None, s, hps * HD), lambda h, b: (b, 0, h)),
        out_shape=jax.ShapeDtypeStruct((bsz, s, MOBA_WIDTH), BF16),
        scratch_shapes=[pltpu.VMEM((hps, HD, s), BF16),
                        pltpu.VMEM((hps, nbp, HD), F32)],
        compiler_params=_params("parallel", "parallel"),
        name="moba_attention",
    )(proj, proj, proj, tabs)


def _memattn_kernel(q_ref, kv_ref, o_ref, *, n_chunks, qc):
    scale = HD ** -0.5 * LOG2E
    ks =[kv_ref[:, h * HD:(h + 1) * HD] for h in range(MEM_HEADS)]
    vts = [kv_ref[:, MEM_WIDTH + h * HD:MEM_WIDTH + (h + 1) * HD].astype(F32).T.astype(BF16)
           for h in range(MEM_HEADS)]
    ones = jnp.ones((SUBLANES, kv_ref.shape[0]), BF16)

    def scores(h, c):
        return _dot_nt(ks[h], q_ref[c * qc:(c + 1) * qc, h * HD:(h + 1) * HD])

    def softmax(s):
        s = s * scale
        return jnp.exp2(s - jnp.max(s, axis=0, keepdims=True)).astype(BF16)

    def output(h, c, p):
        l = _dot(ones, p)[0:1, :]
        o_ref[c * qc:(c + 1) * qc, h * HD:(h + 1) * HD] = (_dot(vts[h], p) / l).T.astype(o_ref.dtype)

    work = [(h, c) for h in range(MEM_HEADS) for c in range(n_chunks)]
    s_live, p_live = {}, {}
    for step in range(len(work) + 2):
        if step < len(work):
            s_live[step] = scores(*work[step])
        if 1 <= step <= len(work):
            p_live[step - 1] = softmax(s_live.pop(step - 1))
        if step >= 2:
            output(*work[step - 2], p_live.pop(step - 2))


def memory_attention(proj, q_col, kv, bsz, s):
    m_len = kv.shape[1]
    qc = 256
    assert q_col % MEM_WIDTH == 0
    return pl.pallas_call(
        functools.partial(_memattn_kernel, n_chunks=s // qc, qc=qc),
        grid=(bsz,),
        in_specs=[pl.BlockSpec((None, s, MEM_WIDTH), lambda b: (b, 0, q_col // MEM_WIDTH)),
                  pl.BlockSpec((None, m_len, 2 * MEM_WIDTH), lambda b: (b, 0, 0))],
        out_specs=pl.BlockSpec((None, s, MEM_WIDTH), lambda b: (b, 0, 0)),
        out_shape=jax.ShapeDtypeStruct((bsz, s, MEM_WIDTH), BF16),
        compiler_params=_params("parallel"),
        name="memory_attention",
    )(proj, kv)


def _outproj_kernel(x_ref, ya_ref, yb_ref, w_ref, o_ref):
    ka = ya_ref.shape[1]
    o_ref[...] = x_ref[...] + _dot(ya_ref[...], w_ref[0:ka, :]) + _dot(yb_ref[...], w_ref[ka:, :])


def out_projection(x, ya, yb, w, tm):
    t, d = x.shape
    ka, kb = ya.shape[1], yb.shape[1]
    return pl.pallas_call(
        _outproj_kernel,
        grid=(t // tm,),
        in_specs=[pl.BlockSpec((tm, d), lambda i: (i, 0)),
                  pl.BlockSpec((tm, ka), lambda i: (i, 0)),
                  pl.BlockSpec((tm, kb), lambda i: (i, 0)),
                  pl.BlockSpec((ka + kb, d), lambda i: (0, 0))],
        out_specs=pl.BlockSpec((tm, d), lambda i: (i, 0)),
        out_shape=jax.ShapeDtypeStruct((t, d), F32),
        compiler_params=_params("parallel"),
        name="out_projection",
    )(x, ya, yb, w)


def _swiglu(h, wg, wu, wd):
    a = _dot(h, wg)
    u = _dot(h, wu)
    return _dot((a * _sigmoid(a) * u).astype(BF16), wd)


def _swiglu_accumulate(h, wg, wu, wd, acc_ref):
    acc_ref[...] += _swiglu(h, wg, wu, wd)


def _ffn_kernel(x_ref, g_ref, wg_ref, wu_ref, wd_ref, o_ref, h_ref, *, tm):
    @pl.when(pl.program_id(1) == 0)
    def _():
        _rms_rows_to(h_ref, x_ref, g_ref, tm)
        o_ref[...] = x_ref[...]

    _swiglu_accumulate(h_ref[...], wg_ref[...], wu_ref[...], wd_ref[...], o_ref)


def dense_ffn(x, g, wg, wu, wd, tm, tf):
    t, d = x.shape
    ff = wg.shape[1]
    return pl.pallas_call(
        functools.partial(_ffn_kernel, tm=tm),
        grid=(t // tm, ff // tf),
        in_specs=[pl.BlockSpec((tm, d), lambda i, f: (i, 0)),
                  pl.BlockSpec((1, d), lambda i, f: (0, 0)),
                  pl.BlockSpec((d, tf), lambda i, f: (0, f)),
                  pl.BlockSpec((d, tf), lambda i, f: (0, f)),
                  pl.BlockSpec((tf, d), lambda i, f: (f, 0))],
        out_specs=pl.BlockSpec((tm, d), lambda i, f: (i, 0)),
        out_shape=jax.ShapeDtypeStruct((t, d), F32),
        scratch_shapes=[pltpu.VMEM((tm, d), BF16)],
        compiler_params=_params("parallel", "arbitrary"),
        name="dense_ffn",
    )(x, g.reshape(1, d), wg, wu, wd)


def _ssd_kernel(z_ref, xs_ref, b_ref, c_ref, dt_ref, cw_ref, cb_ref, dtb_ref, alog_ref, dsk_ref,
                gout_ref, exp_ref, o_ref, h_ref, *, nc, unroll, groups):
    gw, n = SSD_GW, SSD_STATE
    chunks = []
    for sg in range(groups):
        wide, narrow, lanes = pl.ds(sg * gw, gw), pl.ds(sg * n, n), pl.ds(sg * LANES, LANES)
        chunks.append(_ssd_group_chunk(
            z_ref.at[:, wide], xs_ref.at[:, wide], b_ref.at[:, narrow], c_ref.at[:, narrow], dt_ref.at[:, lanes],
            cw_ref.at[sg], cb_ref.at[sg], dtb_ref.at[sg], alog_ref.at[sg], dsk_ref.at[sg], gout_ref.at[sg],
            exp_ref, o_ref.at[:, wide], h_ref.at[sg]))

    def body(it, carry):
        work = [(sg, it * unroll + u) for u in range(unroll) for sg in range(groups)]
        live = {}
        for step in range(len(work) + 1):
            if step < len(work):
                sg, c = work[step]
                live[step] = chunks[sg][0](c)
            if step >= 1:
                chunks[work[step - 1][0]][1](*live.pop(step - 1))
        return carry

    lax.fori_loop(0, nc // unroll, body, 0)


def _ssd_group_chunk(z_ref, xs_ref, b_ref, c_ref, dt_ref, cw_ref, cb_ref, dtb_ref, alog_ref, dsk_ref,
                     gout_ref, exp_ref, o_ref, h_ref):
    L = SSD_CHUNK
    gw = SSD_GW
    n = SSD_STATE
    h_ref[...] = jnp.zeros(h_ref.shape, h_ref.dtype)
    a_neg = -jnp.exp(alog_ref[...])
    causal = (lax.broadcasted_iota(jnp.int32, (L, L), 0) >= lax.broadcasted_iota(jnp.int32, (L, L), 1))
    tri = jnp.where(causal, 1.0, 0.0).astype(BF16)
    first_half = lax.broadcasted_iota(jnp.int32, (1, 2 * SSD_HD), 1) < SSD_HD
    expm = exp_ref[...]
    srcs = (xs_ref, b_ref, c_ref)
    sh_r = lax.broadcasted_iota(jnp.int32, (L, 2 * L), 0)
    sh_c = lax.broadcasted_iota(jnp.int32, (L, 2 * L), 1)
    shifts = [jnp.where(sh_c == sh_r + (L - s), 1.0, 0.0).astype(BF16) for s in range(1, SSD_CONV)]

    def local(c):
        r0 = pl.multiple_of(c * L, L)
        rq = pl.multiple_of(jnp.maximum(r0 - L, 0), L)
        cur = jnp.concatenate([ref[pl.ds(r0, L), :] for ref in srcs], axis=1)
        prev = jnp.concatenate([ref[pl.ds(rq, L), :] for ref in srcs], axis=1)
        prev = jnp.where(c > 0, prev, jnp.zeros_like(prev))
        ext = jnp.concatenate([prev, cur], axis=0)
        conv = cb_ref[...] + cw_ref[SSD_CONV - 1:SSD_CONV, :] * cur.astype(F32)
        for s in range(1, SSD_CONV):
            k = SSD_CONV - 1 - s
            conv = conv + cw_ref[k:k + 1, :] * _dot(shifts[s - 1], ext)
        act = conv * _sigmoid(conv)
        xs = act[:, 0:gw]
        bm = act[:, gw:gw + n]
        cm = act[:, gw + n:gw + 2 * n]

        dtr = dt_ref[pl.ds(r0, L), :] + dtb_ref[...]
        dt = jnp.maximum(dtr, 0.0) + jnp.log(1.0 + jnp.exp(-jnp.abs(dtr)))
        la = dt * a_neg
        cs = _dot_split_r(tri, la, 3)
        cs_t = cs.T
        dt_t = dt.T
        bm16 = bm.astype(BF16)
        cm16 = cm.astype(BF16)
        scores = _dot_nt(cm16, bm16)
        from_start = _dot_split_l(jnp.exp(cs), expm, 2)
        w_end = dt * jnp.exp(cs[L - 1:L, :] - cs)
        xs_t = xs.T
        return r0, xs, bm, cm16, cs, cs_t, dt_t, scores, from_start, w_end, xs_t

    def carried(r0, xs, bm, cm16, cs, cs_t, dt_t, scores, from_start, w_end, xs_t):
        y = _dot_nt(cm16, h_ref[...].astype(BF16)) * from_start
        y_in = []
        for e in range(SSD_GHEADS):
            diff = cs[:, e:e + 1] - cs_t[e:e + 1, :]
            dec = jnp.exp(jnp.where(causal, diff, -jnp.inf))
            mm = (scores * dec * dt_t[e:e + 1, :]).astype(BF16)
            if e % 2 == 0:
                mm_even = mm
            else:
                xp = xs[:, (e - 1) * SSD_HD:(e + 1) * SSD_HD]
                rhs = jnp.concatenate([jnp.where(first_half, xp, 0.0), jnp.where(first_half, 0.0, xp)], axis=0)
                y_in.append(_dot(jnp.concatenate([mm_even, mm], axis=1), rhs.astype(BF16)))
            bw = (bm * w_end[:, e:e + 1]).astype(BF16)
            st = _dot(xs_t[e * SSD_HD:(e + 1) * SSD_HD, :].astype(BF16), bw)
            cdec = jnp.exp(cs[L - 1:L, e:e + 1])
            hs = slice(e * SSD_HD, (e + 1) * SSD_HD)
            h_ref[hs, :] = h_ref[hs, :] * cdec + st
        y = y + jnp.concatenate(y_in, axis=1) + xs * dsk_ref[...]
        z = z_ref[pl.ds(r0, L), :].astype(F32)
        u = y * (z * _sigmoid(z))
        ms = jnp.mean(u * u, axis=-1, keepdims=True)
        o_ref[pl.ds(r0, L), :] = (u * lax.rsqrt(ms + EPS) * gout_ref[...]).astype(o_ref.dtype)

    return local, carried


def ssd_mixer(proj, dt_all, cw_g, cb_g, dtb_g, alog_g, dsk_g, gout_g, expand, bsz, s):
    assert s % SSD_CHUNK == 0
    nc = s // SSD_CHUNK
    gps = SSD_GROUPS_PER_STEP
    gw = SSD_GW * gps
    nw = SSD_STATE * gps
    z_blk = 0
    xs_blk = SSD_INNER // gw
    b_blk = 2 * SSD_INNER // nw
    c_blk = b_blk + SSD_GROUPS // gps

    def seq(width, off):
        return pl.BlockSpec((None, s, width), lambda b, g: (b, 0, off + g))

    def par(rows, width):
        return pl.BlockSpec((gps, rows, width), lambda b, g: (g, 0, 0))

    return pl.pallas_call(
        functools.partial(_ssd_kernel, nc=nc, unroll=_pick(nc, (8, 4, 2, 1)), groups=gps),
        grid=(bsz, SSD_GROUPS // gps),
        in_specs=[seq(gw, z_blk), seq(gw, xs_blk), seq(nw, b_blk), seq(nw, c_blk),
                  seq(LANES * gps, 0),
                  par(SSD_CONV, SSD_CONV_GW), par(1, SSD_CONV_GW), par(1, LANES), par(1, LANES),
                  par(1, SSD_GW), par(1, SSD_GW),
                  pl.BlockSpec((LANES, SSD_GW), lambda b, g: (0, 0))],
        out_specs=seq(gw, 0),
        out_shape=jax.ShapeDtypeStruct((bsz, s, SSD_INNER), BF16),
        scratch_shapes=[pltpu.VMEM((gps, SSD_GW, SSD_STATE), F32)],
        compiler_params=_params("parallel", "parallel"),
        name="ssd_mixer",
    )(proj, proj, proj, proj, dt_all, cw_g, cb_g, dtb_g, alog_g, dsk_g, gout_g, expand)


def _ssd_group_params(conv_w, conv_b, dt_bias, a_log, d_skip, g_out):
    g, gh, gw, n = SSD_GROUPS, SSD_GHEADS, SSD_GW, SSD_STATE

    def conv_cols(a):
        xs = a[..., :SSD_INNER].reshape(a.shape[:-1] + (g, gw))
        bb = a[..., SSD_INNER:SSD_INNER + g * n].reshape(a.shape[:-1] + (g, n))
        cc = a[..., SSD_INNER + g * n:].reshape(a.shape[:-1] + (g, n))
        return jnp.moveaxis(jnp.concatenate([xs, bb, cc], axis=-1), -2, 0)

    cw_g = conv_cols(conv_w.astype(F32))
    cb_g = conv_cols(conv_b.astype(F32)[None, :])

    def per_head(a):
        return jnp.pad(a.astype(F32).reshape(g, 1, gh), ((0, 0), (0, 0), (0, LANES - gh)))

    dsk_g = jnp.repeat(d_skip.astype(F32), SSD_HD).reshape(g, 1, gw)
    gout_g = g_out.astype(F32).reshape(g, 1, gw)
    expand = (jnp.arange(LANES)[:, None] == (jnp.arange(gw)[None, :] // SSD_HD)).astype(BF16)
    return cw_g, cb_g, per_head(dt_bias), per_head(a_log), dsk_g, gout_g, expand


def _router_kernel(x_ref, g_ref, wrt_ref, br_ref, h_ref, route_ref, runs_ref, cnt_ref, carry_ref, *, tm):
    @pl.when(pl.program_id(0) == 0)
    def _():
        carry_ref[...] = jnp.zeros_like(carry_ref)

    ne = N_EXPERTS
    hn = _rms(x_ref[...], g_ref[...])
    h_hi = hn.astype(BF16)
    h_ref[...] = h_hi
    h_lo = (hn - h_hi.astype(F32)).astype(BF16)
    wr = wrt_ref[...]
    w_hi = wr.astype(BF16)
    w_lo = (wr - w_hi.astype(F32)).astype(BF16)
    logits = _dot_nt(w_hi, h_hi) + _dot_nt(w_hi, h_lo) + _dot_nt(w_lo, h_hi) + br_ref[:, 0:1]
    sub = lax.broadcasted_iota(jnp.int32, (ne, tm), 0)
    l1 = jnp.max(logits, axis=0, keepdims=True)
    i1 = jnp.min(jnp.where(logits == l1, sub, ne), axis=0, keepdims=True)
    rest = jnp.where(sub == i1, -jnp.inf, logits)
    l2 = jnp.max(rest, axis=0, keepdims=True)
    i2 = jnp.min(jnp.where(rest == l2, sub, ne), axis=0, keepdims=True)
    e2 = jnp.exp(l2 - l1)
    g1 = 1.0 / (1.0 + e2)
    g2 = e2 / (1.0 + e2)
    sel = jnp.where((sub == i1) | (sub == i2), 1.0, 0.0)
    before = (lax.broadcasted_iota(jnp.int32, (tm, tm), 0) < lax.broadcasted_iota(jnp.int32, (tm, tm), 1))
    prefix = _dot(sel.astype(BF16), jnp.where(before, 1.0, 0.0).astype(BF16))
    lane = lax.broadcasted_iota(jnp.int32, (ne, LANES), 1)
    subl = lax.broadcasted_iota(jnp.int32, (ne, LANES), 0)
    n_e = (jnp.sum(sel, axis=1, keepdims=True) + jnp.zeros((ne, LANES), F32)).astype(jnp.int32)
    n_pad = (n_e + (SUBLANES - 1)) & (-SUBLANES)
    off = jnp.zeros((ne, LANES), jnp.int32)
    for j in range(ne - 1):
        off = off + jnp.where(subl > j, n_pad[j:j + 1, :], 0)
    slot = prefix + off[:, 0:1].astype(F32)
    p1 = jnp.sum(jnp.where(sub == i1, slot, 0.0), axis=0, keepdims=True)
    p2 = jnp.sum(jnp.where(sub == i2, slot, 0.0), axis=0, keepdims=True)
    route_ref[...] = jnp.where(sub == 0, p1, jnp.where(sub == 1, p2, jnp.where(
        sub == 2, g1, jnp.where(sub == 3, g2, 0.0))))
    carry = carry_ref[...]
    runs_ref[...] = jnp.where(lane == 0, off, jnp.where(lane == 1, carry, jnp.where(lane == 2, n_pad, 0)))
    carry_ref[...] = carry + n_pad
    cnt_ref[...] = carry + n_pad


def moe_router(x, g, w_router, b_router, tm):
    t, d = x.shape
    ne = N_EXPERTS
    wrt = w_router.astype(F32).T
    br = jnp.broadcast_to(b_router.astype(F32)[:, None], (ne, LANES))
    return pl.pallas_call(
        functools.partial(_router_kernel, tm=tm),
        grid=(t // tm,),
        in_specs=[pl.BlockSpec((tm, d), lambda i: (i, 0)),
                  pl.BlockSpec((1, d), lambda i: (0, 0)),
                  pl.BlockSpec((ne, d), lambda i: (0, 0)),
                  pl.BlockSpec((ne, LANES), lambda i: (0, 0))],
        out_specs=[pl.BlockSpec((tm, d), lambda i: (i, 0)),
                   pl.BlockSpec((ne, tm), lambda i: (0, i)),
                   pl.BlockSpec((ne, LANES), lambda i: (i, 0)),
                   pl.BlockSpec((ne, LANES), lambda i: (0, 0))],
        out_shape=[jax.ShapeDtypeStruct((t, d), BF16),
                   jax.ShapeDtypeStruct((ne, t), F32),
                   jax.ShapeDtypeStruct((t // tm * ne, LANES), jnp.int32),
                   jax.ShapeDtypeStruct((ne, LANES), jnp.int32)],
        scratch_shapes=[pltpu.VMEM((ne, LANES), jnp.int32)],
        compiler_params=_params("arbitrary"),
        name="moe_router",
    )(x, g.reshape(1, d), wrt, br)


def _pieces(length, lo_bit, hi_bit):
    for b in range(lo_bit, hi_bit):
        n = 1 << b
        yield n, length & (n - 1), ((length >> b) & 1) == 1


def _run_copies(tab_ref, tile, tile_ref, buf_ref, sem, tm, to_buf, wait):
    for e in range(N_EXPERTS):
        base = (tile * N_EXPERTS + e) * 3
        off, dst, rows = tab_ref[base], tab_ref[base + 1], tab_ref[base + 2]
        for n, lo, present in _pieces(rows, SUBLANES.bit_length() - 1, tm.bit_length()):
            in_tile = tile_ref.at[pl.ds(pl.multiple_of(off + lo, SUBLANES), n), :]
            in_buf = buf_ref.at[pl.ds(pl.multiple_of(dst + lo, SUBLANES), n), :]
            cp = pltpu.make_async_copy(in_tile, in_buf, sem) if to_buf else pltpu.make_async_copy(in_buf, in_tile, sem)

            @pl.when(present)
            def _():
                if wait:
                    cp.wait()
                else:
                    cp.start()


def _dispatch_kernel(tab_ref, meta_ref, route_ref, h_ref, xb_ref, srt_ref, zero_ref, sem, zsem, *, tm, n_tiles):
    i = pl.program_id(0)
    nt = pl.num_programs(0)
    slot = i % 2
    rs = srt_ref.shape[1]

    def runs(tile, slot, wait):
        _run_copies(tab_ref, tile, srt_ref.at[slot], xb_ref, sem.at[slot], tm, True, wait)

    @pl.when(i >= 2)
    def _():
        runs(i - 2, slot, True)

    pos = route_ref[0:2, :].astype(jnp.int32)
    row = lax.broadcasted_iota(jnp.int32, (rs, tm), 0)
    onehot = jnp.where((row == pos[0:1, :]) | (row == pos[1:2, :]), 1.0, 0.0).astype(BF16)
    srt_ref[slot] = _dot(onehot, h_ref[...])
    runs(i, slot, False)

    @pl.when(i == nt - 1)
    def _():
        zero_ref[...] = jnp.zeros_like(zero_ref)
        zrows = zero_ref.shape[0]
        per = EXPERT_ROWS // zrows
        first_unused = meta_ref[2 * N_EXPERTS] * per
        for wait in (False, True):
            def go(cp, cond):
                @pl.when(cond)
                def _():
                    if wait:
                        cp.wait()
                    else:
                        cp.start()

            for e in range(N_EXPERTS):
                c = meta_ref[e]
                first = meta_ref[N_EXPERTS + e]
                tail = (-c) & (EXPERT_ROWS - 1)
                for n, lo, present in _pieces(tail, SUBLANES.bit_length() - 1, EXPERT_ROWS.bit_length() - 1):
                    row0 = pl.multiple_of(first + c + lo, SUBLANES)
                    go(pltpu.make_async_copy(zero_ref.at[pl.ds(0, n), :], xb_ref.at[pl.ds(row0, n), :], zsem), present)
            for j in range(N_EXPERTS * per):
                row0 = pl.multiple_of(jnp.minimum(first_unused + j, n_tiles * per - 1) * zrows, zrows)
                go(pltpu.make_async_copy(zero_ref, xb_ref.at[pl.ds(row0, zrows), :], zsem),
                   first_unused + j < n_tiles * per)

        @pl.when(i >= 1)
        def _():
            runs(i - 1, 1 - slot, True)

        runs(i, slot, True)


def moe_dispatch(tab, meta, route, h, tm, n_tiles):
    t, d = h.shape
    ne = route.shape[0]
    rs = TOP_K * tm + LANES
    return pl.pallas_call(
        functools.partial(_dispatch_kernel, tm=tm, n_tiles=n_tiles),
        grid=(t // tm,),
        in_specs=[pl.BlockSpec(memory_space=pltpu.SMEM),
                  pl.BlockSpec(memory_space=pltpu.SMEM),
                  pl.BlockSpec((ne, tm), lambda i: (0, i)),
                  pl.BlockSpec((tm, d), lambda i: (i, 0))],
        out_specs=pl.BlockSpec(memory_space=pl.ANY),
        out_shape=jax.ShapeDtypeStruct((n_tiles * EXPERT_ROWS, d), F32),
        scratch_shapes=[pltpu.VMEM((2, rs, d), F32),
                        pltpu.VMEM((EXPERT_ROWS // 2, d), F32),
                        pltpu.SemaphoreType.DMA((2,)), pltpu.SemaphoreType.DMA(())],
        compiler_params=_params("arbitrary"),
        name="moe_dispatch",
    )(tab, meta, route, h)


def _expert_kernel(te_ref, nr_ref, xb_ref, wg_ref, wu_ref, wd_ref, o_ref):
    rows = nr_ref[pl.program_id(0)]
    r, rc = EXPERT_ROWS, EXPERT_ROW_CHUNK
    first = pl.program_id(1) == 0
    full = rows == r

    def full_tile():
        return _swiglu(xb_ref[...].astype(BF16), wg_ref[...].astype(BF16), wu_ref[...].astype(BF16),
                       wd_ref[...].astype(BF16))

    @pl.when(jnp.logical_and(full, first))
    def _():
        o_ref[...] = full_tile()

    @pl.when(jnp.logical_and(full, jnp.logical_not(first)))
    def _():
        o_ref[...] += full_tile()

    @pl.when(jnp.logical_and(first, jnp.logical_not(full)))
    def _():
        o_ref[...] = jnp.zeros_like(o_ref)

    @pl.when(jnp.logical_and(rows > 0, rows < r))
    def _():
        wg, wu, wd = wg_ref[...].astype(BF16), wu_ref[...].astype(BF16), wd_ref[...].astype(BF16)
        for j in range(r // rc):
            @pl.when(j * rc < rows)
            def _():
                _swiglu_accumulate(xb_ref[j * rc:(j + 1) * rc, :].astype(BF16), wg, wu, wd,
                                   o_ref.at[j * rc:(j + 1) * rc, :])


def expert_ffn(tile_expert, tile_rows, xb, wg, wu, wd, tf):
    n_rows, d = xb.shape
    ff = wg.shape[2]
    r = EXPERT_ROWS
    nf = ff // tf

    def wcol(c, f, te, nr):
        return (te[c], 0, jnp.where(nr[c] > 0, f, nf - 1))

    def wrow(c, f, te, nr):
        return (te[c], jnp.where(nr[c] > 0, f, nf - 1), 0)

    grid_spec = pltpu.PrefetchScalarGridSpec(
        num_scalar_prefetch=2,
        grid=(n_rows // r, nf),
        in_specs=[pl.BlockSpec((r, d), lambda c, f, te, nr: (c, 0)),
                  pl.BlockSpec((None, d, tf), wcol),
                  pl.BlockSpec((None, d, tf), wcol),
                  pl.BlockSpec((None, tf, d), wrow)],
        out_specs=pl.BlockSpec((r, d), lambda c, f, te, nr: (c, 0)),
    )
    return pl.pallas_call(
        _expert_kernel,
        grid_spec=grid_spec,
        out_shape=jax.ShapeDtypeStruct((n_rows, d), F32),
        compiler_params=_params("parallel", "arbitrary"),
        name="expert_ffn",
    )(tile_expert, tile_rows, xb, wg, wu, wd)


def _combine_kernel(tab_ref, route_ref, x_ref, g_ref, yb_ref, o_ref, srt_ref, sem, *, tm):
    i = pl.program_id(0)
    nt = pl.num_programs(0)
    slot = i % 2
    rs = srt_ref.shape[1]

    def runs(tile, slot, wait):
        _run_copies(tab_ref, tile, srt_ref.at[slot], yb_ref, sem.at[slot], tm, False, wait)

    @pl.when(i == 0)
    def _():
        srt_ref[...] = jnp.zeros_like(srt_ref)
        runs(0, 0, False)

    @pl.when(i + 1 < nt)
    def _():
        runs(i + 1, 1 - slot, False)

    runs(i, slot, True)
    aux = route_ref[...]
    aux_t = jnp.concatenate([aux, jnp.zeros((LANES - aux.shape[0], tm), F32)], axis=0).T
    srt = srt_ref[slot].astype(BF16)
    col = lax.broadcasted_iota(jnp.int32, (tm, rs), 1)
    ys = []
    for k in range(TOP_K):
        onehot = jnp.where(col == aux_t[:, k:k + 1].astype(jnp.int32), 1.0, 0.0).astype(BF16)
        ys.append(aux_t[:, TOP_K + k:TOP_K + k + 1] * _dot(onehot, srt))
    o_ref[...] = _rms(x_ref[...] + (ys[0] + ys[1]), g_ref[...])


def combine_final(tab, route, x, g, yb, tm):
    t, d = x.shape
    ne = route.shape[0]
    rs = TOP_K * tm + LANES
    return pl.pallas_call(
        functools.partial(_combine_kernel, tm=tm),
        grid=(t // tm,),
        in_specs=[pl.BlockSpec(memory_space=pltpu.SMEM),
                  pl.BlockSpec((ne, tm), lambda i: (0, i)),
                  pl.BlockSpec((tm, d), lambda i: (i, 0)),
                  pl.BlockSpec((1, d), lambda i: (0, 0)),
                  pl.BlockSpec(memory_space=pl.ANY)],
        out_specs=pl.BlockSpec((tm, d), lambda i: (i, 0)),
        out_shape=jax.ShapeDtypeStruct((t, d), F32),
        scratch_shapes=[pltpu.VMEM((2, rs, d), F32), pltpu.SemaphoreType.DMA((2,))],
        compiler_params=_params("arbitrary"),
        name="combine_final",
    )(tab, route, x, g.reshape(1, d), yb)


def _pick(n, prefs):
    for p in prefs:
        if n % p == 0:
            return p
    return n


class _Tiles(NamedTuple):
    in_rows: int
    proj_rows: int
    out_rows: int
    mem_rows: int
    moe_rows: int
    moba_cols: int
    ssd_cols: int
    ffn_cols: int
    expert_cols: int


def _choose_tiles(t, mem_rows):
    return _Tiles(in_rows=_pick(t, (1024, 512, 256)), proj_rows=_pick(t, (512, 256)),
                  out_rows=_pick(t, (1024, 512, 256)),
                  mem_rows=_pick(mem_rows, (1024, 512, 256)), moe_rows=_pick(t, (512, 256)),
                  moba_cols=1280, ssd_cols=2048, ffn_cols=1792, expert_cols=512)


class _ExpertLayout(NamedTuple):
    n_tiles: int
    tile_expert: Any
    tile_rows: Any
    run_table: Any
    meta: Any


def _expert_layout(runs, cnt, t, tm):
    r, rc = EXPERT_ROWS, EXPERT_ROW_CHUNK
    nt = t // tm
    n_tiles = -(-(t * TOP_K + nt * N_EXPERTS * (SUBLANES - 1)) // r) + N_EXPERTS
    tiles_per = (cnt + r - 1) // r
    ends = jnp.cumsum(tiles_per)
    first_row = (ends - tiles_per) * r
    n_used = ends[-1].astype(jnp.int32)
    step = jnp.minimum(jnp.arange(n_tiles, dtype=jnp.int32), n_used - 1)
    tile_expert = jnp.sum(step[:, None] >= ends[None, :], axis=1).astype(jnp.int32)
    rows_left = jnp.sum(jnp.where(tile_expert[:, None] == jnp.arange(N_EXPERTS)[None, :],
                                  (cnt + first_row)[None, :], 0), axis=1) - step * r
    tile_rows = jnp.where(jnp.arange(n_tiles) < n_used, jnp.clip((rows_left + rc - 1) // rc * rc, 0, r), 0)
    tab = runs.reshape(nt, N_EXPERTS, LANES)[:, :, :3]
    tab = tab.at[:, :, 1].add(first_row[None, :]).reshape(-1).astype(jnp.int32)
    meta = jnp.concatenate([cnt, first_row, n_used.reshape(1)]).astype(jnp.int32)
    return _ExpertLayout(n_tiles, tile_expert, tile_rows.astype(jnp.int32), tab, meta)


def kernel(x, mem, g_mix, g_mem, w_in_moba, w_in_ssd, w_mem_kv, w_out, rel_bias, conv_w, conv_b, dt_bias, a_log, d_skip, g_ssd_out, g_ffn, w_ffn_gate, w_ffn_up, w_ffn_down, w_router, b_router, w_exp_gate, w_exp_up, w_exp_down, g_final):
    bsz, s, d = x.shape
    t = bsz * s
    m_len = mem.shape[1]
    tiles = _choose_tiles(t, bsz * m_len)
    xf = x.reshape(t, d).astype(F32)
    memf = mem.reshape(bsz * m_len, d).astype(F32)

    proj0, = norm_matmul(xf, g_mix[0], [w_in_moba[0].astype(BF16)], [BF16], tiles.in_rows, [tiles.moba_cols])
    kv0, = norm_matmul(memf, g_mem[0], [w_mem_kv[0].astype(BF16)], [BF16], tiles.mem_rows, [2 * MEM_WIDTH])
    proj0 = proj0.reshape(bsz, s, -1)
    y_tok = moba_attention(proj0, _moba_bias_tables(rel_bias), bsz, s)
    y_mem = memory_attention(proj0, 3 * MOBA_WIDTH, kv0.reshape(bsz, m_len, -1), bsz, s)
    x1 = out_projection(xf, y_tok.reshape(t, -1), y_mem.reshape(t, -1), w_out[0].astype(BF16), tiles.out_rows)
    x1 = dense_ffn(x1, g_ffn[0], w_ffn_gate[0].astype(BF16), w_ffn_up[0].astype(BF16),
                   w_ffn_down[0].astype(BF16), tiles.proj_rows, tiles.ffn_cols)

    n_zx = SSD_INNER + SSD_INNER + 2 * SSD_GROUPS * SSD_STATE
    w1 = w_in_ssd[0]
    w_zx = w1[:, :n_zx].astype(BF16)
    w_qm = w1[:, n_zx + SSD_HEADS:].astype(BF16)
    w_dt = jnp.pad(w1[:, n_zx:n_zx + SSD_HEADS].reshape(d, SSD_GROUPS, SSD_GHEADS),
                   ((0, 0), (0, 0), (0, LANES - SSD_GHEADS))).reshape(d, SSD_GROUPS * LANES).astype(BF16)
    proj1, q_mem, dt_all = norm_matmul(x1, g_mix[1], [w_zx, w_qm, w_dt], [BF16, BF16, F32], tiles.in_rows,
                                       [tiles.ssd_cols, MEM_WIDTH, SSD_GROUPS * LANES])
    proj1, q_mem, dt_all = (a.reshape(bsz, s, -1) for a in (proj1, q_mem, dt_all))
    kv1, = norm_matmul(memf, g_mem[1], [w_mem_kv[1].astype(BF16)], [BF16], tiles.mem_rows, [2 * MEM_WIDTH])
    y_tok = ssd_mixer(proj1, dt_all, *_ssd_group_params(conv_w[0], conv_b[0], dt_bias[0], a_log[0],
                                                         d_skip[0], g_ssd_out[0]), bsz, s)
    y_mem = memory_attention(q_mem, 0, kv1.reshape(bsz, m_len, -1), bsz, s)
    x2 = out_projection(x1, y_tok.reshape(t, -1), y_mem.reshape(t, -1), w_out[1].astype(BF16), tiles.out_rows)

    h, route, runs, counts = moe_router(x2, g_ffn[1], w_router[0], b_router[0], tiles.moe_rows)
    lay = _expert_layout(runs, counts[:, 0], t, tiles.moe_rows)
    xb = moe_dispatch(lay.run_table, lay.meta, route, h, tiles.moe_rows, lay.n_tiles)
    yb = expert_ffn(lay.tile_expert, lay.tile_rows, xb, w_exp_gate[0], w_exp_up[0], w_exp_down[0], tiles.expert_cols)
    out = combine_final(lay.run_table, route, x2, g_final, yb, tiles.moe_rows)
    return out.reshape(bsz, s, d).astype(x.dtype)
```

```python
import functools
import math
from typing import Any, NamedTuple

import jax
import jax.numpy as jnp
from jax import lax
from jax.experimental import pallas as pl
from jax.experimental.pallas import tpu as pltpu

F32 = jnp.float32
BF16 = jnp.bfloat16

D_MODEL = 1024
HD = 128
MOBA_HEADS = 12
MOBA_WIDTH = MOBA_HEADS * HD
MOBA_BLOCK = 256
MOBA_TOPK = 3
MOBA_HEADS_PER_STEP = 4
REL_BUCKETS = 32
REL_MAX_DIST = 128
MEM_HEADS = 4
MEM_WIDTH = MEM_HEADS * HD
SSD_HEADS = 24
SSD_HD = 64
SSD_INNER = SSD_HEADS * SSD_HD
SSD_GROUPS = 4
SSD_STATE = 128
SSD_CONV = 4
SSD_CHUNK = 128
SSD_GROUPS_PER_STEP = 2
SSD_GHEADS = SSD_HEADS // SSD_GROUPS
SSD_GW = SSD_GHEADS * SSD_HD
SSD_CONV_GW = SSD_GW + 2 * SSD_STATE
D_FF = 3584
N_EXPERTS = 8
TOP_K = 2
EPS = 1e-6

LOG2E = math.log2(math.e)
LANES = 128
SUBLANES = 8
VMEM_LIMIT = 48 * 1024 * 1024
EXPERT_ROWS = 1024
EXPERT_ROW_CHUNK = 256


def _dot(a, b):
    return jnp.dot(a, b, preferred_element_type=F32)


def _dot_nt(a, b):
    return lax.dot_general(a, b, (((1,), (1,)), ((), ())), preferred_element_type=F32)


def _split(x, pieces):
    out = []
    for _ in range(pieces):
        p = x.astype(BF16)
        out.append(p)
        x = x - p.astype(F32)
    return out


def _dot_split_l(x, w, pieces):
    return sum(_dot(p, w) for p in _split(x, pieces))


def _dot_split_r(w, x, pieces):
    return sum(_dot(w, p) for p in _split(x, pieces))


def _sigmoid(x):
    return 1.0 / (1.0 + jnp.exp(-x))


def _rms(x, g):
    ms = jnp.mean(x * x, axis=-1, keepdims=True)
    return x * lax.rsqrt(ms + EPS) * g


def _rms_rows_to(h_ref, x_ref, g_ref, rows):
    step = min(rows, 256)
    for r in range(0, rows, step):
        h_ref[r:r + step, :] = _rms(x_ref[r:r + step, :], g_ref[...]).astype(h_ref.dtype)


def _params(*sem):
    return pltpu.CompilerParams(dimension_semantics=sem, vmem_limit_bytes=VMEM_LIMIT)


def _normmm_kernel(x_ref, g_ref, *refs, rc, tns):
    w_refs, o_refs = refs[:len(tns)], refs[len(tns):]
    tm = x_ref.shape[0]

    def norm(r):
        return _rms(x_ref[r * rc:(r + 1) * rc, :], g_ref[...]).astype(BF16)

    def matmuls(r, h):
        for w_ref, o_ref, tn in zip(w_refs, o_refs, tns):
            for j in range(w_ref.shape[1] // tn):
                cols = slice(j * tn, (j + 1) * tn)
                o_ref[r * rc:(r + 1) * rc, cols] = _dot(h, w_ref[:, cols]).astype(o_ref.dtype)

    live = {}
    for step in range(tm // rc + 1):
        if step < tm // rc:
            live[step] = norm(step)
        if step >= 1:
            matmuls(step - 1, live.pop(step - 1))


def norm_matmul(x, g, ws, out_dtypes, tm, tns):
    t, d = x.shape
    return pl.pallas_call(
        functools.partial(_normmm_kernel, rc=min(tm, 256), tns=tuple(tns)),
        grid=(t // tm,),
        in_specs=[pl.BlockSpec((tm, d), lambda i: (i, 0)),
                  pl.BlockSpec((1, d), lambda i: (0, 0))]
        + [pl.BlockSpec(w.shape, lambda i: (0, 0), pipeline_mode=pl.Buffered(1)) for w in ws],
        out_specs=[pl.BlockSpec((tm, w.shape[1]), lambda i: (i, 0)) for w in ws],
        out_shape=[jax.ShapeDtypeStruct((t, w.shape[1]), dt) for w, dt in zip(ws, out_dtypes)],
        compiler_params=_params("parallel"),
        name="norm_matmul",
    )(x, g.reshape(1, d), *ws)


def _moba_kernel(q_ref, k_ref, v_ref, tab_ref, o_ref, vt_ref, km_ref, *, nb, n_sel, heads):
    for hh in range(heads):
        lanes = pl.ds(hh * HD, HD)
        _moba_head(q_ref.at[:, lanes], k_ref.at[:, lanes], v_ref.at[:, lanes], tab_ref.at[hh],
                   o_ref.at[:, lanes], vt_ref.at[hh], km_ref.at[hh], nb=nb, n_sel=n_sel)


def _moba_head(q_ref, k_ref, v_ref, tab_ref, o_ref, vt_ref, km_ref, *, nb, n_sel):
    blk = MOBA_BLOCK
    scale = HD ** -0.5 * LOG2E
    nbp = km_ref.shape[0]
    km_ref[...] = jnp.zeros_like(km_ref)
    for j in range(nb):
        rows = slice(j * blk, (j + 1) * blk)
        vt_ref[:, rows] = v_ref[rows, :].astype(F32).T.astype(BF16)
        km_ref[j:j + 1, :] = jnp.mean(k_ref[rows, :].astype(F32), axis=0, keepdims=True)
    km = km_ref[...]
    km_hi = km.astype(BF16)
    km_lo = (km - km_hi.astype(F32)).astype(BF16)
    q_all = q_ref[...]
    gate = _dot_nt(km_hi, q_all) + _dot_nt(km_lo, q_all)
    sub = lax.broadcasted_iota(jnp.int32, (nbp, blk), 0)
    bias_far = tab_ref[2, 0:1, :]

    def scores(i):
        return _dot_nt(k_ref[0:(i + 1) * blk, :], q_ref[i * blk:(i + 1) * blk, :])

    def softmax(i, s_all):
        rows = slice(i * blk, (i + 1) * blk)
        if i > 0:
            valid = sub < i
            gm = jnp.where(valid, gate[:, rows], -jnp.inf)
            rank = jnp.zeros((nbp, blk), F32)
            for jp in range(i):
                row = gm[jp:jp + 1, :]
                beats = (row > gm) | ((row == gm) & (sub > jp))
                rank = rank + jnp.where(beats, 1.0, 0.0)
            selm = jnp.where(valid & (rank < n_sel), 0.0, -jnp.inf)
        bands = []
        for j in range(i + 1):
            sj = s_all[j * blk:(j + 1) * blk, :] * scale
            if j == i:
                sj = sj + tab_ref[0]
            elif j == i - 1:
                sj = sj + (tab_ref[1] + selm[j:j + 1, :])
            else:
                sj = sj + (bias_far + selm[j:j + 1, :])
            bands.append(sj)
        s = jnp.concatenate(bands, axis=0) if i > 0 else bands[0]
        m = jnp.max(s, axis=0, keepdims=True)
        return jnp.exp2(s - m).astype(BF16)

    def output(i, p):
        nk = (i + 1) * blk
        l = _dot(jnp.ones((SUBLANES, nk), BF16), p)[0:1, :]
        acc = _dot(vt_ref[:, 0:nk], p)
        o_ref[i * blk:(i + 1) * blk, :] = (acc / l).T.astype(o_ref.dtype)

    s_live, p_live = {}, {}
    units = [sorted({u, nb - 1 - u}) for u in range((nb + 1) // 2)]
    for step in range(len(units) + 2):
        if step < len(units):
            for i in units[step]:
                s_live[i] = scores(i)
        if 1 <= step <= len(units):
            for i in units[step - 1]:
                p_live[i] = softmax(i, s_live.pop(i))
        if step >= 2:
            for i in units[step - 2]:
                output(i, p_live.pop(i))


def _t5_bucket_idx(dist):
    n = jnp.maximum(dist, 0)
    max_exact = REL_BUCKETS // 2
    large = max_exact + (jnp.log(jnp.maximum(n, 1).astype(F32) / max_exact)
                         / math.log(REL_MAX_DIST / max_exact)
                         * (REL_BUCKETS - max_exact)).astype(jnp.int32)
    large = jnp.minimum(large, REL_BUCKETS - 1)
    return jnp.where(n < max_exact, n, large)


def _bias_table_kernel(rb_ref, idx_ref, o_ref):
    h = pl.program_id(0)
    for t in range(3):
        idx = idx_ref[t]
        acc = jnp.full(idx.shape, -jnp.inf, F32)
        for b in range(REL_BUCKETS):
            acc = jnp.where(idx == b, rb_ref[h, b], acc)
        o_ref[t] = acc * LOG2E


def _moba_bias_tables(rel_bias):
    blk = MOBA_BLOCK
    loc = jnp.arange(blk)
    d0 = loc[None, :] - loc[:, None]
    idx = jnp.stack([jnp.where(d0 >= 0, _t5_bucket_idx(d0), -1),
                     _t5_bucket_idx(d0 + blk),
                     _t5_bucket_idx(d0 + 2 * blk)]).astype(jnp.int32)
    return pl.pallas_call(
        _bias_table_kernel,
        grid=(MOBA_HEADS,),
        in_specs=[pl.BlockSpec(memory_space=pltpu.SMEM),
                  pl.BlockSpec((3, blk, blk), lambda h: (0, 0, 0))],
        out_specs=pl.BlockSpec((None, 3, blk, blk), lambda h: (h, 0, 0, 0)),
        out_shape=jax.ShapeDtypeStruct((MOBA_HEADS, 3, blk, blk), F32),
        compiler_params=_params("parallel"),
        name="moba_bias_tables",
    )(rel_bias.T.astype(F32), idx)


def moba_attention(proj, tabs, bsz, s):
    assert s % MOBA_BLOCK == 0
    nb = s // MOBA_BLOCK
    nbp = -(-nb // 8) * 8
    n_sel = min(MOBA_TOPK, nb - 1)
    blk = MOBA_BLOCK

    hps = MOBA_HEADS_PER_STEP
    groups = MOBA_HEADS // hps

    def col(off):
        return pl.BlockSpec((None, s, hps * HD), lambda h, b: (b, 0, off + h))

    return pl.pallas_call(
        functools.partial(_moba_kernel, nb=nb, n_sel=n_sel, heads=hps),
        grid=(groups, bsz),
        in_specs=[col(0), col(groups), col(2 * groups),
                  pl.BlockSpec((hps, 3, blk, blk), lambda h, b: (h, 0, 0, 0))],
        out_specs=pl.BlockSpec((None, s, hps * HD), lambda h, b: (b, 0, h)),
        out_shape=jax.ShapeDtypeStruct((bsz, s, MOBA_WIDTH), BF16),
        scratch_shapes=[pltpu.VMEM((hps, HD, s), BF16),
                        pltpu.VMEM((hps, nbp, HD), F32)],
        compiler_params=_params("parallel", "parallel"),
        name="moba_attention",
    )(proj, proj, proj, tabs)


def _memattn_kernel(q_ref, kv_ref, o_ref, *, n_chunks, qc):
    scale = HD ** -0.5 * LOG2E
    ks =[kv_ref[:, h * HD:(h + 1) * HD] for h in range(MEM_HEADS)]
    vts = [kv_ref[:, MEM_WIDTH + h * HD:MEM_WIDTH + (h + 1) * HD].astype(F32).T.astype(BF16)
           for h in range(MEM_HEADS)]
    ones = jnp.ones((SUBLANES, kv_ref.shape[0]), BF16)

    def scores(h, c):
        return _dot_nt(ks[h], q_ref[c * qc:(c + 1) * qc, h * HD:(h + 1) * HD])

    def softmax(s):
        s = s * scale
        return jnp.exp2(s - jnp.max(s, axis=0, keepdims=True)).astype(BF16)

    def output(h, c, p):
        l = _dot(ones, p)[0:1, :]
        o_ref[c * qc:(c + 1) * qc, h * HD:(h + 1) * HD] = (_dot(vts[h], p) / l).T.astype(o_ref.dtype)

    work = [(h, c) for h in range(MEM_HEADS) for c in range(n_chunks)]
    s_live, p_live = {}, {}
    for step in range(len(work) + 2):
        if step < len(work):
            s_live[step] = scores(*work[step])
        if 1 <= step <= len(work):
            p_live[step - 1] = softmax(s_live.pop(step - 1))
        if step >= 2:
            output(*work[step - 2], p_live.pop(step - 2))


def memory_attention(proj, q_col, kv, bsz, s):
    m_len = kv.shape[1]
    qc = 256
    assert q_col % MEM_WIDTH == 0
    return pl.pallas_call(
        functools.partial(_memattn_kernel, n_chunks=s // qc, qc=qc),
        grid=(bsz,),
        in_specs=[pl.BlockSpec((None, s, MEM_WIDTH), lambda b: (b, 0, q_col // MEM_WIDTH)),
                  pl.BlockSpec((None, m_len, 2 * MEM_WIDTH), lambda b: (b, 0, 0))],
        out_specs=pl.BlockSpec((None, s, MEM_WIDTH), lambda b: (b, 0, 0)),
        out_shape=jax.ShapeDtypeStruct((bsz, s, MEM_WIDTH), BF16),
        compiler_params=_params("parallel"),
        name="memory_attention",
    )(proj, kv)


def _outproj_kernel(x_ref, ya_ref, yb_ref, w_ref, o_ref):
    ka = ya_ref.shape[1]
    o_ref[...] = x_ref[...] + _dot(ya_ref[...], w_ref[0:ka, :]) + _dot(yb_ref[...], w_ref[ka:, :])


def out_projection(x, ya, yb, w, tm):
    t, d = x.shape
    ka, kb = ya.shape[1], yb.shape[1]
    return pl.pallas_call(
        _outproj_kernel,
        grid=(t // tm,),
        in_specs=[pl.BlockSpec((tm, d), lambda i: (i, 0)),
                  pl.BlockSpec((tm, ka), lambda i: (i, 0)),
                  pl.BlockSpec((tm, kb), lambda i: (i, 0)),
                  pl.BlockSpec((ka + kb, d), lambda i: (0, 0))],
        out_specs=pl.BlockSpec((tm, d), lambda i: (i, 0)),
        out_shape=jax.ShapeDtypeStruct((t, d), F32),
        compiler_params=_params("parallel"),
        name="out_projection",
    )(x, ya, yb, w)


def _swiglu(h, wg, wu, wd):
    a = _dot(h, wg)
    u = _dot(h, wu)
    return _dot((a * _sigmoid(a) * u).astype(BF16), wd)


def _swiglu_accumulate(h, wg, wu, wd, acc_ref):
    acc_ref[...] += _swiglu(h, wg, wu, wd)


def _ffn_kernel(x_ref, g_ref, wg_ref, wu_ref, wd_ref, o_ref, h_ref, *, tm):
    @pl.when(pl.program_id(1) == 0)
    def _():
        _rms_rows_to(h_ref, x_ref, g_ref, tm)
        o_ref[...] = x_ref[...]

    _swiglu_accumulate(h_ref[...], wg_ref[...], wu_ref[...], wd_ref[...], o_ref)


def dense_ffn(x, g, wg, wu, wd, tm, tf):
    t, d = x.shape
    ff = wg.shape[1]
    return pl.pallas_call(
        functools.partial(_ffn_kernel, tm=tm),
        grid=(t // tm, ff // tf),
        in_specs=[pl.BlockSpec((tm, d), lambda i, f: (i, 0)),
                  pl.BlockSpec((1, d), lambda i, f: (0, 0)),
                  pl.BlockSpec((d, tf), lambda i, f: (0, f)),
                  pl.BlockSpec((d, tf), lambda i, f: (0, f)),
                  pl.BlockSpec((tf, d), lambda i, f: (f, 0))],
        out_specs=pl.BlockSpec((tm, d), lambda i, f: (i, 0)),
        out_shape=jax.ShapeDtypeStruct((t, d), F32),
        scratch_shapes=[pltpu.VMEM((tm, d), BF16)],
        compiler_params=_params("parallel", "arbitrary"),
        name="dense_ffn",
    )(x, g.reshape(1, d), wg, wu, wd)


def _ssd_kernel(z_ref, xs_ref, b_ref, c_ref, dt_ref, cw_ref, cb_ref, dtb_ref, alog_ref, dsk_ref,
                gout_ref, exp_ref, o_ref, h_ref, *, nc, unroll, groups):
    gw, n = SSD_GW, SSD_STATE
    chunks = []
    for sg in range(groups):
        wide, narrow, lanes = pl.ds(sg * gw, gw), pl.ds(sg * n, n), pl.ds(sg * LANES, LANES)
        chunks.append(_ssd_group_chunk(
            z_ref.at[:, wide], xs_ref.at[:, wide], b_ref.at[:, narrow], c_ref.at[:, narrow], dt_ref.at[:, lanes],
            cw_ref.at[sg], cb_ref.at[sg], dtb_ref.at[sg], alog_ref.at[sg], dsk_ref.at[sg], gout_ref.at[sg],
            exp_ref, o_ref.at[:, wide], h_ref.at[sg]))

    def body(it, carry):
        work = [(sg, it * unroll + u) for u in range(unroll) for sg in range(groups)]
        live = {}
        for step in range(len(work) + 1):
            if step < len(work):
                sg, c = work[step]
                live[step] = chunks[sg][0](c)
            if step >= 1:
                chunks[work[step - 1][0]][1](*live.pop(step - 1))
        return carry

    lax.fori_loop(0, nc // unroll, body, 0)


def _ssd_group_chunk(z_ref, xs_ref, b_ref, c_ref, dt_ref, cw_ref, cb_ref, dtb_ref, alog_ref, dsk_ref,
                     gout_ref, exp_ref, o_ref, h_ref):
    L = SSD_CHUNK
    gw = SSD_GW
    n = SSD_STATE
    h_ref[...] = jnp.zeros(h_ref.shape, h_ref.dtype)
    a_neg = -jnp.exp(alog_ref[...])
    causal = (lax.broadcasted_iota(jnp.int32, (L, L), 0) >= lax.broadcasted_iota(jnp.int32, (L, L), 1))
    tri = jnp.where(causal, 1.0, 0.0).astype(BF16)
    first_half = lax.broadcasted_iota(jnp.int32, (1, 2 * SSD_HD), 1) < SSD_HD
    expm = exp_ref[...]
    srcs = (xs_ref, b_ref, c_ref)
    sh_r = lax.broadcasted_iota(jnp.int32, (L, 2 * L), 0)
    sh_c = lax.broadcasted_iota(jnp.int32, (L, 2 * L), 1)
    shifts = [jnp.where(sh_c == sh_r + (L - s), 1.0, 0.0).astype(BF16) for s in range(1, SSD_CONV)]

    def local(c):
        r0 = pl.multiple_of(c * L, L)
        rq = pl.multiple_of(jnp.maximum(r0 - L, 0), L)
        cur = jnp.concatenate([ref[pl.ds(r0, L), :] for ref in srcs], axis=1)
        prev = jnp.concatenate([ref[pl.ds(rq, L), :] for ref in srcs], axis=1)
        prev = jnp.where(c > 0, prev, jnp.zeros_like(prev))
        ext = jnp.concatenate([prev, cur], axis=0)
        conv = cb_ref[...] + cw_ref[SSD_CONV - 1:SSD_CONV, :] * cur.astype(F32)
        for s in range(1, SSD_CONV):
            k = SSD_CONV - 1 - s
            conv = conv + cw_ref[k:k + 1, :] * _dot(shifts[s - 1], ext)
        act = conv * _sigmoid(conv)
        xs = act[:, 0:gw]
        bm = act[:, gw:gw + n]
        cm = act[:, gw + n:gw + 2 * n]

        dtr = dt_ref[pl.ds(r0, L), :] + dtb_ref[...]
        dt = jnp.maximum(dtr, 0.0) + jnp.log(1.0 + jnp.exp(-jnp.abs(dtr)))
        la = dt * a_neg
        cs = _dot_split_r(tri, la, 3)
        cs_t = cs.T
        dt_t = dt.T
        bm16 = bm.astype(BF16)
        cm16 = cm.astype(BF16)
        scores = _dot_nt(cm16, bm16)
        from_start = _dot_split_l(jnp.exp(cs), expm, 2)
        w_end = dt * jnp.exp(cs[L - 1:L, :] - cs)
        xs_t = xs.T
        return r0, xs, bm, cm16, cs, cs_t, dt_t, scores, from_start, w_end, xs_t

    def carried(r0, xs, bm, cm16, cs, cs_t, dt_t, scores, from_start, w_end, xs_t):
        y = _dot_nt(cm16, h_ref[...].astype(BF16)) * from_start
        y_in = []
        for e in range(SSD_GHEADS):
            diff = cs[:, e:e + 1] - cs_t[e:e + 1, :]
            dec = jnp.exp(jnp.where(causal, diff, -jnp.inf))
            mm = (scores * dec * dt_t[e:e + 1, :]).astype(BF16)
            if e % 2 == 0:
                mm_even = mm
            else:
                xp = xs[:, (e - 1) * SSD_HD:(e + 1) * SSD_HD]
                rhs = jnp.concatenate([jnp.where(first_half, xp, 0.0), jnp.where(first_half, 0.0, xp)], axis=0)
                y_in.append(_dot(jnp.concatenate([mm_even, mm], axis=1), rhs.astype(BF16)))
            bw = (bm * w_end[:, e:e + 1]).astype(BF16)
            st = _dot(xs_t[e * SSD_HD:(e + 1) * SSD_HD, :].astype(BF16), bw)
            cdec = jnp.exp(cs[L - 1:L, e:e + 1])
            hs = slice(e * SSD_HD, (e + 1) * SSD_HD)
            h_ref[hs, :] = h_ref[hs, :] * cdec + st
        y = y + jnp.concatenate(y_in, axis=1) + xs * dsk_ref[...]
        z = z_ref[pl.ds(r0, L), :].astype(F32)
        u = y * (z * _sigmoid(z))
        ms = jnp.mean(u * u, axis=-1, keepdims=True)
        o_ref[pl.ds(r0, L), :] = (u * lax.rsqrt(ms + EPS) * gout_ref[...]).astype(o_ref.dtype)

    return local, carried


def ssd_mixer(proj, dt_all, cw_g, cb_g, dtb_g, alog_g, dsk_g, gout_g, expand, bsz, s):
    assert s % SSD_CHUNK == 0
    nc = s // SSD_CHUNK
    gps = SSD_GROUPS_PER_STEP
    gw = SSD_GW * gps
    nw = SSD_STATE * gps
    z_blk = 0
    xs_blk = SSD_INNER // gw
    b_blk = 2 * SSD_INNER // nw
    c_blk = b_blk + SSD_GROUPS // gps

    def seq(width, off):
        return pl.BlockSpec((None, s, width), lambda b, g: (b, 0, off + g))

    def par(rows, width):
        return pl.BlockSpec((gps, rows, width), lambda b, g: (g, 0, 0))

    return pl.pallas_call(
        functools.partial(_ssd_kernel, nc=nc, unroll=_pick(nc, (8, 4, 2, 1)), groups=gps),
        grid=(bsz, SSD_GROUPS // gps),
        in_specs=[seq(gw, z_blk), seq(gw, xs_blk), seq(nw, b_blk), seq(nw, c_blk),
                  seq(LANES * gps, 0),
                  par(SSD_CONV, SSD_CONV_GW), par(1, SSD_CONV_GW), par(1, LANES), par(1, LANES),
                  par(1, SSD_GW), par(1, SSD_GW),
                  pl.BlockSpec((LANES, SSD_GW), lambda b, g: (0, 0))],
        out_specs=seq(gw, 0),
        out_shape=jax.ShapeDtypeStruct((bsz, s, SSD_INNER), BF16),
        scratch_shapes=[pltpu.VMEM((gps, SSD_GW, SSD_STATE), F32)],
        compiler_params=_params("parallel", "parallel"),
        name="ssd_mixer",
    )(proj, proj, proj, proj, dt_all, cw_g, cb_g, dtb_g, alog_g, dsk_g, gout_g, expand)


def _ssd_group_params(conv_w, conv_b, dt_bias, a_log, d_skip, g_out):
    g, gh, gw, n = SSD_GROUPS, SSD_GHEADS, SSD_GW, SSD_STATE

    def conv_cols(a):
        xs = a[..., :SSD_INNER].reshape(a.shape[:-1] + (g, gw))
        bb = a[..., SSD_INNER:SSD_INNER + g * n].reshape(a.shape[:-1] + (g, n))
        cc = a[..., SSD_INNER + g * n:].reshape(a.shape[:-1] + (g, n))
        return jnp.moveaxis(jnp.concatenate([xs, bb, cc], axis=-1), -2, 0)

    cw_g = conv_cols(conv_w.astype(F32))
    cb_g = conv_cols(conv_b.astype(F32)[None, :])

    def per_head(a):
        return jnp.pad(a.astype(F32).reshape(g, 1, gh), ((0, 0), (0, 0), (0, LANES - gh)))

    dsk_g = jnp.repeat(d_skip.astype(F32), SSD_HD).reshape(g, 1, gw)
    gout_g = g_out.astype(F32).reshape(g, 1, gw)
    expand = (jnp.arange(LANES)[:, None] == (jnp.arange(gw)[None, :] // SSD_HD)).astype(BF16)
    return cw_g, cb_g, per_head(dt_bias), per_head(a_log), dsk_g, gout_g, expand


def _router_kernel(x_ref, g_ref, wrt_ref, br_ref, h_ref, route_ref, runs_ref, cnt_ref, carry_ref, *, tm):
    @pl.when(pl.program_id(0) == 0)
    def _():
        carry_ref[...] = jnp.zeros_like(carry_ref)

    ne = N_EXPERTS
    hn = _rms(x_ref[...], g_ref[...])
    h_hi = hn.astype(BF16)
    h_ref[...] = h_hi
    h_lo = (hn - h_hi.astype(F32)).astype(BF16)
    wr = wrt_ref[...]
    w_hi = wr.astype(BF16)
    w_lo = (wr - w_hi.astype(F32)).astype(BF16)
    logits = _dot_nt(w_hi, h_hi) + _dot_nt(w_hi, h_lo) + _dot_nt(w_lo, h_hi) + br_ref[:, 0:1]
    sub = lax.broadcasted_iota(jnp.int32, (ne, tm), 0)
    l1 = jnp.max(logits, axis=0, keepdims=True)
    i1 = jnp.min(jnp.where(logits == l1, sub, ne), axis=0, keepdims=True)
    rest = jnp.where(sub == i1, -jnp.inf, logits)
    l2 = jnp.max(rest, axis=0, keepdims=True)
    i2 = jnp.min(jnp.where(rest == l2, sub, ne), axis=0, keepdims=True)
    e2 = jnp.exp(l2 - l1)
    g1 = 1.0 / (1.0 + e2)
    g2 = e2 / (1.0 + e2)
    sel = jnp.where((sub == i1) | (sub == i2), 1.0, 0.0)
    before = (lax.broadcasted_iota(jnp.int32, (tm, tm), 0) < lax.broadcasted_iota(jnp.int32, (tm, tm), 1))
    prefix = _dot(sel.astype(BF16), jnp.where(before, 1.0, 0.0).astype(BF16))
    lane = lax.broadcasted_iota(jnp.int32, (ne, LANES), 1)
    subl = lax.broadcasted_iota(jnp.int32, (ne, LANES), 0)
    n_e = (jnp.sum(sel, axis=1, keepdims=True) + jnp.zeros((ne, LANES), F32)).astype(jnp.int32)
    n_pad = (n_e + (SUBLANES - 1)) & (-SUBLANES)
    off = jnp.zeros((ne, LANES), jnp.int32)
    for j in range(ne - 1):
        off = off + jnp.where(subl > j, n_pad[j:j + 1, :], 0)
    slot = prefix + off[:, 0:1].astype(F32)
    p1 = jnp.sum(jnp.where(sub == i1, slot, 0.0), axis=0, keepdims=True)
    p2 = jnp.sum(jnp.where(sub == i2, slot, 0.0), axis=0, keepdims=True)
    route_ref[...] = jnp.where(sub == 0, p1, jnp.where(sub == 1, p2, jnp.where(
        sub == 2, g1, jnp.where(sub == 3, g2, 0.0))))
    carry = carry_ref[...]
    runs_ref[...] = jnp.where(lane == 0, off, jnp.where(lane == 1, carry, jnp.where(lane == 2, n_pad, 0)))
    carry_ref[...] = carry + n_pad
    cnt_ref[...] = carry + n_pad


def moe_router(x, g, w_router, b_router, tm):
    t, d = x.shape
    ne = N_EXPERTS
    wrt = w_router.astype(F32).T
    br = jnp.broadcast_to(b_router.astype(F32)[:, None], (ne, LANES))
    return pl.pallas_call(
        functools.partial(_router_kernel, tm=tm),
        grid=(t // tm,),
        in_specs=[pl.BlockSpec((tm, d), lambda i: (i, 0)),
                  pl.BlockSpec((1, d), lambda i: (0, 0)),
                  pl.BlockSpec((ne, d), lambda i: (0, 0)),
                  pl.BlockSpec((ne, LANES), lambda i: (0, 0))],
        out_specs=[pl.BlockSpec((tm, d), lambda i: (i, 0)),
                   pl.BlockSpec((ne, tm), lambda i: (0, i)),
                   pl.BlockSpec((ne, LANES), lambda i: (i, 0)),
                   pl.BlockSpec((ne, LANES), lambda i: (0, 0))],
        out_shape=[jax.ShapeDtypeStruct((t, d), BF16),
                   jax.ShapeDtypeStruct((ne, t), F32),
                   jax.ShapeDtypeStruct((t // tm * ne, LANES), jnp.int32),
                   jax.ShapeDtypeStruct((ne, LANES), jnp.int32)],
        scratch_shapes=[pltpu.VMEM((ne, LANES), jnp.int32)],
        compiler_params=_params("arbitrary"),
        name="moe_router",
    )(x, g.reshape(1, d), wrt, br)


def _pieces(length, lo_bit, hi_bit):
    for b in range(lo_bit, hi_bit):
        n = 1 << b
        yield n, length & (n - 1), ((length >> b) & 1) == 1


def _run_copies(tab_ref, tile, tile_ref, buf_ref, sem, tm, to_buf, wait):
    for e in range(N_EXPERTS):
        base = (tile * N_EXPERTS + e) * 3
        off, dst, rows = tab_ref[base], tab_ref[base + 1], tab_ref[base + 2]
        for n, lo, present in _pieces(rows, SUBLANES.bit_length() - 1, tm.bit_length()):
            in_tile = tile_ref.at[pl.ds(pl.multiple_of(off + lo, SUBLANES), n), :]
            in_buf = buf_ref.at[pl.ds(pl.multiple_of(dst + lo, SUBLANES), n), :]
            cp = pltpu.make_async_copy(in_tile, in_buf, sem) if to_buf else pltpu.make_async_copy(in_buf, in_tile, sem)

            @pl.when(present)
            def _():
                if wait:
                    cp.wait()
                else:
                    cp.start()


def _dispatch_kernel(tab_ref, meta_ref, route_ref, h_ref, xb_ref, srt_ref, zero_ref, sem, zsem, *, tm, n_tiles):
    i = pl.program_id(0)
    nt = pl.num_programs(0)
    slot = i % 2
    rs = srt_ref.shape[1]

    def runs(tile, slot, wait):
        _run_copies(tab_ref, tile, srt_ref.at[slot], xb_ref, sem.at[slot], tm, True, wait)

    @pl.when(i >= 2)
    def _():
        runs(i - 2, slot, True)

    pos = route_ref[0:2, :].astype(jnp.int32)
    row = lax.broadcasted_iota(jnp.int32, (rs, tm), 0)
    onehot = jnp.where((row == pos[0:1, :]) | (row == pos[1:2, :]), 1.0, 0.0).astype(BF16)
    srt_ref[slot] = _dot(onehot, h_ref[...])
    runs(i, slot, False)

    @pl.when(i == nt - 1)
    def _():
        zero_ref[...] = jnp.zeros_like(zero_ref)
        zrows = zero_ref.shape[0]
        per = EXPERT_ROWS // zrows
        first_unused = meta_ref[2 * N_EXPERTS] * per
        for wait in (False, True):
            def go(cp, cond):
                @pl.when(cond)
                def _():
                    if wait:
                        cp.wait()
                    else:
                        cp.start()

            for e in range(N_EXPERTS):
                c = meta_ref[e]
                first = meta_ref[N_EXPERTS + e]
                tail = (-c) & (EXPERT_ROWS - 1)
                for n, lo, present in _pieces(tail, SUBLANES.bit_length() - 1, EXPERT_ROWS.bit_length() - 1):
                    row0 = pl.multiple_of(first + c + lo, SUBLANES)
                    go(pltpu.make_async_copy(zero_ref.at[pl.ds(0, n), :], xb_ref.at[pl.ds(row0, n), :], zsem), present)
            for j in range(N_EXPERTS * per):
                row0 = pl.multiple_of(jnp.minimum(first_unused + j, n_tiles * per - 1) * zrows, zrows)
                go(pltpu.make_async_copy(zero_ref, xb_ref.at[pl.ds(row0, zrows), :], zsem),
                   first_unused + j < n_tiles * per)

        @pl.when(i >= 1)
        def _():
            runs(i - 1, 1 - slot, True)

        runs(i, slot, True)


def moe_dispatch(tab, meta, route, h, tm, n_tiles):
    t, d = h.shape
    ne = route.shape[0]
    rs = TOP_K * tm + LANES
    return pl.pallas_call(
        functools.partial(_dispatch_kernel, tm=tm, n_tiles=n_tiles),
        grid=(t // tm,),
        in_specs=[pl.BlockSpec(memory_space=pltpu.SMEM),
                  pl.BlockSpec(memory_space=pltpu.SMEM),
                  pl.BlockSpec((ne, tm), lambda i: (0, i)),
                  pl.BlockSpec((tm, d), lambda i: (i, 0))],
        out_specs=pl.BlockSpec(memory_space=pl.ANY),
        out_shape=jax.ShapeDtypeStruct((n_tiles * EXPERT_ROWS, d), F32),
        scratch_shapes=[pltpu.VMEM((2, rs, d), F32),
                        pltpu.VMEM((EXPERT_ROWS // 2, d), F32),
                        pltpu.SemaphoreType.DMA((2,)), pltpu.SemaphoreType.DMA(())],
        compiler_params=_params("arbitrary"),
        name="moe_dispatch",
    )(tab, meta, route, h)


def _expert_kernel(te_ref, nr_ref, xb_ref, wg_ref, wu_ref, wd_ref, o_ref):
    rows = nr_ref[pl.program_id(0)]
    r, rc = EXPERT_ROWS, EXPERT_ROW_CHUNK
    first = pl.program_id(1) == 0
    full = rows == r

    def full_tile():
        return _swiglu(xb_ref[...].astype(BF16), wg_ref[...].astype(BF16), wu_ref[...].astype(BF16),
                       wd_ref[...].astype(BF16))

    @pl.when(jnp.logical_and(full, first))
    def _():
        o_ref[...] = full_tile()

    @pl.when(jnp.logical_and(full, jnp.logical_not(first)))
    def _():
        o_ref[...] += full_tile()

    @pl.when(jnp.logical_and(first, jnp.logical_not(full)))
    def _():
        o_ref[...] = jnp.zeros_like(o_ref)

    @pl.when(jnp.logical_and(rows > 0, rows < r))
    def _():
        wg, wu, wd = wg_ref[...].astype(BF16), wu_ref[...].astype(BF16), wd_ref[...].astype(BF16)
        for j in range(r // rc):
            @pl.when(j * rc < rows)
            def _():
                _swiglu_accumulate(xb_ref[j * rc:(j + 1) * rc, :].astype(BF16), wg, wu, wd,
                                   o_ref.at[j * rc:(j + 1) * rc, :])


def expert_ffn(tile_expert, tile_rows, xb, wg, wu, wd, tf):
    n_rows, d = xb.shape
    ff = wg.shape[2]
    r = EXPERT_ROWS
    nf = ff // tf

    def wcol(c, f, te, nr):
        return (te[c], 0, jnp.where(nr[c] > 0, f, nf - 1))

    def wrow(c, f, te, nr):
        return (te[c], jnp.where(nr[c] > 0, f, nf - 1), 0)

    grid_spec = pltpu.PrefetchScalarGridSpec(
        num_scalar_prefetch=2,
        grid=(n_rows // r, nf),
        in_specs=[pl.BlockSpec((r, d), lambda c, f, te, nr: (c, 0)),
                  pl.BlockSpec((None, d, tf), wcol),
                  pl.BlockSpec((None, d, tf), wcol),
                  pl.BlockSpec((None, tf, d), wrow)],
        out_specs=pl.BlockSpec((r, d), lambda c, f, te, nr: (c, 0)),
    )
    return pl.pallas_call(
        _expert_kernel,
        grid_spec=grid_spec,
        out_shape=jax.ShapeDtypeStruct((n_rows, d), F32),
        compiler_params=_params("parallel", "arbitrary"),
        name="expert_ffn",
    )(tile_expert, tile_rows, xb, wg, wu, wd)


def _combine_kernel(tab_ref, route_ref, x_ref, g_ref, yb_ref, o_ref, srt_ref, sem, *, tm):
    i = pl.program_id(0)
    nt = pl.num_programs(0)
    slot = i % 2
    rs = srt_ref.shape[1]

    def runs(tile, slot, wait):
        _run_copies(tab_ref, tile, srt_ref.at[slot], yb_ref, sem.at[slot], tm, False, wait)

    @pl.when(i == 0)
    def _():
        srt_ref[...] = jnp.zeros_like(srt_ref)
        runs(0, 0, False)

    @pl.when(i + 1 < nt)
    def _():
        runs(i + 1, 1 - slot, False)

    runs(i, slot, True)
    aux = route_ref[...]
    aux_t = jnp.concatenate([aux, jnp.zeros((LANES - aux.shape[0], tm), F32)], axis=0).T
    srt = srt_ref[slot].astype(BF16)
    col = lax.broadcasted_iota(jnp.int32, (tm, rs), 1)
    ys = []
    for k in range(TOP_K):
        onehot = jnp.where(col == aux_t[:, k:k + 1].astype(jnp.int32), 1.0, 0.0).astype(BF16)
        ys.append(aux_t[:, TOP_K + k:TOP_K + k + 1] * _dot(onehot, srt))
    o_ref[...] = _rms(x_ref[...] + (ys[0] + ys[1]), g_ref[...])


def combine_final(tab, route, x, g, yb, tm):
    t, d = x.shape
    ne = route.shape[0]
    rs = TOP_K * tm + LANES
    return pl.pallas_call(
        functools.partial(_combine_kernel, tm=tm),
        grid=(t // tm,),
        in_specs=[pl.BlockSpec(memory_space=pltpu.SMEM),
                  pl.BlockSpec((ne, tm), lambda i: (0, i)),
                  pl.BlockSpec((tm, d), lambda i: (i, 0)),
                  pl.BlockSpec((1, d), lambda i: (0, 0)),
                  pl.BlockSpec(memory_space=pl.ANY)],
        out_specs=pl.BlockSpec((tm, d), lambda i: (i, 0)),
        out_shape=jax.ShapeDtypeStruct((t, d), F32),
        scratch_shapes=[pltpu.VMEM((2, rs, d), F32), pltpu.SemaphoreType.DMA((2,))],
        compiler_params=_params("arbitrary"),
        name="combine_final",
    )(tab, route, x, g.reshape(1, d), yb)


def _pick(n, prefs):
    for p in prefs:
        if n % p == 0:
            return p
    return n


class _Tiles(NamedTuple):
    in_rows: int
    proj_rows: int
    out_rows: int
    mem_rows: int
    moe_rows: int
    moba_cols: int
    ssd_cols: int
    ffn_cols: int
    expert_cols: int


def _choose_tiles(t, mem_rows):
    return _Tiles(in_rows=_pick(t, (1024, 512, 256)), proj_rows=_pick(t, (512, 256)),
                  out_rows=_pick(t, (1024, 512, 256)),
                  mem_rows=_pick(mem_rows, (1024, 512, 256)), moe_rows=_pick(t, (512, 256)),
                  moba_cols=1280, ssd_cols=2048, ffn_cols=1792, expert_cols=512)


class _ExpertLayout(NamedTuple):
    n_tiles: int
    tile_expert: Any
    tile_rows: Any
    run_table: Any
    meta: Any


def _expert_layout(runs, cnt, t, tm):
    r, rc = EXPERT_ROWS, EXPERT_ROW_CHUNK
    nt = t // tm
    n_tiles = -(-(t * TOP_K + nt * N_EXPERTS * (SUBLANES - 1)) // r) + N_EXPERTS
    tiles_per = (cnt + r - 1) // r
    ends = jnp.cumsum(tiles_per)
    first_row = (ends - tiles_per) * r
    n_used = ends[-1].astype(jnp.int32)
    step = jnp.minimum(jnp.arange(n_tiles, dtype=jnp.int32), n_used - 1)
    tile_expert = jnp.sum(step[:, None] >= ends[None, :], axis=1).astype(jnp.int32)
    rows_left = jnp.sum(jnp.where(tile_expert[:, None] == jnp.arange(N_EXPERTS)[None, :],
                                  (cnt + first_row)[None, :], 0), axis=1) - step * r
    tile_rows = jnp.where(jnp.arange(n_tiles) < n_used, jnp.clip((rows_left + rc - 1) // rc * rc, 0, r), 0)
    tab = runs.reshape(nt, N_EXPERTS, LANES)[:, :, :3]
    tab = tab.at[:, :, 1].add(first_row[None, :]).reshape(-1).astype(jnp.int32)
    meta = jnp.concatenate([cnt, first_row, n_used.reshape(1)]).astype(jnp.int32)
    return _ExpertLayout(n_tiles, tile_expert, tile_rows.astype(jnp.int32), tab, meta)


def kernel(x, mem, g_mix, g_mem, w_in_moba, w_in_ssd, w_mem_kv, w_out, rel_bias, conv_w, conv_b, dt_bias, a_log, d_skip, g_ssd_out, g_ffn, w_ffn_gate, w_ffn_up, w_ffn_down, w_router, b_router, w_exp_gate, w_exp_up, w_exp_down, g_final):
    bsz, s, d = x.shape
    t = bsz * s
    m_len = mem.shape[1]
    tiles = _choose_tiles(t, bsz * m_len)
    xf = x.reshape(t, d).astype(F32)
    memf = mem.reshape(bsz * m_len, d).astype(F32)

    proj0, = norm_matmul(xf, g_mix[0], [w_in_moba[0].astype(BF16)], [BF16], tiles.in_rows, [tiles.moba_cols])
    kv0, = norm_matmul(memf, g_mem[0], [w_mem_kv[0].astype(BF16)], [BF16], tiles.mem_rows, [2 * MEM_WIDTH])
    proj0 = proj0.reshape(bsz, s, -1)
    y_tok = moba_attention(proj0, _moba_bias_tables(rel_bias), bsz, s)
    y_mem = memory_attention(proj0, 3 * MOBA_WIDTH, kv0.reshape(bsz, m_len, -1), bsz, s)
    x1 = out_projection(xf, y_tok.reshape(t, -1), y_mem.reshape(t, -1), w_out[0].astype(BF16), tiles.out_rows)
    x1 = dense_ffn(x1, g_ffn[0], w_ffn_gate[0].astype(BF16), w_ffn_up[0].astype(BF16),
                   w_ffn_down[0].astype(BF16), tiles.proj_rows, tiles.ffn_cols)

    n_zx = SSD_INNER + SSD_INNER + 2 * SSD_GROUPS * SSD_STATE
    w1 = w_in_ssd[0]
    w_zx = w1[:, :n_zx].astype(BF16)
    w_qm = w1[:, n_zx + SSD_HEADS:].astype(BF16)
    w_dt = jnp.pad(w1[:, n_zx:n_zx + SSD_HEADS].reshape(d, SSD_GROUPS, SSD_GHEADS),
                   ((0, 0), (0, 0), (0, LANES - SSD_GHEADS))).reshape(d, SSD_GROUPS * LANES).astype(BF16)
    proj1, q_mem, dt_all = norm_matmul(x1, g_mix[1], [w_zx, w_qm, w_dt], [BF16, BF16, F32], tiles.in_rows,
                                       [tiles.ssd_cols, MEM_WIDTH, SSD_GROUPS * LANES])
    proj1, q_mem, dt_all = (a.reshape(bsz, s, -1) for a in (proj1, q_mem, dt_all))
    kv1, = norm_matmul(memf, g_mem[1], [w_mem_kv[1].astype(BF16)], [BF16], tiles.mem_rows, [2 * MEM_WIDTH])
    y_tok = ssd_mixer(proj1, dt_all, *_ssd_group_params(conv_w[0], conv_b[0], dt_bias[0], a_log[0],
                                                         d_skip[0], g_ssd_out[0]), bsz, s)
    y_mem = memory_attention(q_mem, 0, kv1.reshape(bsz, m_len, -1), bsz, s)
    x2 = out_projection(x1, y_tok.reshape(t, -1), y_mem.reshape(t, -1), w_out[1].astype(BF16), tiles.out_rows)

    h, route, runs, counts = moe_router(x2, g_ffn[1], w_router[0], b_router[0], tiles.moe_rows)
    lay = _expert_layout(runs, counts[:, 0], t, tiles.moe_rows)
    xb = moe_dispatch(lay.run_table, lay.meta, route, h, tiles.moe_rows, lay.n_tiles)
    yb = expert_ffn(lay.tile_expert, lay.tile_rows, xb, w_exp_gate[0], w_exp_up[0], w_exp_down[0], tiles.expert_cols)
    out = combine_final(lay.run_table, route, x2, g_final, yb, tiles.moe_rows)
    return out.reshape(bsz, s, d).astype(x.dtype)
```

```python
import functools
import math
from typing import Any, NamedTuple

import jax
import jax.numpy as jnp
from jax import lax
from jax.experimental import pallas as pl
from jax.experimental.pallas import tpu as pltpu

F32 = jnp.float32
BF16 = jnp.bfloat16

D_MODEL = 1024
HD = 128
MOBA_HEADS = 12
MOBA_WIDTH = MOBA_HEADS * HD
MOBA_BLOCK = 256
MOBA_TOPK = 3
MOBA_HEADS_PER_STEP = 2
REL_BUCKETS = 32
REL_MAX_DIST = 128
MEM_HEADS = 4
MEM_WIDTH = MEM_HEADS * HD
SSD_HEADS = 24
SSD_HD = 64
SSD_INNER = SSD_HEADS * SSD_HD
SSD_GROUPS = 4
SSD_STATE = 128
SSD_CONV = 4
SSD_CHUNK = 128
SSD_GROUPS_PER_STEP = 2
SSD_GHEADS = SSD_HEADS // SSD_GROUPS
SSD_GW = SSD_GHEADS * SSD_HD
SSD_CONV_GW = SSD_GW + 2 * SSD_STATE
D_FF = 3584
N_EXPERTS = 8
TOP_K = 2
EPS = 1e-6

LOG2E = math.log2(math.e)
LANES = 128
SUBLANES = 8
VMEM_LIMIT = 48 * 1024 * 1024
EXPERT_ROWS = 1024
EXPERT_ROW_CHUNK = 256


def _dot(a, b):
    return jnp.dot(a, b, preferred_element_type=F32)


def _dot_nt(a, b):
    return lax.dot_general(a, b, (((1,), (1,)), ((), ())), preferred_element_type=F32)


def _split(x, pieces):
    out = []
    for _ in range(pieces):
        p = x.astype(BF16)
        out.append(p)
        x = x - p.astype(F32)
    return out


def _dot_split_l(x, w, pieces):
    return sum(_dot(p, w) for p in _split(x, pieces))


def _dot_split_r(w, x, pieces):
    return sum(_dot(w, p) for p in _split(x, pieces))


def _sigmoid(x):
    return lax.logistic(x)


def _rms(x, g):
    ms = jnp.mean(x * x, axis=-1, keepdims=True)
    return x * lax.rsqrt(ms + EPS) * g


def _rms_rows_to(h_ref, x_ref, g_ref, rows):
    step = min(rows, 256)
    for r in range(0, rows, step):
        h_ref[r:r + step, :] = _rms(x_ref[r:r + step, :], g_ref[...]).astype(h_ref.dtype)


def _params(*sem):
    return pltpu.CompilerParams(dimension_semantics=sem, vmem_limit_bytes=VMEM_LIMIT)


def _normmm_kernel(x_ref, g_ref, *refs, rc, tns):
    w_refs, o_refs = refs[:len(tns)], refs[len(tns):]
    tm = x_ref.shape[0]

    def norm(r):
        return _rms(x_ref[r * rc:(r + 1) * rc, :], g_ref[...]).astype(BF16)

    def matmuls(r, h):
        for w_ref, o_ref, tn in zip(w_refs, o_refs, tns):
            for j in range(w_ref.shape[1] // tn):
                cols = slice(j * tn, (j + 1) * tn)
                o_ref[r * rc:(r + 1) * rc, cols] = _dot(h, w_ref[:, cols]).astype(o_ref.dtype)

    live = {}
    for step in range(tm // rc + 1):
        if step < tm // rc:
            live[step] = norm(step)
        if step >= 1:
            matmuls(step - 1, live.pop(step - 1))


def norm_matmul(x, g, ws, out_dtypes, tm, tns):
    t, d = x.shape
    return pl.pallas_call(
        functools.partial(_normmm_kernel, rc=min(tm, 256), tns=tuple(tns)),
        grid=(t // tm,),
        in_specs=[pl.BlockSpec((tm, d), lambda i: (i, 0)),
                  pl.BlockSpec((1, d), lambda i: (0, 0))]
        + [pl.BlockSpec(w.shape, lambda i: (0, 0), pipeline_mode=pl.Buffered(1)) for w in ws],
        out_specs=[pl.BlockSpec((tm, w.shape[1]), lambda i: (i, 0)) for w in ws],
        out_shape=[jax.ShapeDtypeStruct((t, w.shape[1]), dt) for w, dt in zip(ws, out_dtypes)],
        compiler_params=_params("parallel"),
        name="norm_matmul",
    )(x, g.reshape(1, d), *ws)


def _moba_kernel(q_ref, k_ref, v_ref, tab_ref, o_ref, vt_ref, km_ref, *, nb, n_sel, heads):
    for hh in range(heads):
        lanes = pl.ds(hh * HD, HD)
        _moba_head(q_ref.at[:, lanes], k_ref.at[:, lanes], v_ref.at[:, lanes], tab_ref.at[hh],
                   o_ref.at[:, lanes], vt_ref.at[hh], km_ref.at[hh], nb=nb, n_sel=n_sel)


def _moba_head(q_ref, k_ref, v_ref, tab_ref, o_ref, vt_ref, km_ref, *, nb, n_sel):
    blk = MOBA_BLOCK
    scale = HD ** -0.5 * LOG2E
    nbp = km_ref.shape[0]
    km_ref[...] = jnp.zeros_like(km_ref)
    for j in range(nb):
        rows = slice(j * blk, (j + 1) * blk)
        vt_ref[:, rows] = v_ref[rows, :].astype(F32).T.astype(BF16)
        km_ref[j:j + 1, :] = jnp.mean(k_ref[rows, :].astype(F32), axis=0, keepdims=True)
    km = km_ref[...]
    km_hi = km.astype(BF16)
    km_lo = (km - km_hi.astype(F32)).astype(BF16)
    q_all = q_ref[...]
    gate = _dot_nt(km_hi, q_all) + _dot_nt(km_lo, q_all)
    sub = lax.broadcasted_iota(jnp.int32, (nbp, blk), 0)
    bias_far = tab_ref[2, 0:1, :]

    def scores(i):
        return _dot_nt(k_ref[0:(i + 1) * blk, :], q_ref[i * blk:(i + 1) * blk, :])

    def softmax(i, s_all):
        rows = slice(i * blk, (i + 1) * blk)
        if i > 0:
            valid = sub < i
            gm = jnp.where(valid, gate[:, rows], -jnp.inf)
            rank = jnp.zeros((nbp, blk), F32)
            for jp in range(i):
                row = gm[jp:jp + 1, :]
                beats = (row > gm) | ((row == gm) & (sub > jp))
                rank = rank + jnp.where(beats, 1.0, 0.0)
            selm = jnp.where(valid & (rank < n_sel), 0.0, -jnp.inf)
        bands = []
        for j in range(i + 1):
            sj = s_all[j * blk:(j + 1) * blk, :] * scale
            if j == i:
                sj = sj + tab_ref[0]
            elif j == i - 1:
                sj = sj + (tab_ref[1] + selm[j:j + 1, :])
            else:
                sj = sj + (bias_far + selm[j:j + 1, :])
            bands.append(sj)
        s = jnp.concatenate(bands, axis=0) if i > 0 else bands[0]
        m = jnp.max(s, axis=0, keepdims=True)
        return jnp.exp2(s - m).astype(BF16)

    def output(i, p):
        nk = (i + 1) * blk
        l = _dot(jnp.ones((SUBLANES, nk), BF16), p)[0:1, :]
        acc = _dot(vt_ref[:, 0:nk], p)
        o_ref[i * blk:(i + 1) * blk, :] = (acc / l).T.astype(o_ref.dtype)

    s_live, p_live = {}, {}
    units = [sorted({u, nb - 1 - u}) for u in range((nb + 1) // 2)]
    for step in range(len(units) + 2):
        if step < len(units):
            for i in units[step]:
                s_live[i] = scores(i)
        if 1 <= step <= len(units):
            for i in units[step - 1]:
                p_live[i] = softmax(i, s_live.pop(i))
        if step >= 2:
            for i in units[step - 2]:
                output(i, p_live.pop(i))


def _t5_bucket_idx(dist):
    n = jnp.maximum(dist, 0)
    max_exact = REL_BUCKETS // 2
    large = max_exact + (jnp.log(jnp.maximum(n, 1).astype(F32) / max_exact)
                         / math.log(REL_MAX_DIST / max_exact)
                         * (REL_BUCKETS - max_exact)).astype(jnp.int32)
    large = jnp.minimum(large, REL_BUCKETS - 1)
    return jnp.where(n < max_exact, n, large)


def _bias_table_kernel(rb_ref, idx_ref, o_ref):
    h = pl.program_id(0)
    for t in range(3):
        idx = idx_ref[t]
        acc = jnp.full(idx.shape, -jnp.inf, F32)
        for b in range(REL_BUCKETS):
            acc = jnp.where(idx == b, rb_ref[h, b], acc)
        o_ref[t] = acc * LOG2E


def _moba_bias_tables(rel_bias):
    blk = MOBA_BLOCK
    loc = jnp.arange(blk)
    d0 = loc[None, :] - loc[:, None]
    idx = jnp.stack([jnp.where(d0 >= 0, _t5_bucket_idx(d0), -1),
                     _t5_bucket_idx(d0 + blk),
                     _t5_bucket_idx(d0 + 2 * blk)]).astype(jnp.int32)
    return pl.pallas_call(
        _bias_table_kernel,
        grid=(MOBA_HEADS,),
        in_specs=[pl.BlockSpec(memory_space=pltpu.SMEM),
                  pl.BlockSpec((3, blk, blk), lambda h: (0, 0, 0))],
        out_specs=pl.BlockSpec((None, 3, blk, blk), lambda h: (h, 0, 0, 0)),
        out_shape=jax.ShapeDtypeStruct((MOBA_HEADS, 3, blk, blk), F32),
        compiler_params=_params("parallel"),
        name="moba_bias_tables",
    )(rel_bias.T.astype(F32), idx)


def moba_attention(proj, tabs, bsz, s):
    assert s % MOBA_BLOCK == 0
    nb = s // MOBA_BLOCK
    nbp = -(-nb // 8) * 8
    n_sel = min(MOBA_TOPK, nb - 1)
    blk = MOBA_BLOCK

    hps = MOBA_HEADS_PER_STEP
    groups = MOBA_HEADS // hps

    def col(off):
        return pl.BlockSpec((None, s, hps * HD), lambda h, b: (b, 0, off + h))

    return pl.pallas_call(
        functools.partial(_moba_kernel, nb=nb, n_sel=n_sel, heads=hps),
        grid=(groups, bsz),
        in_specs=[col(0), col(groups), col(2 * groups),
                  pl.BlockSpec((hps, 3, blk, blk), lambda h, b: (h, 0, 0, 0))],
        out_specs=pl.BlockSpec((None, s, hps * HD), lambda h, b: (b, 0, h)),
        out_shape=jax.ShapeDtypeStruct((bsz, s, MOBA_WIDTH), BF16),
        scratch_shapes=[pltpu.VMEM((hps, HD, s), BF16),
                        pltpu.VMEM((hps, nbp, HD), F32)],
        compiler_params=_params("parallel", "parallel"),
        name="moba_attention",
    )(proj, proj, proj, tabs)


def _memattn_kernel(q_ref, kv_ref, o_ref, *, n_chunks, qc):
    scale = HD ** -0.5 * LOG2E
    ks =[kv_ref[:, h * HD:(h + 1) * HD] for h in range(MEM_HEADS)]
    vts = [kv_ref[:, MEM_WIDTH + h * HD:MEM_WIDTH + (h + 1) * HD].astype(F32).T.astype(BF16)
           for h in range(MEM_HEADS)]
    ones = jnp.ones((SUBLANES, kv_ref.shape[0]), BF16)

    def scores(h, c):
        return _dot_nt(ks[h], q_ref[c * qc:(c + 1) * qc, h * HD:(h + 1) * HD])

    def softmax(s):
        s = s * scale
        return jnp.exp2(s - jnp.max(s, axis=0, keepdims=True)).astype(BF16)

    def output(h, c, p):
        l = _dot(ones, p)[0:1, :]
        o_ref[c * qc:(c + 1) * qc, h * HD:(h + 1) * HD] = (_dot(vts[h], p) / l).T.astype(o_ref.dtype)

    work = [(h, c) for h in range(MEM_HEADS) for c in range(n_chunks)]
    s_live, p_live = {}, {}
    for step in range(len(work) + 2):
        if step < len(work):
            s_live[step] = scores(*work[step])
        if 1 <= step <= len(work):
            p_live[step - 1] = softmax(s_live.pop(step - 1))
        if step >= 2:
            output(*work[step - 2], p_live.pop(step - 2))


def memory_attention(proj, q_col, kv, bsz, s):
    m_len = kv.shape[1]
    qc = 256
    assert q_col % MEM_WIDTH == 0
    return pl.pallas_call(
        functools.partial(_memattn_kernel, n_chunks=s // qc, qc=qc),
        grid=(bsz,),
        in_specs=[pl.BlockSpec((None, s, MEM_WIDTH), lambda b: (b, 0, q_col // MEM_WIDTH)),
                  pl.BlockSpec((None, m_len, 2 * MEM_WIDTH), lambda b: (b, 0, 0))],
        out_specs=pl.BlockSpec((None, s, MEM_WIDTH), lambda b: (b, 0, 0)),
        out_shape=jax.ShapeDtypeStruct((bsz, s, MEM_WIDTH), BF16),
        compiler_params=_params("parallel"),
        name="memory_attention",
    )(proj, kv)


def _outproj_kernel(x_ref, ya_ref, yb_ref, w_ref, o_ref):
    ka = ya_ref.shape[1]
    o_ref[...] = x_ref[...] + _dot(ya_ref[...], w_ref[0:ka, :]) + _dot(yb_ref[...], w_ref[ka:, :])


def out_projection(x, ya, yb, w, tm):
    t, d = x.shape
    ka, kb = ya.shape[1], yb.shape[1]
    return pl.pallas_call(
        _outproj_kernel,
        grid=(t // tm,),
        in_specs=[pl.BlockSpec((tm, d), lambda i: (i, 0)),
                  pl.BlockSpec((tm, ka), lambda i: (i, 0)),
                  pl.BlockSpec((tm, kb), lambda i: (i, 0)),
                  pl.BlockSpec((ka + kb, d), lambda i: (0, 0))],
        out_specs=pl.BlockSpec((tm, d), lambda i: (i, 0)),
        out_shape=jax.ShapeDtypeStruct((t, d), F32),
        compiler_params=_params("parallel"),
        name="out_projection",
    )(x, ya, yb, w)


def _swiglu(h, wg, wu, wd):
    a = _dot(h, wg)
    u = _dot(h, wu)
    return _dot((a * _sigmoid(a) * u).astype(BF16), wd)


def _swiglu_accumulate(h, wg, wu, wd, acc_ref):
    acc_ref[...] += _swiglu(h, wg, wu, wd)


def _ffn_kernel(x_ref, g_ref, wg_ref, wu_ref, wd_ref, o_ref, h_ref, *, tm):
    @pl.when(pl.program_id(1) == 0)
    def _():
        _rms_rows_to(h_ref, x_ref, g_ref, tm)
        o_ref[...] = x_ref[...]

    _swiglu_accumulate(h_ref[...], wg_ref[...], wu_ref[...], wd_ref[...], o_ref)


def dense_ffn(x, g, wg, wu, wd, tm, tf):
    t, d = x.shape
    ff = wg.shape[1]
    return pl.pallas_call(
        functools.partial(_ffn_kernel, tm=tm),
        grid=(t // tm, ff // tf),
        in_specs=[pl.BlockSpec((tm, d), lambda i, f: (i, 0)),
                  pl.BlockSpec((1, d), lambda i, f: (0, 0)),
                  pl.BlockSpec((d, tf), lambda i, f: (0, f)),
                  pl.BlockSpec((d, tf), lambda i, f: (0, f)),
                  pl.BlockSpec((tf, d), lambda i, f: (f, 0))],
        out_specs=pl.BlockSpec((tm, d), lambda i, f: (i, 0)),
        out_shape=jax.ShapeDtypeStruct((t, d), F32),
        scratch_shapes=[pltpu.VMEM((tm, d), BF16)],
        compiler_params=_params("parallel", "arbitrary"),
        name="dense_ffn",
    )(x, g.reshape(1, d), wg, wu, wd)


def _ssd_kernel(z_ref, xs_ref, b_ref, c_ref, dt_ref, cw_ref, cb_ref, dtb_ref, alog_ref, dsk_ref,
                gout_ref, exp_ref, o_ref, h_ref, *, nc, unroll, groups):
    gw, n = SSD_GW, SSD_STATE
    chunks = []
    for sg in range(groups):
        wide, narrow, lanes = pl.ds(sg * gw, gw), pl.ds(sg * n, n), pl.ds(sg * LANES, LANES)
        chunks.append(_ssd_group_chunk(
            z_ref.at[:, wide], xs_ref.at[:, wide], b_ref.at[:, narrow], c_ref.at[:, narrow], dt_ref.at[:, lanes],
            cw_ref.at[sg], cb_ref.at[sg], dtb_ref.at[sg], alog_ref.at[sg], dsk_ref.at[sg], gout_ref.at[sg],
            exp_ref, o_ref.at[:, wide], h_ref.at[sg]))

    def body(it, carry):
        work = [(sg, it * unroll + u) for u in range(unroll) for sg in range(groups)]
        live = {}
        for step in range(len(work) + 1):
            if step < len(work):
                sg, c = work[step]
                live[step] = chunks[sg][0](c)
            if step >= 1:
                chunks[work[step - 1][0]][1](*live.pop(step - 1))
        return carry

    lax.fori_loop(0, nc // unroll, body, 0)


def _ssd_group_chunk(z_ref, xs_ref, b_ref, c_ref, dt_ref, cw_ref, cb_ref, dtb_ref, alog_ref, dsk_ref,
                     gout_ref, exp_ref, o_ref, h_ref):
    L = SSD_CHUNK
    gw = SSD_GW
    n = SSD_STATE
    h_ref[...] = jnp.zeros(h_ref.shape, h_ref.dtype)
    a_neg = -jnp.exp(alog_ref[...])
    causal = (lax.broadcasted_iota(jnp.int32, (L, L), 0) >= lax.broadcasted_iota(jnp.int32, (L, L), 1))
    tri = jnp.where(causal, 1.0, 0.0).astype(BF16)
    first_half = lax.broadcasted_iota(jnp.int32, (1, 2 * SSD_HD), 1) < SSD_HD
    expm = exp_ref[...]
    srcs = (xs_ref, b_ref, c_ref)
    sh_r = lax.broadcasted_iota(jnp.int32, (L, 2 * L), 0)
    sh_c = lax.broadcasted_iota(jnp.int32, (L, 2 * L), 1)
    shifts = [jnp.where(sh_c == sh_r + (L - s), 1.0, 0.0).astype(BF16) for s in range(1, SSD_CONV)]

    def local(c):
        r0 = pl.multiple_of(c * L, L)
        rq = pl.multiple_of(jnp.maximum(r0 - L, 0), L)
        cur = jnp.concatenate([ref[pl.ds(r0, L), :] for ref in srcs], axis=1)
        prev = jnp.concatenate([ref[pl.ds(rq, L), :] for ref in srcs], axis=1)
        prev = jnp.where(c > 0, prev, jnp.zeros_like(prev))
        ext = jnp.concatenate([prev, cur], axis=0)
        conv = cb_ref[...] + cw_ref[SSD_CONV - 1:SSD_CONV, :] * cur.astype(F32)
        for s in range(1, SSD_CONV):
            k = SSD_CONV - 1 - s
            conv = conv + cw_ref[k:k + 1, :] * _dot(shifts[s - 1], ext)
        act = conv * _sigmoid(conv)
        xs = act[:, 0:gw]
        bm = act[:, gw:gw + n]
        cm = act[:, gw + n:gw + 2 * n]

        dtr = dt_ref[pl.ds(r0, L), :] + dtb_ref[...]
        dt = jnp.maximum(dtr, 0.0) + jnp.log(1.0 + jnp.exp(-jnp.abs(dtr)))
        la = dt * a_neg
        cs = _dot_split_r(tri, la, 3)
        cs_t = cs.T
        dt_t = dt.T
        bm16 = bm.astype(BF16)
        cm16 = cm.astype(BF16)
        scores = _dot_nt(cm16, bm16)
        from_start = _dot_split_l(jnp.exp(cs), expm, 2)
        w_end = dt * jnp.exp(cs[L - 1:L, :] - cs)
        xs_t = xs.T
        return r0, xs, bm, cm16, cs, cs_t, dt_t, scores, from_start, w_end, xs_t

    def carried(r0, xs, bm, cm16, cs, cs_t, dt_t, scores, from_start, w_end, xs_t):
        y = _dot_nt(cm16, h_ref[...].astype(BF16)) * from_start
        y_in = []
        for e in range(SSD_GHEADS):
            diff = cs[:, e:e + 1] - cs_t[e:e + 1, :]
            dec = jnp.exp(jnp.where(causal, diff, -jnp.inf))
            mm = (scores * dec * dt_t[e:e + 1, :]).astype(BF16)
            if e % 2 == 0:
                mm_even = mm
            else:
                xp = xs[:, (e - 1) * SSD_HD:(e + 1) * SSD_HD]
                rhs = jnp.concatenate([jnp.where(first_half, xp, 0.0), jnp.where(first_half, 0.0, xp)], axis=0)
                y_in.append(_dot(jnp.concatenate([mm_even, mm], axis=1), rhs.astype(BF16)))
            bw = (bm * w_end[:, e:e + 1]).astype(BF16)
            st = _dot(xs_t[e * SSD_HD:(e + 1) * SSD_HD, :].astype(BF16), bw)
            cdec = jnp.exp(cs[L - 1:L, e:e + 1])
            hs = slice(e * SSD_HD, (e + 1) * SSD_HD)
            h_ref[hs, :] = h_ref[hs, :] * cdec + st
        y = y + jnp.concatenate(y_in, axis=1) + xs * dsk_ref[...]
        z = z_ref[pl.ds(r0, L), :].astype(F32)
        u = y * (z * _sigmoid(z))
        ms = jnp.mean(u * u, axis=-1, keepdims=True)
        o_ref[pl.ds(r0, L), :] = (u * lax.rsqrt(ms + EPS) * gout_ref[...]).astype(o_ref.dtype)

    return local, carried


def ssd_mixer(proj, dt_all, cw_g, cb_g, dtb_g, alog_g, dsk_g, gout_g, expand, bsz, s):
    assert s % SSD_CHUNK == 0
    nc = s // SSD_CHUNK
    gps = SSD_GROUPS_PER_STEP
    gw = SSD_GW * gps
    nw = SSD_STATE * gps
    z_blk = 0
    xs_blk = SSD_INNER // gw
    b_blk = 2 * SSD_INNER // nw
    c_blk = b_blk + SSD_GROUPS // gps

    def seq(width, off):
        return pl.BlockSpec((None, s, width), lambda b, g: (b, 0, off + g))

    def par(rows, width):
        return pl.BlockSpec((gps, rows, width), lambda b, g: (g, 0, 0))

    return pl.pallas_call(
        functools.partial(_ssd_kernel, nc=nc, unroll=_pick(nc, (8, 4, 2, 1)), groups=gps),
        grid=(bsz, SSD_GROUPS // gps),
        in_specs=[seq(gw, z_blk), seq(gw, xs_blk), seq(nw, b_blk), seq(nw, c_blk),
                  seq(LANES * gps, 0),
                  par(SSD_CONV, SSD_CONV_GW), par(1, SSD_CONV_GW), par(1, LANES), par(1, LANES),
                  par(1, SSD_GW), par(1, SSD_GW),
                  pl.BlockSpec((LANES, SSD_GW), lambda b, g: (0, 0))],
        out_specs=seq(gw, 0),
        out_shape=jax.ShapeDtypeStruct((bsz, s, SSD_INNER), BF16),
        scratch_shapes=[pltpu.VMEM((gps, SSD_GW, SSD_STATE), F32)],
        compiler_params=_params("parallel", "parallel"),
        name="ssd_mixer",
    )(proj, proj, proj, proj, dt_all, cw_g, cb_g, dtb_g, alog_g, dsk_g, gout_g, expand)


def _ssd_group_params(conv_w, conv_b, dt_bias, a_log, d_skip, g_out):
    g, gh, gw, n = SSD_GROUPS, SSD_GHEADS, SSD_GW, SSD_STATE

    def conv_cols(a):
        xs = a[..., :SSD_INNER].reshape(a.shape[:-1] + (g, gw))
        bb = a[..., SSD_INNER:SSD_INNER + g * n].reshape(a.shape[:-1] + (g, n))
        cc = a[..., SSD_INNER + g * n:].reshape(a.shape[:-1] + (g, n))
        return jnp.moveaxis(jnp.concatenate([xs, bb, cc], axis=-1), -2, 0)

    cw_g = conv_cols(conv_w.astype(F32))
    cb_g = conv_cols(conv_b.astype(F32)[None, :])

    def per_head(a):
        return jnp.pad(a.astype(F32).reshape(g, 1, gh), ((0, 0), (0, 0), (0, LANES - gh)))

    dsk_g = jnp.repeat(d_skip.astype(F32), SSD_HD).reshape(g, 1, gw)
    gout_g = g_out.astype(F32).reshape(g, 1, gw)
    expand = (jnp.arange(LANES)[:, None] == (jnp.arange(gw)[None, :] // SSD_HD)).astype(BF16)
    return cw_g, cb_g, per_head(dt_bias), per_head(a_log), dsk_g, gout_g, expand


def _router_kernel(x_ref, g_ref, wrt_ref, br_ref, h_ref, route_ref, runs_ref, cnt_ref, carry_ref, *, tm):
    @pl.when(pl.program_id(0) == 0)
    def _():
        carry_ref[...] = jnp.zeros_like(carry_ref)

    ne = N_EXPERTS
    hn = _rms(x_ref[...], g_ref[...])
    h_hi = hn.astype(BF16)
    h_ref[...] = h_hi
    h_lo = (hn - h_hi.astype(F32)).astype(BF16)
    wr = wrt_ref[...]
    w_hi = wr.astype(BF16)
    w_lo = (wr - w_hi.astype(F32)).astype(BF16)
    logits = _dot_nt(w_hi, h_hi) + _dot_nt(w_hi, h_lo) + _dot_nt(w_lo, h_hi) + br_ref[:, 0:1]
    sub = lax.broadcasted_iota(jnp.int32, (ne, tm), 0)
    l1 = jnp.max(logits, axis=0, keepdims=True)
    i1 = jnp.min(jnp.where(logits == l1, sub, ne), axis=0, keepdims=True)
    rest = jnp.where(sub == i1, -jnp.inf, logits)
    l2 = jnp.max(rest, axis=0, keepdims=True)
    i2 = jnp.min(jnp.where(rest == l2, sub, ne), axis=0, keepdims=True)
    e2 = jnp.exp(l2 - l1)
    g1 = 1.0 / (1.0 + e2)
    g2 = e2 / (1.0 + e2)
    sel = jnp.where((sub == i1) | (sub == i2), 1.0, 0.0)
    before = (lax.broadcasted_iota(jnp.int32, (tm, tm), 0) < lax.broadcasted_iota(jnp.int32, (tm, tm), 1))
    prefix = _dot(sel.astype(BF16), jnp.where(before, 1.0, 0.0).astype(BF16))
    lane = lax.broadcasted_iota(jnp.int32, (ne, LANES), 1)
    subl = lax.broadcasted_iota(jnp.int32, (ne, LANES), 0)
    n_e = (jnp.sum(sel, axis=1, keepdims=True) + jnp.zeros((ne, LANES), F32)).astype(jnp.int32)
    n_pad = (n_e + (SUBLANES - 1)) & (-SUBLANES)
    off = jnp.zeros((ne, LANES), jnp.int32)
    for j in range(ne - 1):
        off = off + jnp.where(subl > j, n_pad[j:j + 1, :], 0)
    slot = prefix + off[:, 0:1].astype(F32)
    p1 = jnp.sum(jnp.where(sub == i1, slot, 0.0), axis=0, keepdims=True)
    p2 = jnp.sum(jnp.where(sub == i2, slot, 0.0), axis=0, keepdims=True)
    route_ref[...] = jnp.where(sub == 0, p1, jnp.where(sub == 1, p2, jnp.where(
        sub == 2, g1, jnp.where(sub == 3, g2, 0.0))))
    carry = carry_ref[...]
    runs_ref[...] = jnp.where(lane == 0, off, jnp.where(lane == 1, carry, jnp.where(lane == 2, n_pad, 0)))
    carry_ref[...] = carry + n_pad
    cnt_ref[...] = carry + n_pad


def moe_router(x, g, w_router, b_router, tm):
    t, d = x.shape
    ne = N_EXPERTS
    wrt = w_router.astype(F32).T
    br = jnp.broadcast_to(b_router.astype(F32)[:, None], (ne, LANES))
    return pl.pallas_call(
        functools.partial(_router_kernel, tm=tm),
        grid=(t // tm,),
        in_specs=[pl.BlockSpec((tm, d), lambda i: (i, 0)),
                  pl.BlockSpec((1, d), lambda i: (0, 0)),
                  pl.BlockSpec((ne, d), lambda i: (0, 0)),
                  pl.BlockSpec((ne, LANES), lambda i: (0, 0))],
        out_specs=[pl.BlockSpec((tm, d), lambda i: (i, 0)),
                   pl.BlockSpec((ne, tm), lambda i: (0, i)),
                   pl.BlockSpec((ne, LANES), lambda i: (i, 0)),
                   pl.BlockSpec((ne, LANES), lambda i: (0, 0))],
        out_shape=[jax.ShapeDtypeStruct((t, d), BF16),
                   jax.ShapeDtypeStruct((ne, t), F32),
                   jax.ShapeDtypeStruct((t // tm * ne, LANES), jnp.int32),
                   jax.ShapeDtypeStruct((ne, LANES), jnp.int32)],
        scratch_shapes=[pltpu.VMEM((ne, LANES), jnp.int32)],
        compiler_params=_params("arbitrary"),
        name="moe_router",
    )(x, g.reshape(1, d), wrt, br)


def _pieces(length, lo_bit, hi_bit):
    for b in range(lo_bit, hi_bit):
        n = 1 << b
        yield n, length & (n - 1), ((length >> b) & 1) == 1


def _run_copies(tab_ref, tile, tile_ref, buf_ref, sem, tm, to_buf, wait):
    for e in range(N_EXPERTS):
        base = (tile * N_EXPERTS + e) * 3
        off, dst, rows = tab_ref[base], tab_ref[base + 1], tab_ref[base + 2]
        for n, lo, present in _pieces(rows, SUBLANES.bit_length() - 1, tm.bit_length()):
            in_tile = tile_ref.at[pl.ds(pl.multiple_of(off + lo, SUBLANES), n), :]
            in_buf = buf_ref.at[pl.ds(pl.multiple_of(dst + lo, SUBLANES), n), :]
            cp = pltpu.make_async_copy(in_tile, in_buf, sem) if to_buf else pltpu.make_async_copy(in_buf, in_tile, sem)

            @pl.when(present)
            def _():
                if wait:
                    cp.wait()
                else:
                    cp.start()


def _dispatch_kernel(tab_ref, meta_ref, route_ref, h_ref, xb_ref, srt_ref, zero_ref, sem, zsem, *, tm, n_tiles):
    i = pl.program_id(0)
    nt = pl.num_programs(0)
    slot = i % 2
    rs = srt_ref.shape[1]

    def runs(tile, slot, wait):
        _run_copies(tab_ref, tile, srt_ref.at[slot], xb_ref, sem.at[slot], tm, True, wait)

    @pl.when(i >= 2)
    def _():
        runs(i - 2, slot, True)

    pos = route_ref[0:2, :].astype(jnp.int32)
    row = lax.broadcasted_iota(jnp.int32, (rs, tm), 0)
    onehot = jnp.where((row == pos[0:1, :]) | (row == pos[1:2, :]), 1.0, 0.0).astype(BF16)
    srt_ref[slot] = _dot(onehot, h_ref[...])
    runs(i, slot, False)

    @pl.when(i == nt - 1)
    def _():
        zero_ref[...] = jnp.zeros_like(zero_ref)
        zrows = zero_ref.shape[0]
        per = EXPERT_ROWS // zrows
        first_unused = meta_ref[2 * N_EXPERTS] * per
        for wait in (False, True):
            def go(cp, cond):
                @pl.when(cond)
                def _():
                    if wait:
                        cp.wait()
                    else:
                        cp.start()

            for e in range(N_EXPERTS):
                c = meta_ref[e]
                first = meta_ref[N_EXPERTS + e]
                tail = (-c) & (EXPERT_ROWS - 1)
                for n, lo, present in _pieces(tail, SUBLANES.bit_length() - 1, EXPERT_ROWS.bit_length() - 1):
                    row0 = pl.multiple_of(first + c + lo, SUBLANES)
                    go(pltpu.make_async_copy(zero_ref.at[pl.ds(0, n), :], xb_ref.at[pl.ds(row0, n), :], zsem), present)
            for j in range(N_EXPERTS * per):
                row0 = pl.multiple_of(jnp.minimum(first_unused + j, n_tiles * per - 1) * zrows, zrows)
                go(pltpu.make_async_copy(zero_ref, xb_ref.at[pl.ds(row0, zrows), :], zsem),
                   first_unused + j < n_tiles * per)

        @pl.when(i >= 1)
        def _():
            runs(i - 1, 1 - slot, True)

        runs(i, slot, True)


def moe_dispatch(tab, meta, route, h, tm, n_tiles):
    t, d = h.shape
    ne = route.shape[0]
    rs = TOP_K * tm + LANES
    return pl.pallas_call(
        functools.partial(_dispatch_kernel, tm=tm, n_tiles=n_tiles),
        grid=(t // tm,),
        in_specs=[pl.BlockSpec(memory_space=pltpu.SMEM),
                  pl.BlockSpec(memory_space=pltpu.SMEM),
                  pl.BlockSpec((ne, tm), lambda i: (0, i)),
                  pl.BlockSpec((tm, d), lambda i: (i, 0))],
        out_specs=pl.BlockSpec(memory_space=pl.ANY),
        out_shape=jax.ShapeDtypeStruct((n_tiles * EXPERT_ROWS, d), F32),
        scratch_shapes=[pltpu.VMEM((2, rs, d), F32),
                        pltpu.VMEM((EXPERT_ROWS // 2, d), F32),
                        pltpu.SemaphoreType.DMA((2,)), pltpu.SemaphoreType.DMA(())],
        compiler_params=_params("arbitrary"),
        name="moe_dispatch",
    )(tab, meta, route, h)


def _expert_kernel(te_ref, nr_ref, xb_ref, wg_ref, wu_ref, wd_ref, o_ref):
    rows = nr_ref[pl.program_id(0)]
    r, rc = EXPERT_ROWS, EXPERT_ROW_CHUNK
    first = pl.program_id(1) == 0
    full = rows == r

    def full_tile():
        return _swiglu(xb_ref[...].astype(BF16), wg_ref[...].astype(BF16), wu_ref[...].astype(BF16),
                       wd_ref[...].astype(BF16))

    @pl.when(jnp.logical_and(full, first))
    def _():
        o_ref[...] = full_tile()

    @pl.when(jnp.logical_and(full, jnp.logical_not(first)))
    def _():
        o_ref[...] += full_tile()

    @pl.when(jnp.logical_and(first, jnp.logical_not(full)))
    def _():
        o_ref[...] = jnp.zeros_like(o_ref)

    @pl.when(jnp.logical_and(rows > 0, rows < r))
    def _():
        wg, wu, wd = wg_ref[...].astype(BF16), wu_ref[...].astype(BF16), wd_ref[...].astype(BF16)
        for j in range(r // rc):
            @pl.when(j * rc < rows)
            def _():
                _swiglu_accumulate(xb_ref[j * rc:(j + 1) * rc, :].astype(BF16), wg, wu, wd,
                                   o_ref.at[j * rc:(j + 1) * rc, :])


def expert_ffn(tile_expert, tile_rows, xb, wg, wu, wd, tf):
    n_rows, d = xb.shape
    ff = wg.shape[2]
    r = EXPERT_ROWS
    nf = ff // tf

    def wcol(c, f, te, nr):
        return (te[c], 0, jnp.where(nr[c] > 0, f, nf - 1))

    def wrow(c, f, te, nr):
        return (te[c], jnp.where(nr[c] > 0, f, nf - 1), 0)

    grid_spec = pltpu.PrefetchScalarGridSpec(
        num_scalar_prefetch=2,
        grid=(n_rows // r, nf),
        in_specs=[pl.BlockSpec((r, d), lambda c, f, te, nr: (c, 0)),
                  pl.BlockSpec((None, d, tf), wcol),
                  pl.BlockSpec((None, d, tf), wcol),
                  pl.BlockSpec((None, tf, d), wrow)],
        out_specs=pl.BlockSpec((r, d), lambda c, f, te, nr: (c, 0)),
    )
    return pl.pallas_call(
        _expert_kernel,
        grid_spec=grid_spec,
        out_shape=jax.ShapeDtypeStruct((n_rows, d), F32),
        compiler_params=_params("parallel", "arbitrary"),
        name="expert_ffn",
    )(tile_expert, tile_rows, xb, wg, wu, wd)


def _combine_kernel(tab_ref, route_ref, x_ref, g_ref, yb_ref, o_ref, srt_ref, sem, *, tm):
    i = pl.program_id(0)
    nt = pl.num_programs(0)
    slot = i % 2
    rs = srt_ref.shape[1]

    def runs(tile, slot, wait):
        _run_copies(tab_ref, tile, srt_ref.at[slot], yb_ref, sem.at[slot], tm, False, wait)

    @pl.when(i == 0)
    def _():
        srt_ref[...] = jnp.zeros_like(srt_ref)
        runs(0, 0, False)

    @pl.when(i + 1 < nt)
    def _():
        runs(i + 1, 1 - slot, False)

    runs(i, slot, True)
    aux = route_ref[...]
    aux_t = jnp.concatenate([aux, jnp.zeros((LANES - aux.shape[0], tm), F32)], axis=0).T
    srt = srt_ref[slot].astype(BF16)
    col = lax.broadcasted_iota(jnp.int32, (tm, rs), 1)
    ys = []
    for k in range(TOP_K):
        onehot = jnp.where(col == aux_t[:, k:k + 1].astype(jnp.int32), 1.0, 0.0).astype(BF16)
        ys.append(aux_t[:, TOP_K + k:TOP_K + k + 1] * _dot(onehot, srt))
    o_ref[...] = _rms(x_ref[...] + (ys[0] + ys[1]), g_ref[...])


def combine_final(tab, route, x, g, yb, tm):
    t, d = x.shape
    ne = route.shape[0]
    rs = TOP_K * tm + LANES
    return pl.pallas_call(
        functools.partial(_combine_kernel, tm=tm),
        grid=(t // tm,),
        in_specs=[pl.BlockSpec(memory_space=pltpu.SMEM),
                  pl.BlockSpec((ne, tm), lambda i: (0, i)),
                  pl.BlockSpec((tm, d), lambda i: (i, 0)),
                  pl.BlockSpec((1, d), lambda i: (0, 0)),
                  pl.BlockSpec(memory_space=pl.ANY)],
        out_specs=pl.BlockSpec((tm, d), lambda i: (i, 0)),
        out_shape=jax.ShapeDtypeStruct((t, d), F32),
        scratch_shapes=[pltpu.VMEM((2, rs, d), F32), pltpu.SemaphoreType.DMA((2,))],
        compiler_params=_params("arbitrary"),
        name="combine_final",
    )(tab, route, x, g.reshape(1, d), yb)


def _pick(n, prefs):
    for p in prefs:
        if n % p == 0:
            return p
    return n


class _Tiles(NamedTuple):
    in_rows: int
    proj_rows: int
    out_rows: int
    mem_rows: int
    moe_rows: int
    moba_cols: int
    ssd_cols: int
    ffn_cols: int
    expert_cols: int


def _choose_tiles(t, mem_rows):
    return _Tiles(in_rows=_pick(t, (1024, 512, 256)), proj_rows=_pick(t, (512, 256)),
                  out_rows=_pick(t, (1024, 512, 256)),
                  mem_rows=_pick(mem_rows, (1024, 512, 256)), moe_rows=_pick(t, (512, 256)),
                  moba_cols=1280, ssd_cols=2048, ffn_cols=1792, expert_cols=512)


class _ExpertLayout(NamedTuple):
    n_tiles: int
    tile_expert: Any
    tile_rows: Any
    run_table: Any
    meta: Any


def _expert_layout(runs, cnt, t, tm):
    r, rc = EXPERT_ROWS, EXPERT_ROW_CHUNK
    nt = t // tm
    n_tiles = -(-(t * TOP_K + nt * N_EXPERTS * (SUBLANES - 1)) // r) + N_EXPERTS
    tiles_per = (cnt + r - 1) // r
    ends = jnp.cumsum(tiles_per)
    first_row = (ends - tiles_per) * r
    n_used = ends[-1].astype(jnp.int32)
    step = jnp.minimum(jnp.arange(n_tiles, dtype=jnp.int32), n_used - 1)
    tile_expert = jnp.sum(step[:, None] >= ends[None, :], axis=1).astype(jnp.int32)
    rows_left = jnp.sum(jnp.where(tile_expert[:, None] == jnp.arange(N_EXPERTS)[None, :],
                                  (cnt + first_row)[None, :], 0), axis=1) - step * r
    tile_rows = jnp.where(jnp.arange(n_tiles) < n_used, jnp.clip((rows_left + rc - 1) // rc * rc, 0, r), 0)
    tab = runs.reshape(nt, N_EXPERTS, LANES)[:, :, :3]
    tab = tab.at[:, :, 1].add(first_row[None, :]).reshape(-1).astype(jnp.int32)
    meta = jnp.concatenate([cnt, first_row, n_used.reshape(1)]).astype(jnp.int32)
    return _ExpertLayout(n_tiles, tile_expert, tile_rows.astype(jnp.int32), tab, meta)


def kernel(x, mem, g_mix, g_mem, w_in_moba, w_in_ssd, w_mem_kv, w_out, rel_bias, conv_w, conv_b, dt_bias, a_log, d_skip, g_ssd_out, g_ffn, w_ffn_gate, w_ffn_up, w_ffn_down, w_router, b_router, w_exp_gate, w_exp_up, w_exp_down, g_final):
    bsz, s, d = x.shape
    t = bsz * s
    m_len = mem.shape[1]
    tiles = _choose_tiles(t, bsz * m_len)
    xf = x.reshape(t, d).astype(F32)
    memf = mem.reshape(bsz * m_len, d).astype(F32)

    proj0, = norm_matmul(xf, g_mix[0], [w_in_moba[0].astype(BF16)], [BF16], tiles.in_rows, [tiles.moba_cols])
    kv0, = norm_matmul(memf, g_mem[0], [w_mem_kv[0].astype(BF16)], [BF16], tiles.mem_rows, [2 * MEM_WIDTH])
    proj0 = proj0.reshape(bsz, s, -1)
    y_tok = moba_attention(proj0, _moba_bias_tables(rel_bias), bsz, s)
    y_mem = memory_attention(proj0, 3 * MOBA_WIDTH, kv0.reshape(bsz, m_len, -1), bsz, s)
    x1 = out_projection(xf, y_tok.reshape(t, -1), y_mem.reshape(t, -1), w_out[0].astype(BF16), tiles.out_rows)
    x1 = dense_ffn(x1, g_ffn[0], w_ffn_gate[0].astype(BF16), w_ffn_up[0].astype(BF16),
                   w_ffn_down[0].astype(BF16), tiles.proj_rows, tiles.ffn_cols)

    n_zx = SSD_INNER + SSD_INNER + 2 * SSD_GROUPS * SSD_STATE
    w1 = w_in_ssd[0]
    w_zx = w1[:, :n_zx].astype(BF16)
    w_qm = w1[:, n_zx + SSD_HEADS:].astype(BF16)
    w_dt = jnp.pad(w1[:, n_zx:n_zx + SSD_HEADS].reshape(d, SSD_GROUPS, SSD_GHEADS),
                   ((0, 0), (0, 0), (0, LANES - SSD_GHEADS))).reshape(d, SSD_GROUPS * LANES).astype(BF16)
    proj1, q_mem, dt_all = norm_matmul(x1, g_mix[1], [w_zx, w_qm, w_dt], [BF16, BF16, F32], tiles.in_rows,
                                       [tiles.ssd_cols, MEM_WIDTH, SSD_GROUPS * LANES])
    proj1, q_mem, dt_all = (a.reshape(bsz, s, -1) for a in (proj1, q_mem, dt_all))
    kv1, = norm_matmul(memf, g_mem[1], [w_mem_kv[1].astype(BF16)], [BF16], tiles.mem_rows, [2 * MEM_WIDTH])
    y_tok = ssd_mixer(proj1, dt_all, *_ssd_group_params(conv_w[0], conv_b[0], dt_bias[0], a_log[0],
                                                         d_skip[0], g_ssd_out[0]), bsz, s)
    y_mem = memory_attention(q_mem, 0, kv1.reshape(bsz, m_len, -1), bsz, s)
    x2 = out_projection(x1, y_tok.reshape(t, -1), y_mem.reshape(t, -1), w_out[1].astype(BF16), tiles.out_rows)

    h, route, runs, counts = moe_router(x2, g_ffn[1], w_router[0], b_router[0], tiles.moe_rows)
    lay = _expert_layout(runs, counts[:, 0], t, tiles.moe_rows)
    xb = moe_dispatch(lay.run_table, lay.meta, route, h, tiles.moe_rows, lay.n_tiles)
    yb = expert_ffn(lay.tile_expert, lay.tile_rows, xb, w_exp_gate[0], w_exp_up[0], w_exp_down[0], tiles.expert_cols)
    out = combine_final(lay.run_table, route, x2, g_final, yb, tiles.moe_rows)
    return out.reshape(bsz, s, d).astype(x.dtype)
```
